```python
import math
import jax, jax.numpy as jnp
from jax import lax
import numpy as np

D_MODEL = 1024
BATCH = 8
SEQ = 2048
DEPTH = 4
DEC_BATCH = 128
DEC_SEQ = 1
PAST_LEN = 16384
PAGE_SIZE = 128

N_BRANCH = 4
BR_WIDTH = D_MODEL // N_BRANCH
HEAD_DIM = 64
N_HEADS = BR_WIDTH // HEAD_DIM
CONV_W = 4
CHUNK = 64
GLA_GATE_RANK = 16
GLA_GATE_NORM = 16.0
S5_GROUP = 16
S5_GROUPS = BR_WIDTH // S5_GROUP
S5_P = 64
D_FF = -(-8 * D_MODEL // (3 * 256)) * 256
EPS = 1e-6
SPLIT_SIZES = (3 * BR_WIDTH, N_HEADS, N_HEADS, BR_WIDTH,
               BR_WIDTH, BR_WIDTH, BR_WIDTH, GLA_GATE_RANK, BR_WIDTH,
               BR_WIDTH,
               BR_WIDTH, BR_WIDTH, BR_WIDTH, BR_WIDTH,
               N_BRANCH * D_MODEL)
N_IN = sum(SPLIT_SIZES)
F32 = jnp.float32

kernel_name = 'hybrid_gdn_gla_s5_hgrn2_decode_step'


def rmsnorm(x, w):
    xf = x.astype(F32)
    y = xf * lax.rsqrt(jnp.mean(xf * xf, axis=-1, keepdims=True) + EPS)
    return (y * w.astype(F32)).astype(x.dtype)


def l2norm(x):
    return x * lax.rsqrt(jnp.sum(x * x, axis=-1, keepdims=True) + EPS)


def to_heads(t):
    b, l, _ = t.shape
    return t.reshape(b, l, N_HEADS, HEAD_DIM).transpose(0, 2, 1, 3)


def gated_head_norm(o, gate, w):
    b, h, l, d = o.shape
    o = o * lax.rsqrt(jnp.mean(o * o, axis=-1, keepdims=True) + EPS)
    o = o.transpose(0, 2, 1, 3).reshape(b, l, h * d)
    return o * w.astype(F32) * jax.nn.silu(gate.astype(F32))


def chunk_len(l):
    return CHUNK if l % CHUNK == 0 else l


def to_chunks(t, c):
    b, h, l = t.shape[:3]
    return jnp.moveaxis(t.reshape((b, h, l // c, c) + t.shape[3:]), 2, 0)


def from_chunks(t):
    n, b, h, c, d = t.shape
    return jnp.moveaxis(t, 0, 2).reshape(b, h, n * c, d)


def split_cols(p):
    out, off = [], 0
    for n in SPLIT_SIZES:
        out.append(p[..., off:off + n])
        off += n
    return out


def short_conv(x, buf, w):
    l = x.shape[1]
    xp = jnp.concatenate([buf.astype(x.dtype), x], axis=1)
    xf, wf = xp.astype(F32), w.astype(F32)
    y = sum(xf[:, j:j + l] * wf[j] for j in range(CONV_W))
    return jax.nn.silu(y), xp[:, l:]


def gated_delta_rule(q, k, v, g, beta, s0):
    dv = v.shape[-1]
    c = chunk_len(q.shape[2])
    q, k, v, g, beta = (to_chunks(t, c) for t in (q, k, v, g, beta))
    gam = jnp.cumsum(g, axis=-1)
    idx = jnp.arange(c)
    incl = idx[:, None] >= idx[None, :]
    strict = idx[:, None] > idx[None, :]
    decay = jnp.exp(jnp.where(incl, gam[..., :, None] - gam[..., None, :], -jnp.inf))
    kk = jnp.einsum('nbhid,nbhjd->nbhij', k, k)
    tmat = jnp.where(strict, beta[..., :, None] * kk * decay, 0.0) + jnp.eye(c, dtype=F32)
    rhs = jnp.concatenate([beta[..., None] * v, (beta * jnp.exp(gam))[..., None] * k], axis=-1)
    sol = lax.linalg.triangular_solve(tmat, rhs, left_side=True, lower=True, unit_diagonal=True)
    u_v, w_k = sol[..., :dv], sol[..., dv:]
    attn = jnp.einsum('nbhid,nbhjd->nbhij', q, k) * decay
    dec_last = jnp.exp(gam[..., -1:] - gam)
    g_last = jnp.exp(gam[..., -1])

    def step(s, xs):
        qc, kc, uvc, wkc, ac, gc, dl, gl = xs
        u = uvc - jnp.einsum('bhik,bhkv->bhiv', wkc, s)
        o = jnp.exp(gc)[..., None] * jnp.einsum('bhik,bhkv->bhiv', qc, s) + jnp.einsum('bhij,bhjv->bhiv', ac, u)
        s_new = gl[..., None, None] * s + jnp.einsum('bhjk,bhjv->bhkv', kc * dl[..., None], u)
        return s_new, o

    s_fin, o = lax.scan(step, s0, (q, k, u_v, w_k, attn, gam, dec_last, g_last))
    return from_chunks(o), s_fin


def gla_recurrence(q, k, v, logf, s0):
    c = chunk_len(q.shape[2])
    q, k, v, logf = (to_chunks(t, c) for t in (q, k, v, logf))
    bcum = jnp.cumsum(logf, axis=-2)
    idx = jnp.arange(c)
    incl = (idx[:, None] >= idx[None, :])[:, :, None]

    def step(s, xs):
        qc, kc, vc, bc = xs
        dec = jnp.exp(jnp.where(incl, bc[:, :, :, None, :] - bc[:, :, None, :, :], -jnp.inf))
        attn = jnp.einsum('bhid,bhjd,bhijd->bhij', qc, kc, dec)
        o = jnp.einsum('bhid,bhdv->bhiv', qc * jnp.exp(bc), s) + jnp.einsum('bhij,bhjv->bhiv', attn, vc)
        s_new = jnp.exp(bc[:, :, -1, :])[..., None] * s + jnp.einsum('bhjd,bhjv->bhdv', kc * jnp.exp(bc[:, :, -1:, :] - bc), vc)
        return s_new, o

    s_fin, o = lax.scan(step, s0, (q, k, v, bcum))
    return from_chunks(o), s_fin


def complex_affine_combine(e1, e2):
    a1r, a1i, b1r, b1i = e1
    a2r, a2i, b2r, b2i = e2
    return (a2r * a1r - a2i * a1i, a2r * a1i + a2i * a1r,
            a2r * b1r - a2i * b1i + b2r, a2r * b1i + a2i * b1r + b2i)


def s5_branch(u, x0_re, x0_im, lam_re, lam_im, b_re, b_im, c_re, c_im, d, log_dt, glu_w):
    bsz, l = u.shape[:2]
    lr = jnp.minimum(lam_re.astype(F32), -1e-4)
    li = lam_im.astype(F32)
    dt = jnp.exp(log_dt.astype(F32))[:, None]
    mag = jnp.exp(lr * dt)
    ab_re, ab_im = mag * jnp.cos(li * dt), mag * jnp.sin(li * dt)
    den = lr * lr + li * li
    z_re = ((ab_re - 1.0) * lr + ab_im * li) / den
    z_im = (ab_im * lr - (ab_re - 1.0) * li) / den
    br_, bi_ = b_re.astype(F32), b_im.astype(F32)
    bb_re = z_re[..., None] * br_ - z_im[..., None] * bi_
    bb_im = z_re[..., None] * bi_ + z_im[..., None] * br_
    bu_re = jnp.einsum('blgh,gph->blgp', u, bb_re)
    bu_im = jnp.einsum('blgh,gph->blgp', u, bb_im)
    x0r, x0i = x0_re.astype(F32), x0_im.astype(F32)
    bu_re = bu_re.at[:, 0].add(ab_re * x0r - ab_im * x0i)
    bu_im = bu_im.at[:, 0].add(ab_re * x0i + ab_im * x0r)
    a_re = jnp.broadcast_to(ab_re, bu_re.shape)
    a_im = jnp.broadcast_to(ab_im, bu_im.shape)
    _, _, xr, xi = lax.associative_scan(complex_affine_combine, (a_re, a_im, bu_re, bu_im), axis=1)
    y = (jnp.einsum('blgp,ghp->blgh', xr, c_re.astype(F32))
         - jnp.einsum('blgp,ghp->blgh', xi, c_im.astype(F32))).reshape(bsz, l, BR_WIDTH)
    y = y + d.astype(F32) * u.reshape(bsz, l, BR_WIDTH)
    hg = jax.nn.gelu(y) @ glu_w.astype(F32)
    return hg[..., :BR_WIDTH] * jax.nn.sigmoid(hg[..., BR_WIDTH:]), xr[:, -1], xi[:, -1]


def hgrn_lower_bounds(logits):
    p = jax.nn.softmax(logits.astype(F32), axis=0)
    return jnp.cumsum(p, axis=0) - p[0]


def hgrn_log_forget(xf, lb):
    ls_pos = jax.nn.log_sigmoid(xf)
    pos = lb > 0
    lb_safe = jnp.where(pos, lb, 1.0)
    mixed = jnp.logaddexp(ls_pos, jnp.log(lb_safe) + jax.nn.log_sigmoid(-xf))
    return jnp.where(pos, mixed, ls_pos)


def trunk(x, states, prm):
    b, l, _ = x.shape
    conv_in, gdn_in, gla_in, s5r_in, s5i_in, hg_in = states
    lb_all = hgrn_lower_bounds(prm['hgrn_lb_logits'])
    new_states = ([], [], [], [], [], [])
    h = x
    for i in range(DEPTH):
        xn = rmsnorm(h, prm['norm1_w'][i])
        (a_qkv, a_alpha, a_beta, a_gate, b_q, b_k, b_v, b_gk, b_gate, c_u,
         d_q, d_f, d_i, d_gate, merge) = split_cols(xn @ prm['w_in'][i])

        qkv, conv_new = short_conv(a_qkv, conv_in[i], prm['gdn_conv_w'][i])
        qa, ka, va = jnp.split(qkv, 3, axis=-1)
        beta = jax.nn.sigmoid(a_beta.astype(F32)).transpose(0, 2, 1)
        g = (-jnp.exp(prm['gdn_a_log'][i].astype(F32))
             * jax.nn.softplus(a_alpha.astype(F32) + prm['gdn_dt_bias'][i].astype(F32))).transpose(0, 2, 1)
        o_a, s_a = gated_delta_rule(l2norm(to_heads(qa)) * HEAD_DIM ** -0.5, l2norm(to_heads(ka)),
                                    to_heads(va), g, beta, gdn_in[i].astype(F32))
        o_a = gated_head_norm(o_a, a_gate, prm['gdn_norm_w'][i])

        logf_b = jax.nn.log_sigmoid(b_gk.astype(F32) @ prm['gla_gk_w'][i].astype(F32)
                                    + prm['gla_gk_b'][i].astype(F32)) / GLA_GATE_NORM
        o_b, s_b = gla_recurrence(to_heads(b_q.astype(F32)) * HEAD_DIM ** -0.5, to_heads(b_k.astype(F32)),
                                  to_heads(b_v.astype(F32)), to_heads(logf_b), gla_in[i].astype(F32))
        o_b = gated_head_norm(o_b, b_gate, prm['gla_norm_w'][i])

        o_c, s5r, s5i = s5_branch(c_u.astype(F32).reshape(b, l, S5_GROUPS, S5_GROUP), s5r_in[i], s5i_in[i],
                                  prm['s5_lambda_re'][i], prm['s5_lambda_im'][i], prm['s5_b_re'][i],
                                  prm['s5_b_im'][i], prm['s5_c_re'][i], prm['s5_c_im'][i], prm['s5_d'][i],
                                  prm['s5_log_dt'][i], prm['s5_glu_w'][i])

        lb = lb_all[i]
        xf = d_f.astype(F32)
        logf_d = hgrn_log_forget(xf, lb)
        k_d = (1.0 - lb) * jax.nn.sigmoid(-xf)
        o_d, s_d = gla_recurrence(to_heads(jax.nn.silu(d_q.astype(F32))) * HEAD_DIM ** -0.5, to_heads(k_d),
                                  to_heads(d_i.astype(F32)), to_heads(logf_d), hg_in[i].astype(F32))
        o_d = gated_head_norm(o_d, d_gate, prm['hgrn_norm_w'][i])

        branches = jnp.stack([o_a, o_b, o_c, o_d], axis=2).astype(x.dtype)
        br = jnp.einsum('blkw,kwd->blkd', branches, prm['w_branch'][i])
        gates = jax.nn.sigmoid(merge.reshape(b, l, N_BRANCH, D_MODEL))
        h = h + jnp.sum(gates * br, axis=2) @ prm['w_out'][i]

        hn = rmsnorm(h, prm['norm2_w'][i])
        h = h + (jax.nn.silu(hn @ prm['ffn_w_gate'][i]) * (hn @ prm['ffn_w_up'][i])) @ prm['ffn_w_down'][i]

        for lst, s, ref in zip(new_states, (conv_new, s_a, s_b, s5r, s5i, s_d), states):
            lst.append(s.astype(ref.dtype))
    y = rmsnorm(h, prm['final_norm_w'])
    return y, [jnp.stack(lst) for lst in new_states]


def setup_inputs(seed: int = 0) -> dict:
    key = jax.random.key(seed)
    ks = iter(jax.random.split(key, 48))

    def nrm(shape, scale):
        return scale * jax.random.normal(next(ks), shape, F32)

    def log_uniform(shape, lo, hi):
        return jnp.exp(jax.random.uniform(next(ks), shape, F32, math.log(lo), math.log(hi)))

    W, H, G, P = BR_WIDTH, N_HEADS, S5_GROUPS, S5_P
    dt = log_uniform((DEPTH, H), 1e-3, 0.1)
    return {
        'x_prompt': nrm((BATCH, SEQ, D_MODEL), 1.0),
        'x_sample': nrm((DEC_BATCH, DEC_SEQ, D_MODEL), 1.0),
        'state_gdn_conv': nrm((DEPTH, DEC_BATCH, CONV_W - 1, 3 * W), 1.0),
        'state_gdn': nrm((DEPTH, DEC_BATCH, H, HEAD_DIM, HEAD_DIM), HEAD_DIM ** -0.5),
        'state_gla': nrm((DEPTH, DEC_BATCH, H, HEAD_DIM, HEAD_DIM), 1.0),
        'state_s5_re': nrm((DEPTH, DEC_BATCH, G, P), 0.5),
        'state_s5_im': nrm((DEPTH, DEC_BATCH, G, P), 0.5),
        'state_hgrn': nrm((DEPTH, DEC_BATCH, H, HEAD_DIM, HEAD_DIM), 1.0),
        'norm1_w': 1.0 + nrm((DEPTH, D_MODEL), 0.02),
        'w_in': nrm((DEPTH, D_MODEL, N_IN), D_MODEL ** -0.5),
        'gdn_conv_w': nrm((DEPTH, CONV_W, 3 * W), CONV_W ** -0.5),
        'gdn_a_log': jnp.log(jax.random.uniform(next(ks), (DEPTH, H), F32, 1.0, 16.0)),
        'gdn_dt_bias': dt + jnp.log(-jnp.expm1(-dt)),
        'gdn_norm_w': 1.0 + nrm((DEPTH, W), 0.02),
        'gla_gk_w': nrm((DEPTH, GLA_GATE_RANK, W), GLA_GATE_RANK ** -0.5),
        'gla_gk_b': nrm((DEPTH, W), 0.1),
        'gla_norm_w': 1.0 + nrm((DEPTH, W), 0.02),
        's5_lambda_re': -0.5 + nrm((DEPTH, G, P), 0.01),
        's5_lambda_im': jnp.pi * jnp.arange(P, dtype=F32) + nrm((DEPTH, G, P), 0.01),
        's5_b_re': nrm((DEPTH, G, P, S5_GROUP), (2 * S5_GROUP) ** -0.5),
        's5_b_im': nrm((DEPTH, G, P, S5_GROUP), (2 * S5_GROUP) ** -0.5),
        's5_c_re': nrm((DEPTH, G, S5_GROUP, P), P ** -0.5),
        's5_c_im': nrm((DEPTH, G, S5_GROUP, P), P ** -0.5),
        's5_d': nrm((DEPTH, W), 1.0),
        's5_log_dt': jnp.log(log_uniform((DEPTH, G), 1e-3, 0.1)),
        's5_glu_w': nrm((DEPTH, W, 2 * W), W ** -0.5),
        'hgrn_lb_logits': nrm((DEPTH, W), 0.5),
        'hgrn_norm_w': 1.0 + nrm((DEPTH, W), 0.02),
        'w_branch': nrm((DEPTH, N_BRANCH, W, D_MODEL), W ** -0.5),
        'w_out': nrm((DEPTH, D_MODEL, D_MODEL), D_MODEL ** -0.5),
        'norm2_w': 1.0 + nrm((DEPTH, D_MODEL), 0.02),
        'ffn_w_gate': nrm((DEPTH, D_MODEL, D_FF), D_MODEL ** -0.5),
        'ffn_w_up': nrm((DEPTH, D_MODEL, D_FF), D_MODEL ** -0.5),
        'ffn_w_down': nrm((DEPTH, D_FF, D_MODEL), D_FF ** -0.5),
        'final_norm_w': 1.0 + nrm((D_MODEL,), 0.02),
    }


def reference(x_prompt, x_sample, state_gdn_conv, state_gdn, state_gla, state_s5_re, state_s5_im, state_hgrn,
              norm1_w, w_in, gdn_conv_w, gdn_a_log, gdn_dt_bias, gdn_norm_w, gla_gk_w, gla_gk_b, gla_norm_w,
              s5_lambda_re, s5_lambda_im, s5_b_re, s5_b_im, s5_c_re, s5_c_im, s5_d, s5_log_dt, s5_glu_w,
              hgrn_lb_logits, hgrn_norm_w, w_branch, w_out, norm2_w, ffn_w_gate, ffn_w_up, ffn_w_down,
              final_norm_w):
    prm = dict(norm1_w=norm1_w, w_in=w_in, gdn_conv_w=gdn_conv_w, gdn_a_log=gdn_a_log, gdn_dt_bias=gdn_dt_bias,
               gdn_norm_w=gdn_norm_w, gla_gk_w=gla_gk_w, gla_gk_b=gla_gk_b, gla_norm_w=gla_norm_w,
               s5_lambda_re=s5_lambda_re, s5_lambda_im=s5_lambda_im, s5_b_re=s5_b_re, s5_b_im=s5_b_im,
               s5_c_re=s5_c_re, s5_c_im=s5_c_im, s5_d=s5_d, s5_log_dt=s5_log_dt, s5_glu_w=s5_glu_w,
               hgrn_lb_logits=hgrn_lb_logits, hgrn_norm_w=hgrn_norm_w, w_branch=w_branch, w_out=w_out,
               norm2_w=norm2_w, ffn_w_gate=ffn_w_gate, ffn_w_up=ffn_w_up, ffn_w_down=ffn_w_down,
               final_norm_w=final_norm_w)
    sample_states = (state_gdn_conv, state_gdn, state_gla, state_s5_re, state_s5_im, state_hgrn)
    prompt_states = tuple(jnp.zeros((DEPTH, x_prompt.shape[0]) + s.shape[2:], s.dtype) for s in sample_states)
    y_prompt, ps = trunk(x_prompt, prompt_states, prm)
    y_sample, ss = trunk(x_sample, sample_states, prm)
    return (y_prompt, y_sample, ps[0], ps[1], ps[2], ps[3], ps[4], ps[5],
            ss[0], ss[1], ss[2], ss[3], ss[4], ss[5])
```

```python
import functools
import math

import jax
import jax.numpy as jnp
from jax import lax
from jax.experimental import pallas as pl
from jax.experimental.pallas import tpu as pltpu

F32 = jnp.float32
BF16 = jnp.bfloat16
HIGHEST = lax.Precision.HIGHEST

D_MODEL = 1024
DEPTH = 4
N_BRANCH = 4
BR_WIDTH = D_MODEL // N_BRANCH
HEAD_DIM = 64
N_HEADS = BR_WIDTH // HEAD_DIM
CONV_W = 4
CHUNK = 64
SUB = 16
GLA_GATE_RANK = 16
GLA_GATE_NORM = 16.0
S5_GROUP = 16
S5_GROUPS = BR_WIDTH // S5_GROUP
S5_P = 64
S5_STATE = S5_GROUPS * S5_P
D_FF = -(-8 * D_MODEL // (3 * 256)) * 256
FF_TILE = 256
EPS = 1e-6
EXP_CAP = 60.0
LANE = 128
SEG_A = 3 * BR_WIDTH + BR_WIDTH + LANE
SEG_B = 4 * BR_WIDTH + LANE
SEG_C = BR_WIDTH
SEG_D = 4 * BR_WIDTH
SEG_OFF = (0, SEG_A, SEG_A + SEG_B, SEG_A + SEG_B + SEG_C, SEG_A + SEG_B + SEG_C + SEG_D)
N_MIX = SEG_OFF[-1]
VMEM_LIMIT = 56 * 1024 * 1024


def _cparams(*sem):
    return pltpu.CompilerParams(dimension_semantics=sem, vmem_limit_bytes=VMEM_LIMIT)


def _full(shape):
    n = len(shape)
    return pl.BlockSpec(shape, lambda *_: (0,) * n)


def _rms(x, w):
    return x * lax.rsqrt(jnp.mean(x * x, axis=-1, keepdims=True) + EPS) * w


def _sigmoid(x):
    return 1.0 / (1.0 + jnp.exp(-x))


def _silu(x):
    return x * _sigmoid(x)


def _softplus(x):
    return jnp.maximum(x, 0.0) + jnp.log1p(jnp.exp(-jnp.abs(x)))


def _log_sigmoid(x):
    return -_softplus(-x)


def _logaddexp(a, b):
    return jnp.maximum(a, b) + jnp.log1p(jnp.exp(-jnp.abs(a - b)))


def _gelu(x):
    return 0.5 * x * (1.0 + jnp.tanh(math.sqrt(2.0 / math.pi) * (x + 0.044715 * x * x * x)))


def _dot(a, b, precision=None):
    return jnp.dot(a, b, preferred_element_type=F32, precision=precision)


def _dot_nt(a, b, precision=None):
    return lax.dot_general(a, b, (((1,), (1,)), ((), ())), preferred_element_type=F32, precision=precision)


def _dot_tn(a, b):
    return lax.dot_general(a, b, (((0,), (0,)), ((), ())), preferred_element_type=F32)


def _iota2(shape, dim):
    return lax.broadcasted_iota(jnp.int32, shape, dim)


def _head_masks():
    r = _iota2((CHUNK, N_HEADS * CHUNK), 0)
    c = _iota2((CHUNK, N_HEADS * CHUNK), 1) & (CHUNK - 1)
    rr = _iota2((N_HEADS * CHUNK, N_HEADS * HEAD_DIM), 0) >> 6
    cc = _iota2((N_HEADS * CHUNK, N_HEADS * HEAD_DIM), 1) >> 6
    return dict(causal=r >= c, strict=r > c, eye=r == c, same_sub=(r >> 4) == (c >> 4), block=rr == cc)


def _tri():
    return (_iota2((CHUNK, CHUNK), 0) >= _iota2((CHUNK, CHUNK), 1)).astype(F32)


def _block_ones():
    r = _iota2((BR_WIDTH, BR_WIDTH), 0) >> 6
    c = _iota2((BR_WIDTH, BR_WIDTH), 1) >> 6
    return (r == c).astype(F32)


def _blockdiag(x, block):
    return jnp.where(block, jnp.concatenate([x] * N_HEADS, axis=0), 0.0)


def _head_norm_gate(o, gate, w, ones_blk):
    ms = _dot(o * o, ones_blk, HIGHEST) * (1.0 / HEAD_DIM)
    return o * lax.rsqrt(ms + EPS) * w * _silu(gate)


def _in_proj_kernel(x_ref, nw_ref, w_ref, pa_ref, pb_ref, pc_ref, pd_ref):
    xn = _rms(x_ref[...], nw_ref[...]).astype(BF16)
    for i, ref in enumerate((pa_ref, pb_ref, pc_ref, pd_ref)):
        ref[...] = _dot(xn, w_ref[:, SEG_OFF[i]:SEG_OFF[i + 1]])


def _in_proj(x, nw, w, tm):
    t = x.shape[0]
    widths = (SEG_A, SEG_B, SEG_C, SEG_D)
    return pl.pallas_call(
        _in_proj_kernel,
        grid=(t // tm,),
        in_specs=[pl.BlockSpec((tm, D_MODEL), lambda i: (i, 0)), _full((1, D_MODEL)), _full((D_MODEL, N_MIX))],
        out_specs=[pl.BlockSpec((tm, n), lambda i: (i, 0)) for n in widths],
        out_shape=[jax.ShapeDtypeStruct((t, n), F32) for n in widths],
        compiler_params=_cparams("parallel"),
        name="in_proj",
    )(x, nw, w)


def _merge_kernel(h_ref, oa_ref, ob_ref, oc_ref, od_ref, nw_ref, wm_ref, wb_ref, wo_ref, out_ref):
    h = h_ref[...]
    xn = _rms(h, nw_ref[...]).astype(BF16)
    acc = jnp.zeros(h.shape, F32)
    for k, o_ref in enumerate((oa_ref, ob_ref, oc_ref, od_ref)):
        gate = _sigmoid(_dot(xn, wm_ref[:, k * D_MODEL:(k + 1) * D_MODEL]))
        acc = acc + gate * _dot(o_ref[...].astype(BF16), wb_ref[k])
    out_ref[...] = h + _dot(acc.astype(BF16), wo_ref[...])


def _merge(h, oa, ob, oc, od, nw, wm, wb, wo, tm):
    t = h.shape[0]
    row = lambda n: pl.BlockSpec((tm, n), lambda i: (i, 0))
    return pl.pallas_call(
        _merge_kernel,
        grid=(t // tm,),
        in_specs=[row(D_MODEL)] + [row(BR_WIDTH)] * 4 + [
            _full((1, D_MODEL)), _full((D_MODEL, N_BRANCH * D_MODEL)),
            _full((N_BRANCH, BR_WIDTH, D_MODEL)), _full((D_MODEL, D_MODEL))],
        out_specs=row(D_MODEL),
        out_shape=jax.ShapeDtypeStruct((t, D_MODEL), F32),
        compiler_params=_cparams("parallel"),
        name="merge",
    )(h, oa, ob, oc, od, nw, wm, wb, wo)


def _ffn_kernel(h_ref, nw_ref, wg_ref, wu_ref, wd_ref, fw_ref, out_ref, y_ref):
    h = h_ref[...]
    hn = _rms(h, nw_ref[...]).astype(BF16)
    acc = jnp.zeros(h.shape, F32)
    for c in range(D_FF // FF_TILE):
        sl = slice(c * FF_TILE, (c + 1) * FF_TILE)
        a = _silu(_dot(hn, wg_ref[:, sl])) * _dot(hn, wu_ref[:, sl])
        acc = acc + _dot(a.astype(BF16), wd_ref[sl, :])
    hnew = h + acc
    out_ref[...] = hnew
    y_ref[...] = _rms(hnew, fw_ref[...])


def _ffn(h, nw, wg, wu, wd, fw, tm):
    t = h.shape[0]
    row = pl.BlockSpec((tm, D_MODEL), lambda i: (i, 0))
    return pl.pallas_call(
        _ffn_kernel,
        grid=(t // tm,),
        in_specs=[row, _full((1, D_MODEL)), _full((D_MODEL, D_FF)), _full((D_MODEL, D_FF)),
                  _full((D_FF, D_MODEL)), _full((1, D_MODEL))],
        out_specs=[row, row],
        out_shape=[jax.ShapeDtypeStruct((t, D_MODEL), F32)] * 2,
        compiler_params=_cparams("parallel"),
        name="ffn",
    )(h, nw, wg, wu, wd, fw)


def _gla_chunk(q, k, v, lf, st, tri, m):
    b = _dot(tri, lf, HIGHEST)
    b_last = b[CHUNK - 1:CHUNK, :]
    qe = q * jnp.exp(b)
    kd = k * jnp.exp(b_last - b)
    rows = []
    for i0 in range(0, CHUNK, SUB):
        anchor = b[i0:i0 + 1, :]
        q_i = q[i0:i0 + SUB, :] * jnp.exp(b[i0:i0 + SUB, :] - anchor)
        k_i = k * jnp.exp(jnp.minimum(anchor - b, EXP_CAP))
        rows.append(_dot_nt(q_i, _blockdiag(k_i, m["block"])))
    attn = jnp.where(m["causal"], jnp.concatenate(rows, axis=0), 0.0)
    o = _dot_nt(qe, st) + _dot(attn, _blockdiag(v, m["block"]))
    st_new = st * jnp.exp(b_last) + jnp.where(m["block"], _dot_tn(v, kd), 0.0)
    return o, st_new


def _gla_scan(q_s, k_s, v_s, lf_s, o_s, st_ref, n_chunks):
    m = _head_masks()
    tri = _tri()

    def body(c, carry):
        r = pl.ds(pl.multiple_of(c * CHUNK, CHUNK), CHUNK)
        o, st_new = _gla_chunk(q_s[r, :], k_s[r, :], v_s[r, :], lf_s[r, :], st_ref[0], tri, m)
        o_s[r, :] = o
        st_ref[0] = st_new
        return carry

    lax.fori_loop(0, n_chunks, body, 0)


def _gla_prompt_kernel(p_ref, gkw_ref, gkb_ref, nw_ref, o_ref, st_ref, q_s, k_s, v_s, lf_s, o_s, *, tt):
    @pl.when(pl.program_id(1) == 0)
    def _():
        st_ref[...] = jnp.zeros(st_ref.shape, F32)

    w = BR_WIDTH
    q_s[...] = p_ref[:, 0:w] * HEAD_DIM ** -0.5
    k_s[...] = p_ref[:, w:2 * w]
    v_s[...] = p_ref[:, 2 * w:3 * w]
    lf_s[...] = _log_sigmoid(_dot(p_ref[:, 4 * w:4 * w + LANE], gkw_ref[...]) + gkb_ref[...]) * (1.0 / GLA_GATE_NORM)
    _gla_scan(q_s, k_s, v_s, lf_s, o_s, st_ref, tt // CHUNK)
    o_ref[...] = _head_norm_gate(o_s[...], p_ref[:, 3 * w:4 * w], nw_ref[...], _block_ones())


def _hgrn_prompt_kernel(p_ref, lb_ref, nw_ref, o_ref, st_ref, q_s, k_s, v_s, lf_s, o_s, *, tt):
    @pl.when(pl.program_id(1) == 0)
    def _():
        st_ref[...] = jnp.zeros(st_ref.shape, F32)

    w = BR_WIDTH
    lb = lb_ref[...]
    xf = p_ref[:, w:2 * w]
    q_s[...] = _silu(p_ref[:, 0:w]) * HEAD_DIM ** -0.5
    k_s[...] = (1.0 - lb) * _sigmoid(-xf)
    v_s[...] = p_ref[:, 2 * w:3 * w]
    lf_s[...] = _hgrn_log_forget(xf, lb)
    _gla_scan(q_s, k_s, v_s, lf_s, o_s, st_ref, tt // CHUNK)
    o_ref[...] = _head_norm_gate(o_s[...], p_ref[:, 3 * w:4 * w], nw_ref[...], _block_ones())


def _hgrn_log_forget(xf, lb):
    ls_pos = _log_sigmoid(xf)
    pos = lb > 0
    lb_safe = jnp.where(pos, lb, 1.0)
    mixed = _logaddexp(ls_pos, jnp.log(lb_safe) + _log_sigmoid(-xf))
    return jnp.where(pos, mixed, ls_pos)


def _linear_prompt(kernel, p, params, bsz, seq, tt):
    nt = seq // tt
    width = p.shape[1]
    return pl.pallas_call(
        functools.partial(kernel, tt=tt),
        grid=(bsz, nt),
        in_specs=[pl.BlockSpec((tt, width), lambda b, t: (b * nt + t, 0))] + [_full(a.shape) for a in params],
        out_specs=[pl.BlockSpec((tt, BR_WIDTH), lambda b, t: (b * nt + t, 0)),
                   pl.BlockSpec((1, BR_WIDTH, BR_WIDTH), lambda b, t: (b, 0, 0))],
        out_shape=[jax.ShapeDtypeStruct((bsz * seq, BR_WIDTH), F32),
                   jax.ShapeDtypeStruct((bsz, BR_WIDTH, BR_WIDTH), F32)],
        scratch_shapes=[pltpu.VMEM((tt, BR_WIDTH), F32)] * 5,
        compiler_params=_cparams("parallel", "arbitrary"),
        name=kernel.__name__.strip("_"),
    )(p, *params)


def _state_from_blockdiag(st):
    b = st.shape[0]
    s = st.reshape(b, N_HEADS, HEAD_DIM, N_HEADS, HEAD_DIM)
    s = jnp.stack([s[:, h, :, h, :] for h in range(N_HEADS)], axis=1)
    return jnp.swapaxes(s, -1, -2)


def _split_bf16(x):
    hi = x.astype(BF16)
    lo = (x - hi.astype(F32)).astype(BF16)
    return hi, lo


def _mm3(a, b_blk):
    a_hi, a_lo = _split_bf16(a)
    b_hi, b_lo = _split_bf16(b_blk)
    return _dot(a_hi, b_hi) + (_dot(a_lo, b_hi) + _dot(a_hi, b_lo))


def _unit_lower_inverse(n, m):
    blk = m["block"]
    eye = m["eye"].astype(F32)
    nd = jnp.where(m["same_sub"], n, 0.0)
    low = n - nd
    p1 = _mm3(nd, _blockdiag(nd, blk))
    p2 = _mm3(p1, _blockdiag(p1, blk))
    p3 = _mm3(p2, _blockdiag(p2, blk))
    dinv = _mm3(eye - nd, _blockdiag(eye + p1, blk))
    dinv = _mm3(dinv, _blockdiag(eye + p2, blk))
    dinv = _mm3(dinv, _blockdiag(eye + p3, blk))
    mm = _mm3(dinv, _blockdiag(low, blk))
    mm_blk = _blockdiag(mm, blk)
    m2 = _mm3(mm, mm_blk)
    m3 = _mm3(m2, mm_blk)
    return _mm3(eye - mm + m2 - m3, _blockdiag(dinv, blk))


def _gdn_chunk(q, k, v, gexp, bexp, st, tri, ones_c, m):
    blk = m["block"]
    gam_i = _dot(tri, gexp, HIGHEST)
    gam_j = _dot(ones_c, jnp.where(m["eye"], gam_i, 0.0), HIGHEST)
    gam_last = gam_i[CHUNK - 1:CHUNK, :]
    decay = jnp.where(m["causal"], jnp.exp(jnp.minimum(gam_i - gam_j, 0.0)), 0.0)
    k_blk = _blockdiag(k, blk)
    kk = _dot_nt(k, k_blk)
    attn = _dot_nt(q, k_blk) * decay
    n = jnp.where(m["strict"], bexp * kk * decay, 0.0)
    tinv = _unit_lower_inverse(n, m)
    u_v = _dot(tinv, _blockdiag(bexp * v, blk))
    w_k = _dot(tinv, _blockdiag(bexp * jnp.exp(gam_i) * k, blk))
    u = u_v - _dot_nt(w_k, st)
    o = jnp.exp(gam_i) * _dot_nt(q, st) + _dot(attn, _blockdiag(u, blk))
    kd = k * jnp.exp(gam_last - gam_i)
    st_new = st * jnp.exp(gam_last) + jnp.where(blk, _dot_tn(u, kd), 0.0)
    return o, st_new


def _l2norm_heads(x, ones_blk):
    return x * lax.rsqrt(_dot(x * x, ones_blk, HIGHEST) + EPS)


def _gdn_prep(a_alpha_beta, alog_ref, dtb_ref):
    r = _iota2((LANE, BR_WIDTH), 0)
    c = _iota2((LANE, BR_WIDTH), 1) >> 6
    sel_a = (r == c).astype(F32)
    sel_b = (r == c + N_HEADS).astype(F32)
    alpha = _dot(a_alpha_beta, sel_a, HIGHEST)
    beta = _sigmoid(_dot(a_alpha_beta, sel_b, HIGHEST))
    g = -jnp.exp(alog_ref[...]) * _softplus(alpha + dtb_ref[...])
    return g, beta


def _gdn_prompt_kernel(p_ref, cw_ref, alog_ref, dtb_ref, nw_ref, o_ref, st_ref, conv_ref,
                       xp_s, q_s, k_s, v_s, g_s, b_s, o_s, *, tt):
    w = BR_WIDTH
    t = pl.program_id(1)

    @pl.when(t == 0)
    def _():
        st_ref[...] = jnp.zeros(st_ref.shape, F32)
        xp_s[0:8, :] = jnp.zeros((8, 3 * w), F32)

    @pl.when(t > 0)
    def _():
        xp_s[0:8, :] = xp_s[tt:tt + 8, :]

    xp_s[8:8 + tt, :] = p_ref[:, 0:3 * w]
    conv_ref[0] = xp_s[tt:tt + 8, :]
    y = xp_s[8:8 + tt, :] * cw_ref[CONV_W - 1:CONV_W, :]
    for j in range(CONV_W - 1):
        y = y + xp_s[5 + j:5 + j + tt, :] * cw_ref[j:j + 1, :]
    qkv = _silu(y)
    ones_blk = _block_ones()
    q_s[...] = _l2norm_heads(qkv[:, 0:w], ones_blk) * HEAD_DIM ** -0.5
    k_s[...] = _l2norm_heads(qkv[:, w:2 * w], ones_blk)
    v_s[...] = qkv[:, 2 * w:3 * w]
    g, beta = _gdn_prep(p_ref[:, 4 * w:4 * w + LANE], alog_ref, dtb_ref)
    g_s[...] = g
    b_s[...] = beta

    m = _head_masks()
    tri = _tri()
    ones_c = jnp.ones((CHUNK, CHUNK), F32)

    def body(c, carry):
        r = pl.ds(pl.multiple_of(c * CHUNK, CHUNK), CHUNK)
        o, st_new = _gdn_chunk(q_s[r, :], k_s[r, :], v_s[r, :], g_s[r, :], b_s[r, :], st_ref[0], tri, ones_c, m)
        o_s[r, :] = o
        st_ref[0] = st_new
        return carry

    lax.fori_loop(0, tt // CHUNK, body, 0)
    o_ref[...] = _head_norm_gate(o_s[...], p_ref[:, 3 * w:4 * w], nw_ref[...], ones_blk)


def _gdn_prompt(p, params, bsz, seq, tt):
    nt = seq // tt
    w = BR_WIDTH
    return pl.pallas_call(
        functools.partial(_gdn_prompt_kernel, tt=tt),
        grid=(bsz, nt),
        in_specs=[pl.BlockSpec((tt, SEG_A), lambda b, t: (b * nt + t, 0))] + [_full(a.shape) for a in params],
        out_specs=[pl.BlockSpec((tt, w), lambda b, t: (b * nt + t, 0)),
                   pl.BlockSpec((1, w, w), lambda b, t: (b, 0, 0)),
                   pl.BlockSpec((1, 8, 3 * w), lambda b, t: (b, 0, 0))],
        out_shape=[jax.ShapeDtypeStruct((bsz * seq, w), F32),
                   jax.ShapeDtypeStruct((bsz, w, w), F32),
                   jax.ShapeDtypeStruct((bsz, 8, 3 * w), F32)],
        scratch_shapes=[pltpu.VMEM((tt + 8, 3 * w), F32)] + [pltpu.VMEM((tt, w), F32)] * 6,
        compiler_params=_cparams("parallel", "arbitrary"),
        name="gdn_prompt",
    )(p, *params)


def _s5_kernel(u_ref, x0r_ref, x0i_ref, ar_ref, ai_ref, br_ref, bi_ref, cr_ref, ci_ref, d_ref, glu_ref,
               o_ref, xr_ref, xi_ref, sr_s, si_s, *, steps, rows):
    @pl.when(pl.program_id(0) == 0)
    def _():
        xr_ref[...] = x0r_ref[...]
        xi_ref[...] = x0i_ref[...]

    u = u_ref[...]
    ub = u.astype(BF16)
    sr_s[...] = _dot(ub, br_ref[...])
    si_s[...] = _dot(ub, bi_ref[...])
    a_re = ar_ref[...]
    a_im = ai_ref[...]

    def body(t, carry):
        xr, xi = carry
        r = pl.ds(pl.multiple_of(t * rows, rows), rows)
        nr = a_re * xr - a_im * xi + sr_s[r, :]
        ni = a_re * xi + a_im * xr + si_s[r, :]
        sr_s[r, :] = nr
        si_s[r, :] = ni
        return nr, ni

    xr, xi = lax.fori_loop(0, steps, body, (xr_ref[...], xi_ref[...]))
    xr_ref[...] = xr
    xi_ref[...] = xi
    y = _dot(sr_s[...].astype(BF16), cr_ref[...]) - _dot(si_s[...].astype(BF16), ci_ref[...]) + d_ref[...] * u
    hg = _dot(_gelu(y).astype(BF16), glu_ref[...])
    o_ref[...] = hg[:, 0:BR_WIDTH] * _sigmoid(hg[:, BR_WIDTH:2 * BR_WIDTH])


def _s5(u, x0r, x0i, params, steps_total, rows, steps):
    nt = steps_total // steps
    tile = steps * rows
    return pl.pallas_call(
        functools.partial(_s5_kernel, steps=steps, rows=rows),
        grid=(nt,),
        in_specs=[pl.BlockSpec((tile, BR_WIDTH), lambda t: (t, 0)), _full(x0r.shape), _full(x0i.shape)]
                 + [_full(a.shape) for a in params],
        out_specs=[pl.BlockSpec((tile, BR_WIDTH), lambda t: (t, 0)), _full(x0r.shape), _full(x0i.shape)],
        out_shape=[jax.ShapeDtypeStruct((steps_total * rows, BR_WIDTH), F32),
                   jax.ShapeDtypeStruct(x0r.shape, F32), jax.ShapeDtypeStruct(x0i.shape, F32)],
        scratch_shapes=[pltpu.VMEM((tile, S5_STATE), F32)] * 2,
        compiler_params=_cparams("arbitrary"),
        name="s5",
    )(u, x0r, x0i, *params)


def _s5_params(lam_re, lam_im, b_re, b_im, c_re, c_im, d, log_dt, glu_w):
    lr = jnp.minimum(lam_re, -1e-4)
    li = lam_im
    dt = jnp.exp(log_dt)[:, None]
    mag = jnp.exp(lr * dt)
    ab_re, ab_im = mag * jnp.cos(li * dt), mag * jnp.sin(li * dt)
    den = lr * lr + li * li
    z_re = ((ab_re - 1.0) * lr + ab_im * li) / den
    z_im = (ab_im * lr - (ab_re - 1.0) * li) / den
    bb_re = z_re[..., None] * b_re - z_im[..., None] * b_im
    bb_im = z_re[..., None] * b_im + z_im[..., None] * b_re
    eye = jnp.eye(S5_GROUPS, dtype=F32)

    def pack_in(bb):
        return jnp.einsum('gph,gk->ghkp', bb, eye).reshape(BR_WIDTH, S5_STATE).astype(BF16)

    def pack_out(c):
        return jnp.einsum('ghp,gk->gpkh', c, eye).reshape(S5_STATE, BR_WIDTH).astype(BF16)

    return (ab_re.reshape(1, S5_STATE), ab_im.reshape(1, S5_STATE), pack_in(bb_re), pack_in(bb_im),
            pack_out(c_re), pack_out(c_im), d.reshape(1, BR_WIDTH), glu_w.astype(BF16))


def _expand_mats():
    r = _iota2((HEAD_DIM, HEAD_DIM * HEAD_DIM), 0)
    c = _iota2((HEAD_DIM, HEAD_DIM * HEAD_DIM), 1)
    rep_k = (r == (c >> 6)).astype(F32)
    rep_v = (r == (c & (HEAD_DIM - 1))).astype(F32)
    return rep_k, rep_v


def _gla_decode_kernel(q_ref, k_ref, v_ref, lf_ref, s_ref, o_ref, sn_ref):
    rep_k, rep_v = _expand_mats()
    q, k, v, lf = q_ref[0], k_ref[0], v_ref[0], lf_ref[0]
    s = s_ref[...]
    dec = jnp.exp(lf)
    qe_x = _dot(q * dec, rep_k, HIGHEST)
    o = _dot_nt(qe_x * s, rep_v, HIGHEST) + jnp.sum(q * k, axis=-1, keepdims=True) * v
    sn_ref[...] = _dot(dec, rep_k, HIGHEST) * s + _dot(k, rep_k, HIGHEST) * _dot(v, rep_v, HIGHEST)
    o_ref[0] = o


def _gdn_decode_kernel(q_ref, k_ref, v_ref, g_ref, b_ref, s_ref, o_ref, sn_ref):
    rep_k, rep_v = _expand_mats()
    q, k, v = q_ref[0], k_ref[0], v_ref[0]
    eg = jnp.exp(g_ref[0])
    beta = b_ref[0]
    s = s_ref[...]
    k_x = _dot(k, rep_k, HIGHEST)
    ks = _dot_nt(k_x * s, rep_v, HIGHEST)
    qs = _dot_nt(_dot(q, rep_k, HIGHEST) * s, rep_v, HIGHEST)
    u = beta * v - (beta * eg) * ks
    o_ref[0] = eg * qs + jnp.sum(q * k, axis=-1, keepdims=True) * u
    sn_ref[...] = eg * s + k_x * _dot(u, rep_v, HIGHEST)


def _decode_call(kernel, vecs, state, name):
    rows = state.shape[0]
    hw = HEAD_DIM * HEAD_DIM
    return pl.pallas_call(
        kernel,
        grid=(N_HEADS,),
        in_specs=[pl.BlockSpec((1, rows, a.shape[2]), lambda h: (h, 0, 0)) for a in vecs]
                 + [pl.BlockSpec((rows, hw), lambda h: (0, h))],
        out_specs=[pl.BlockSpec((1, rows, HEAD_DIM), lambda h: (h, 0, 0)), pl.BlockSpec((rows, hw), lambda h: (0, h))],
        out_shape=[jax.ShapeDtypeStruct((N_HEADS, rows, HEAD_DIM), F32), jax.ShapeDtypeStruct(state.shape, F32)],
        compiler_params=_cparams("parallel"),
        name=name,
    )(*vecs, state)


def _decode_prep_kernel(pa_ref, pb_ref, pd_ref, conv_ref, cw_ref, alog_ref, dtb_ref, gkw_ref, gkb_ref, lb_ref,
                        a_ref, ag_ref, ab_ref, b_ref, d_ref, convn_ref):
    w = BR_WIDTH
    ones_blk = _block_ones()
    raw = pa_ref[:, 0:3 * w]
    y = raw * cw_ref[CONV_W - 1:CONV_W, :]
    for j in range(CONV_W - 1):
        y = y + conv_ref[j] * cw_ref[j:j + 1, :]
    convn_ref[0] = conv_ref[1]
    convn_ref[1] = conv_ref[2]
    convn_ref[2] = raw
    qkv = _silu(y)
    a_ref[0] = _l2norm_heads(qkv[:, 0:w], ones_blk) * HEAD_DIM ** -0.5
    a_ref[1] = _l2norm_heads(qkv[:, w:2 * w], ones_blk)
    a_ref[2] = qkv[:, 2 * w:3 * w]
    g, beta = _gdn_prep(pa_ref[:, 4 * w:4 * w + LANE], alog_ref, dtb_ref)
    ag_ref[...] = g
    ab_ref[...] = beta
    b_ref[0] = pb_ref[:, 0:w] * HEAD_DIM ** -0.5
    b_ref[1] = pb_ref[:, w:2 * w]
    b_ref[2] = pb_ref[:, 2 * w:3 * w]
    b_ref[3] = _log_sigmoid(_dot(pb_ref[:, 4 * w:4 * w + LANE], gkw_ref[...]) + gkb_ref[...]) * (1.0 / GLA_GATE_NORM)
    lb = lb_ref[...]
    xf = pd_ref[:, w:2 * w]
    d_ref[0] = _silu(pd_ref[:, 0:w]) * HEAD_DIM ** -0.5
    d_ref[1] = (1.0 - lb) * _sigmoid(-xf)
    d_ref[2] = pd_ref[:, 2 * w:3 * w]
    d_ref[3] = _hgrn_log_forget(xf, lb)


def _decode_prep(pa, pb, pd, conv, params):
    rows = pa.shape[0]
    w = BR_WIDTH
    ins = (pa, pb, pd, conv) + tuple(params)
    shapes = [(3, rows, w), (rows, w), (rows, w), (4, rows, w), (4, rows, w), (CONV_W - 1, rows, 3 * w)]
    return pl.pallas_call(
        _decode_prep_kernel,
        grid=(1,),
        in_specs=[_full(a.shape) for a in ins],
        out_specs=[_full(s) for s in shapes],
        out_shape=[jax.ShapeDtypeStruct(s, F32) for s in shapes],
        compiler_params=_cparams("arbitrary"),
        name="decode_prep",
    )(*ins)


def _decode_post_kernel(oa_ref, ob_ref, od_ref, pa_ref, pb_ref, pd_ref, nwa_ref, nwb_ref, nwd_ref,
                        a_ref, b_ref, d_ref):
    w = BR_WIDTH
    ones_blk = _block_ones()
    a_ref[...] = _head_norm_gate(oa_ref[...], pa_ref[:, 3 * w:4 * w], nwa_ref[...], ones_blk)
    b_ref[...] = _head_norm_gate(ob_ref[...], pb_ref[:, 3 * w:4 * w], nwb_ref[...], ones_blk)
    d_ref[...] = _head_norm_gate(od_ref[...], pd_ref[:, 3 * w:4 * w], nwd_ref[...], ones_blk)


def _decode_post(oa, ob, od, pa, pb, pd, nwa, nwb, nwd):
    ins = (oa, ob, od, pa, pb, pd, nwa, nwb, nwd)
    shp = oa.shape
    return pl.pallas_call(
        _decode_post_kernel,
        grid=(1,),
        in_specs=[_full(a.shape) for a in ins],
        out_specs=[_full(shp)] * 3,
        out_shape=[jax.ShapeDtypeStruct(shp, F32)] * 3,
        compiler_params=_cparams("arbitrary"),
        name="decode_post",
    )(*ins)


def _to_heads(x):
    return jnp.transpose(x.reshape(x.shape[0], N_HEADS, HEAD_DIM), (1, 0, 2))


def _from_heads(x):
    return jnp.transpose(x, (1, 0, 2)).reshape(x.shape[1], BR_WIDTH)


def _head_scalar(x):
    return jnp.transpose(x[:, ::HEAD_DIM], (1, 0))[:, :, None]


def _pack_w_in(w_in):
    sizes = (3 * BR_WIDTH, N_HEADS, N_HEADS, BR_WIDTH, BR_WIDTH, BR_WIDTH, BR_WIDTH, GLA_GATE_RANK, BR_WIDTH,
             BR_WIDTH, BR_WIDTH, BR_WIDTH, BR_WIDTH, BR_WIDTH, N_BRANCH * D_MODEL)
    parts, off = [], 0
    for n in sizes:
        parts.append(w_in[:, off:off + n])
        off += n
    (a_qkv, a_alpha, a_beta, a_gate, b_q, b_k, b_v, b_gk, b_gate, c_u, d_q, d_f, d_i, d_gate, merge) = parts
    zpad = lambda n: jnp.zeros((D_MODEL, n), w_in.dtype)
    mix = jnp.concatenate([a_qkv, a_gate, a_alpha, a_beta, zpad(LANE - 2 * N_HEADS),
                           b_q, b_k, b_v, b_gate, b_gk, zpad(LANE - GLA_GATE_RANK),
                           c_u, d_q, d_f, d_i, d_gate], axis=1)
    return mix.astype(BF16), merge.astype(BF16)


def _hgrn_lower_bounds(logits):
    p = jax.nn.softmax(logits, axis=0)
    return jnp.cumsum(p, axis=0) - p[0]


def _row(x):
    return x.reshape(1, -1)


def _rep_heads(x):
    return jnp.repeat(x, HEAD_DIM).reshape(1, BR_WIDTH)


def kernel(x_prompt, x_sample, state_gdn_conv, state_gdn, state_gla, state_s5_re, state_s5_im, state_hgrn, norm1_w, w_in, gdn_conv_w, gdn_a_log, gdn_dt_bias, gdn_norm_w, gla_gk_w, gla_gk_b, gla_norm_w, s5_lambda_re, s5_lambda_im, s5_b_re, s5_b_im, s5_c_re, s5_c_im, s5_d, s5_log_dt, s5_glu_w, hgrn_lb_logits, hgrn_norm_w, w_branch, w_out, norm2_w, ffn_w_gate, ffn_w_up, ffn_w_down, final_norm_w):
    bsz, seq, _ = x_prompt.shape
    dbs = x_sample.shape[0]
    tm_p = 512 if (bsz * seq) % 512 == 0 else bsz * seq
    tt = 256 if seq % 256 == 0 else seq
    s5_steps = 128 if seq % 128 == 0 else seq
    lb_all = _hgrn_lower_bounds(hgrn_lb_logits)
    hp = x_prompt.reshape(bsz * seq, D_MODEL)
    hs = x_sample.reshape(dbs, D_MODEL)
    zeros_p = jnp.zeros((bsz, S5_STATE), F32)
    fw = _row(final_norm_w)
    outs_p = [[] for _ in range(6)]
    outs_s = [[] for _ in range(6)]
    yp = ys = None
    for i in range(DEPTH):
        w_mix, w_merge = _pack_w_in(w_in[i])
        nw1, nw2 = _row(norm1_w[i]), _row(norm2_w[i])
        gkw = jnp.concatenate([gla_gk_w[i], jnp.zeros((LANE - GLA_GATE_RANK, BR_WIDTH), F32)], axis=0)
        gdn_params = (gdn_conv_w[i], _rep_heads(gdn_a_log[i]), _rep_heads(gdn_dt_bias[i]), _row(gdn_norm_w[i]))
        gla_params = (gkw, _row(gla_gk_b[i]), _row(gla_norm_w[i]))
        hgrn_params = (_row(lb_all[i]), _row(hgrn_norm_w[i]))
        s5_params = _s5_params(s5_lambda_re[i], s5_lambda_im[i], s5_b_re[i], s5_b_im[i], s5_c_re[i], s5_c_im[i],
                               s5_d[i], s5_log_dt[i], s5_glu_w[i])
        wb, wo = w_branch[i].astype(BF16), w_out[i].astype(BF16)
        wg, wu, wd = ffn_w_gate[i].astype(BF16), ffn_w_up[i].astype(BF16), ffn_w_down[i].astype(BF16)

        pa, pb, pc, pd = _in_proj(hp, nw1, w_mix, tm_p)
        o_a, st_a, conv_p = _gdn_prompt(pa, gdn_params, bsz, seq, tt)
        o_b, st_b = _linear_prompt(_gla_prompt_kernel, pb, gla_params, bsz, seq, tt)
        o_d, st_d = _linear_prompt(_hgrn_prompt_kernel, pd, hgrn_params, bsz, seq, tt)
        u_tm = jnp.transpose(pc.reshape(bsz, seq, BR_WIDTH), (1, 0, 2)).reshape(seq * bsz, BR_WIDTH)
        o_c_tm, xr_p, xi_p = _s5(u_tm, zeros_p, zeros_p, s5_params, seq, bsz, s5_steps)
        o_c = jnp.transpose(o_c_tm.reshape(seq, bsz, BR_WIDTH), (1, 0, 2)).reshape(bsz * seq, BR_WIDTH)
        hp = _merge(hp, o_a, o_b, o_c, o_d, nw1, w_merge, wb, wo, tm_p)
        hp, yp = _ffn(hp, nw2, wg, wu, wd, fw, tm_p)
        for lst, s in zip(outs_p, (conv_p[:, 5:8, :], _state_from_blockdiag(st_a), _state_from_blockdiag(st_b),
                                   xr_p.reshape(bsz, S5_GROUPS, S5_P), xi_p.reshape(bsz, S5_GROUPS, S5_P),
                                   _state_from_blockdiag(st_d))):
            lst.append(s)

        pa, pb, pc, pd = _in_proj(hs, nw1, w_mix, dbs)
        conv_in = jnp.transpose(state_gdn_conv[i], (1, 0, 2))
        va, g_a, beta_a, vb, vd, conv_s = _decode_prep(
            pa, pb, pd, conv_in,
            (gdn_conv_w[i], gdn_params[1], gdn_params[2], gkw, gla_params[1], hgrn_params[0]))
        flat = lambda s: s.reshape(dbs, N_HEADS * HEAD_DIM * HEAD_DIM)
        oa_h, sa = _decode_call(_gdn_decode_kernel,
                                [_to_heads(va[0]), _to_heads(va[1]), _to_heads(va[2]), _head_scalar(g_a), _head_scalar(beta_a)],
                                flat(state_gdn[i]), "gdn_decode")
        ob_h, sb = _decode_call(_gla_decode_kernel, [_to_heads(vb[j]) for j in range(4)], flat(state_gla[i]), "gla_decode")
        od_h, sd = _decode_call(_gla_decode_kernel, [_to_heads(vd[j]) for j in range(4)], flat(state_hgrn[i]), "hgrn_decode")
        o_a, o_b, o_d = _decode_post(_from_heads(oa_h), _from_heads(ob_h), _from_heads(od_h), pa, pb, pd,
                                     gdn_params[3], gla_params[2], hgrn_params[1])
        o_c, xr_s, xi_s = _s5(pc, state_s5_re[i].reshape(dbs, S5_STATE), state_s5_im[i].reshape(dbs, S5_STATE),
                              s5_params, 1, dbs, 1)
        hs = _merge(hs, o_a, o_b, o_c, o_d, nw1, w_merge, wb, wo, dbs)
        hs, ys = _ffn(hs, nw2, wg, wu, wd, fw, dbs)
        st5 = lambda s: s.reshape(dbs, N_HEADS, HEAD_DIM, HEAD_DIM)
        for lst, s in zip(outs_s, (jnp.transpose(conv_s, (1, 0, 2)), st5(sa), st5(sb),
                                   xr_s.reshape(dbs, S5_GROUPS, S5_P), xi_s.reshape(dbs, S5_GROUPS, S5_P), st5(sd))):
            lst.append(s)

    return (yp.reshape(bsz, seq, D_MODEL), ys.reshape(dbs, 1, D_MODEL),
            *[jnp.stack(l) for l in outs_p], *[jnp.stack(l) for l in outs_s])
```

```python
import functools
import math

import jax
import jax.numpy as jnp
from jax import lax
from jax.experimental import pallas as pl
from jax.experimental.pallas import tpu as pltpu

F32 = jnp.float32
BF16 = jnp.bfloat16

D_MODEL = 1024
DEPTH = 4
N_BRANCH = 4
BR_WIDTH = D_MODEL // N_BRANCH
HEAD_DIM = 64
N_HEADS = BR_WIDTH // HEAD_DIM
CONV_W = 4
CHUNK = 64
SUB = 16
GLA_GATE_RANK = 16
GLA_GATE_NORM = 16.0
S5_GROUP = 16
S5_GROUPS = BR_WIDTH // S5_GROUP
S5_P = 64
S5_STATE = S5_GROUPS * S5_P
D_FF = -(-8 * D_MODEL // (3 * 256)) * 256
FF_TILE = 256
EPS = 1e-6
EXP_CAP = 60.0
LANE = 128
SEG_A = 3 * BR_WIDTH + BR_WIDTH + LANE
SEG_B = 4 * BR_WIDTH + LANE
SEG_C = BR_WIDTH
SEG_D = 4 * BR_WIDTH
SEG_OFF = (0, SEG_A, SEG_A + SEG_B, SEG_A + SEG_B + SEG_C, SEG_A + SEG_B + SEG_C + SEG_D)
N_MIX = SEG_OFF[-1]
VMEM_LIMIT = 56 * 1024 * 1024


def _cparams(*sem):
    return pltpu.CompilerParams(dimension_semantics=sem, vmem_limit_bytes=VMEM_LIMIT)


def _full(shape):
    n = len(shape)
    return pl.BlockSpec(shape, lambda *_: (0,) * n)


def _rms(x, w):
    return x * lax.rsqrt(jnp.mean(x * x, axis=-1, keepdims=True) + EPS) * w


def _sigmoid(x):
    return 1.0 / (1.0 + jnp.exp(-x))


def _silu(x):
    return x * _sigmoid(x)


def _softplus(x):
    return jnp.maximum(x, 0.0) + jnp.log1p(jnp.exp(-jnp.abs(x)))


def _log_sigmoid(x):
    return -_softplus(-x)


def _logaddexp(a, b):
    return jnp.maximum(a, b) + jnp.log1p(jnp.exp(-jnp.abs(a - b)))


def _gelu(x):
    return 0.5 * x * (1.0 + jnp.tanh(math.sqrt(2.0 / math.pi) * (x + 0.044715 * x * x * x)))


def _dot(a, b):
    return jnp.dot(a.astype(BF16), b.astype(BF16), preferred_element_type=F32)


def _dot_nt(a, b):
    return lax.dot_general(a.astype(BF16), b.astype(BF16), (((1,), (1,)), ((), ())), preferred_element_type=F32)


def _dot_tn(a, b):
    return lax.dot_general(a.astype(BF16), b.astype(BF16), (((0,), (0,)), ((), ())), preferred_element_type=F32)


def _split(x, terms):
    parts, rest = [], x
    for t in range(terms):
        p = rest.astype(BF16)
        parts.append(p)
        if t + 1 < terms:
            rest = rest - p.astype(F32)
    return parts


def _dot_xl(x, c, terms=3):
    return sum(jnp.dot(p, c, preferred_element_type=F32) for p in _split(x, terms))


def _dot_xr(c, x, terms=3):
    return sum(jnp.dot(c, p, preferred_element_type=F32) for p in _split(x, terms))


def _dot_nt_xl(x, c, terms=3):
    return sum(lax.dot_general(p, c, (((1,), (1,)), ((), ())), preferred_element_type=F32) for p in _split(x, terms))


def _iota2(shape, dim):
    return lax.broadcasted_iota(jnp.int32, shape, dim)


def _head_masks():
    r = _iota2((CHUNK, N_HEADS * CHUNK), 0)
    c = _iota2((CHUNK, N_HEADS * CHUNK), 1) & (CHUNK - 1)
    rr = _iota2((N_HEADS * CHUNK, N_HEADS * HEAD_DIM), 0)
    cc = _iota2((N_HEADS * CHUNK, N_HEADS * HEAD_DIM), 1)
    block = (rr >> 6) == (cc >> 6)
    one = lambda mask: jnp.where(mask, 1.0, 0.0)
    return dict(causal=r >= c, strict=r > c, eye=r == c, same_sub=(r >> 4) == (c >> 4), block=block,
                block16=one(block).astype(BF16), sub16=one((rr >> 4) == (cc >> 4)).astype(BF16),
                eye_sub=one(_iota2((SUB, BR_WIDTH), 0) == (_iota2((SUB, BR_WIDTH), 1) & (SUB - 1))))


def _chunk_tri(tt):
    r = _iota2((tt, tt), 0)
    c = _iota2((tt, tt), 1)
    return jnp.where((r >= c) & ((r >> 6) == (c >> 6)), 1.0, 0.0).astype(BF16)


def _chunk_ones(tt):
    r = _iota2((tt, tt), 0) >> 6
    c = _iota2((tt, tt), 1) >> 6
    return jnp.where(r == c, 1.0, 0.0).astype(BF16)


def _block_ones():
    return _chunk_ones(BR_WIDTH)


def _blockdiag(x, block16):
    return jnp.concatenate([x.astype(BF16)] * N_HEADS, axis=0) * block16


def _head_norm_gate(o, gate, w, ones_blk):
    ms = _dot_xl(o * o, ones_blk, 2) * (1.0 / HEAD_DIM)
    return o * lax.rsqrt(ms + EPS) * w * _silu(gate)


def _in_proj_kernel(x_ref, nw_ref, w_ref, pa_ref, pb_ref, pc_ref, pd_ref):
    xn = _rms(x_ref[...], nw_ref[...]).astype(BF16)
    for i, ref in enumerate((pa_ref, pb_ref, pc_ref, pd_ref)):
        ref[...] = _dot(xn, w_ref[:, SEG_OFF[i]:SEG_OFF[i + 1]])


def _in_proj(x, nw, w, tm):
    t = x.shape[0]
    widths = (SEG_A, SEG_B, SEG_C, SEG_D)
    return pl.pallas_call(
        _in_proj_kernel,
        grid=(t // tm,),
        in_specs=[pl.BlockSpec((tm, D_MODEL), lambda i: (i, 0)), _full((1, D_MODEL)), _full((D_MODEL, N_MIX))],
        out_specs=[pl.BlockSpec((tm, n), lambda i: (i, 0)) for n in widths],
        out_shape=[jax.ShapeDtypeStruct((t, n), F32) for n in widths],
        compiler_params=_cparams("parallel"),
        name="in_proj",
    )(x, nw, w)


def _merge_kernel(h_ref, oa_ref, ob_ref, oc_ref, od_ref, nw_ref, wm_ref, wb_ref, wo_ref, out_ref):
    h = h_ref[...]
    xn = _rms(h, nw_ref[...]).astype(BF16)
    acc = jnp.zeros(h.shape, F32)
    for k, o_ref in enumerate((oa_ref, ob_ref, oc_ref, od_ref)):
        gate = _sigmoid(_dot(xn, wm_ref[:, k * D_MODEL:(k + 1) * D_MODEL]))
        acc = acc + gate * _dot(o_ref[...].astype(BF16), wb_ref[k])
    out_ref[...] = h + _dot(acc.astype(BF16), wo_ref[...])


def _merge(h, oa, ob, oc, od, nw, wm, wb, wo, tm):
    t = h.shape[0]
    row = lambda n: pl.BlockSpec((tm, n), lambda i: (i, 0))
    return pl.pallas_call(
        _merge_kernel,
        grid=(t // tm,),
        in_specs=[row(D_MODEL)] + [row(BR_WIDTH)] * 4 + [
            _full((1, D_MODEL)), _full((D_MODEL, N_BRANCH * D_MODEL)),
            _full((N_BRANCH, BR_WIDTH, D_MODEL)), _full((D_MODEL, D_MODEL))],
        out_specs=row(D_MODEL),
        out_shape=jax.ShapeDtypeStruct((t, D_MODEL), F32),
        compiler_params=_cparams("parallel"),
        name="merge",
    )(h, oa, ob, oc, od, nw, wm, wb, wo)


def _ffn_kernel(h_ref, nw_ref, wg_ref, wu_ref, wd_ref, fw_ref, out_ref, y_ref):
    h = h_ref[...]
    hn = _rms(h, nw_ref[...]).astype(BF16)
    acc = jnp.zeros(h.shape, F32)
    for c in range(D_FF // FF_TILE):
        sl = slice(c * FF_TILE, (c + 1) * FF_TILE)
        a = _silu(_dot(hn, wg_ref[:, sl])) * _dot(hn, wu_ref[:, sl])
        acc = acc + _dot(a.astype(BF16), wd_ref[sl, :])
    hnew = h + acc
    out_ref[...] = hnew
    y_ref[...] = _rms(hnew, fw_ref[...])


def _ffn(h, nw, wg, wu, wd, fw, tm):
    t = h.shape[0]
    row = pl.BlockSpec((tm, D_MODEL), lambda i: (i, 0))
    return pl.pallas_call(
        _ffn_kernel,
        grid=(t // tm,),
        in_specs=[row, _full((1, D_MODEL)), _full((D_MODEL, D_FF)), _full((D_MODEL, D_FF)),
                  _full((D_FF, D_MODEL)), _full((1, D_MODEL))],
        out_specs=[row, row],
        out_shape=[jax.ShapeDtypeStruct((t, D_MODEL), F32)] * 2,
        compiler_params=_cparams("parallel"),
        name="ffn",
    )(h, nw, wg, wu, wd, fw)


def _gla_chunk(q, k, v, b, st, m):
    each = lambda f, *xs: [f(*a) for a in zip(*xs)]
    blk = m["block16"]
    b_last = each(lambda x: x[CHUNK - 1:CHUNK, :], b)
    rows = []
    for i0 in range(0, CHUNK, SUB):
        q_i = each(lambda qq, x: qq[i0:i0 + SUB, :] * jnp.exp(x[i0:i0 + SUB, :] - x[i0:i0 + 1, :]), q, b)
        k_i = each(lambda kk, x: kk * jnp.exp(jnp.minimum(x[i0:i0 + 1, :] - x, EXP_CAP)), k, b)
        rows.append(each(lambda a, c: _dot_nt(a, _blockdiag(c, blk)), q_i, k_i))
    attn = each(lambda *r: jnp.where(m["causal"], jnp.concatenate(r, axis=0), 0.0), *rows)
    o_loc = each(lambda a, vv: _dot(a, _blockdiag(vv, blk)), attn, v)
    kv = each(lambda vv, kk, x, xl: jnp.where(m["block"], _dot_tn(vv, kk * jnp.exp(xl - x)), 0.0), v, k, b, b_last)
    qe = each(lambda qq, x: qq * jnp.exp(x), q, b)
    outs = []
    for c in range(len(q)):
        outs.append(_dot_nt(qe[c], st) + o_loc[c])
        st = st * jnp.exp(b_last[c]) + kv[c]
    return outs, st


def _gla_scan(q_s, k_s, v_s, lf_s, o_s, st_ref, tt):
    m = _head_masks()
    b_all = _dot_xr(_chunk_tri(tt), lf_s[...], 3)
    rows = [slice(c * CHUNK, (c + 1) * CHUNK) for c in range(tt // CHUNK)]
    chunks = lambda ref: [ref[r, :] for r in rows]
    outs, st = _gla_chunk(chunks(q_s), chunks(k_s), chunks(v_s), [b_all[r, :] for r in rows], st_ref[0], m)
    for r, o in zip(rows, outs):
        o_s[r, :] = o
    st_ref[0] = st


def _gla_prompt_kernel(p_ref, gkw_ref, gkb_ref, nw_ref, o_ref, st_ref, q_s, k_s, v_s, lf_s, o_s, *, tt):
    @pl.when(pl.program_id(1) == 0)
    def _():
        st_ref[...] = jnp.zeros(st_ref.shape, F32)

    w = BR_WIDTH
    q_s[...] = p_ref[:, 0:w] * HEAD_DIM ** -0.5
    k_s[...] = p_ref[:, w:2 * w]
    v_s[...] = p_ref[:, 2 * w:3 * w]
    lf_s[...] = _log_sigmoid(_dot(p_ref[:, 4 * w:4 * w + LANE], gkw_ref[...]) + gkb_ref[...]) * (1.0 / GLA_GATE_NORM)
    _gla_scan(q_s, k_s, v_s, lf_s, o_s, st_ref, tt)
    o_ref[...] = _head_norm_gate(o_s[...], p_ref[:, 3 * w:4 * w], nw_ref[...], _block_ones())


def _hgrn_prompt_kernel(p_ref, lb_ref, nw_ref, o_ref, st_ref, q_s, k_s, v_s, lf_s, o_s, *, tt):
    @pl.when(pl.program_id(1) == 0)
    def _():
        st_ref[...] = jnp.zeros(st_ref.shape, F32)

    w = BR_WIDTH
    lb = lb_ref[...]
    xf = p_ref[:, w:2 * w]
    q_s[...] = _silu(p_ref[:, 0:w]) * HEAD_DIM ** -0.5
    k_s[...] = (1.0 - lb) * _sigmoid(-xf)
    v_s[...] = p_ref[:, 2 * w:3 * w]
    lf_s[...] = _hgrn_log_forget(xf, lb)
    _gla_scan(q_s, k_s, v_s, lf_s, o_s, st_ref, tt)
    o_ref[...] = _head_norm_gate(o_s[...], p_ref[:, 3 * w:4 * w], nw_ref[...], _block_ones())


def _hgrn_log_forget(xf, lb):
    ls_pos = _log_sigmoid(xf)
    pos = lb > 0
    lb_safe = jnp.where(pos, lb, 1.0)
    mixed = _logaddexp(ls_pos, jnp.log(lb_safe) + _log_sigmoid(-xf))
    return jnp.where(pos, mixed, ls_pos)


def _linear_prompt(kernel, p, params, bsz, seq, tt):
    nt = seq // tt
    width = p.shape[1]
    return pl.pallas_call(
        functools.partial(kernel, tt=tt),
        grid=(bsz, nt),
        in_specs=[pl.BlockSpec((tt, width), lambda b, t: (b * nt + t, 0))] + [_full(a.shape) for a in params],
        out_specs=[pl.BlockSpec((tt, BR_WIDTH), lambda b, t: (b * nt + t, 0)),
                   pl.BlockSpec((1, BR_WIDTH, BR_WIDTH), lambda b, t: (b, 0, 0))],
        out_shape=[jax.ShapeDtypeStruct((bsz * seq, BR_WIDTH), F32),
                   jax.ShapeDtypeStruct((bsz, BR_WIDTH, BR_WIDTH), F32)],
        scratch_shapes=[pltpu.VMEM((tt, BR_WIDTH), F32)] * 5,
        compiler_params=_cparams("parallel", "arbitrary"),
        name=kernel.__name__.strip("_"),
    )(p, *params)


def _state_from_blockdiag(st):
    b = st.shape[0]
    s = st.reshape(b, N_HEADS, HEAD_DIM, N_HEADS, HEAD_DIM)
    s = jnp.stack([s[:, h, :, h, :] for h in range(N_HEADS)], axis=1)
    return jnp.swapaxes(s, -1, -2)


def _mm_split(a, b, tile_mask):
    a_hi, a_lo = _split(a, 2)
    b_hi, b_lo = (jnp.concatenate([p] * (BR_WIDTH // b.shape[0]), axis=0) * tile_mask for p in _split(b, 2))
    dot = functools.partial(jnp.dot, preferred_element_type=F32)
    return dot(a_hi, b_hi) + (dot(a_lo, b_hi) + dot(a_hi, b_lo))


def _unit_lower_inverse(n, m):
    each = lambda f, *xs: [f(*a) for a in zip(*xs)]
    eye_c, sub, blk = m["eye_sub"], m["sub16"], m["block16"]
    nd = each(lambda a: jnp.where(m["same_sub"], a, 0.0), n)
    low = each(lambda a, d: a - d, n, nd)
    c = each(lambda d: d[0:SUB] + d[SUB:2 * SUB] + d[2 * SUB:3 * SUB] + d[3 * SUB:4 * SUB], nd)
    p = each(lambda a: _mm_split(a, a, sub), c)
    x = each(lambda a: eye_c - a, c)
    for _ in range(2):
        xp = each(lambda a, b: _mm_split(jnp.concatenate([a, b], axis=0), b, sub), x, p)
        x = each(lambda a, b: a + b[0:SUB], x, xp)
        p = each(lambda b: b[SUB:2 * SUB], xp)
    x = each(lambda a, b: a + _mm_split(a, b, sub), x, p)
    dinv = each(lambda a: jnp.where(m["same_sub"], jnp.concatenate([a] * (CHUNK // SUB), axis=0), 0.0), x)
    mm = each(lambda a, b: _mm_split(a, b, blk), dinv, low)
    m2 = each(lambda a: _mm_split(a, a, blk), mm)
    e = each(lambda d, a: d + _mm_split(a, d, blk), dinv, m2)
    return each(lambda a, b: b - _mm_split(a, b, blk), mm, e)


def _gdn_chunk_local(q, k, v, gam_i, gam_j, bexp, m):
    each = lambda f, *xs: [f(*a) for a in zip(*xs)]
    blk = m["block16"]
    decay = each(lambda gi, gj: jnp.where(m["causal"], jnp.exp(jnp.minimum(gi - gj, 0.0)), 0.0), gam_i, gam_j)
    scores = each(lambda kk, qq: _dot_nt(jnp.concatenate([kk, qq], axis=0), _blockdiag(kk, blk)), k, q)
    attn = each(lambda s, d: s[CHUNK:2 * CHUNK] * d, scores, decay)
    n = each(lambda b, s, d: jnp.where(m["strict"], b * s[0:CHUNK] * d, 0.0), bexp, scores, decay)
    tinv = _unit_lower_inverse(n, m)
    u_v = each(lambda t, b, vv: _dot(t, _blockdiag(b * vv, blk)), tinv, bexp, v)
    w_k = each(lambda t, b, gi, kk: _dot(t, _blockdiag(b * jnp.exp(gi) * kk, blk)), tinv, bexp, gam_i, k)
    return u_v, w_k, attn


def _gdn_chunk_state(q, k, gam_i, u_v, w_k, attn, st, m):
    gam_last = gam_i[CHUNK - 1:CHUNK, :]
    ws = _dot_nt(jnp.concatenate([w_k, q], axis=0), st)
    u = u_v - ws[0:CHUNK]
    o = jnp.exp(gam_i) * ws[CHUNK:2 * CHUNK] + _dot(attn, _blockdiag(u, m["block16"]))
    kd = k * jnp.exp(gam_last - gam_i)
    st_new = st * jnp.exp(gam_last) + jnp.where(m["block"], _dot_tn(u, kd), 0.0)
    return o, st_new


def _l2norm_heads(x, ones_blk):
    return x * lax.rsqrt(_dot_xl(x * x, ones_blk, 2) + EPS)


def _gdn_prep(a_alpha_beta, alog_ref, dtb_ref):
    r = _iota2((LANE, BR_WIDTH), 0)
    c = _iota2((LANE, BR_WIDTH), 1) >> 6
    sel_a = jnp.where(r == c, 1.0, 0.0).astype(BF16)
    sel_b = jnp.where(r == c + N_HEADS, 1.0, 0.0).astype(BF16)
    alpha = _dot_xl(a_alpha_beta, sel_a, 3)
    beta = _sigmoid(_dot_xl(a_alpha_beta, sel_b, 3))
    g = -jnp.exp(alog_ref[...]) * _softplus(alpha + dtb_ref[...])
    return g, beta


def _gdn_prompt_kernel(p_ref, cw_ref, alog_ref, dtb_ref, nw_ref, o_ref, st_ref, conv_ref,
                       xp_s, q_s, k_s, v_s, g_s, b_s, o_s, *, tt):
    w = BR_WIDTH
    t = pl.program_id(1)

    @pl.when(t == 0)
    def _():
        st_ref[...] = jnp.zeros(st_ref.shape, F32)
        xp_s[0:8, :] = jnp.zeros((8, 3 * w), F32)

    @pl.when(t > 0)
    def _():
        xp_s[0:8, :] = xp_s[tt:tt + 8, :]

    xp_s[8:8 + tt, :] = p_ref[:, 0:3 * w]
    conv_ref[0] = xp_s[tt:tt + 8, :]
    y = xp_s[8:8 + tt, :] * cw_ref[CONV_W - 1:CONV_W, :]
    for j in range(CONV_W - 1):
        y = y + xp_s[5 + j:5 + j + tt, :] * cw_ref[j:j + 1, :]
    qkv = _silu(y)
    ones_blk = _block_ones()
    q_s[...] = _l2norm_heads(qkv[:, 0:w], ones_blk) * HEAD_DIM ** -0.5
    k_s[...] = _l2norm_heads(qkv[:, w:2 * w], ones_blk)
    v_s[...] = qkv[:, 2 * w:3 * w]
    g, beta = _gdn_prep(p_ref[:, 4 * w:4 * w + LANE], alog_ref, dtb_ref)
    gam_i = _dot_xr(_chunk_tri(tt), g, 3)
    lane_j = _iota2((tt, w), 1) & (CHUNK - 1)
    row_j = _iota2((tt, w), 0) & (CHUNK - 1)
    g_s[...] = gam_i
    b_s[...] = _dot_xr(_chunk_ones(tt), jnp.where(lane_j == row_j, gam_i, 0.0), 3)

    m = _head_masks()
    n_chunks = tt // CHUNK
    rows = [slice(c * CHUNK, (c + 1) * CHUNK) for c in range(n_chunks)]
    chunks = lambda ref: [ref[r, :] for r in rows]
    u_v, w_k, attn = _gdn_chunk_local(chunks(q_s), chunks(k_s), chunks(v_s), chunks(g_s), chunks(b_s),
                                      [beta[r, :] for r in rows], m)
    st = st_ref[0]
    for c, r in enumerate(rows):
        o, st = _gdn_chunk_state(q_s[r, :], k_s[r, :], g_s[r, :], u_v[c], w_k[c], attn[c], st, m)
        o_s[r, :] = o
    st_ref[0] = st
    o_ref[...] = _head_norm_gate(o_s[...], p_ref[:, 3 * w:4 * w], nw_ref[...], ones_blk)


def _gdn_prompt(p, params, bsz, seq, tt):
    nt = seq // tt
    w = BR_WIDTH
    return pl.pallas_call(
        functools.partial(_gdn_prompt_kernel, tt=tt),
        grid=(bsz, nt),
        in_specs=[pl.BlockSpec((tt, SEG_A), lambda b, t: (b * nt + t, 0))] + [_full(a.shape) for a in params],
        out_specs=[pl.BlockSpec((tt, w), lambda b, t: (b * nt + t, 0)),
                   pl.BlockSpec((1, w, w), lambda b, t: (b, 0, 0)),
                   pl.BlockSpec((1, 8, 3 * w), lambda b, t: (b, 0, 0))],
        out_shape=[jax.ShapeDtypeStruct((bsz * seq, w), F32),
                   jax.ShapeDtypeStruct((bsz, w, w), F32),
                   jax.ShapeDtypeStruct((bsz, 8, 3 * w), F32)],
        scratch_shapes=[pltpu.VMEM((tt + 8, 3 * w), F32)] + [pltpu.VMEM((tt, w), F32)] * 6,
        compiler_params=_cparams("parallel", "arbitrary"),
        name="gdn_prompt",
    )(p, *params)


def _s5_kernel(u_ref, x0r_ref, x0i_ref, ar_ref, ai_ref, br_ref, bi_ref, cr_ref, ci_ref, d_ref, glu_ref,
               o_ref, xr_ref, xi_ref, sr_s, si_s, *, steps, rows):
    @pl.when(pl.program_id(0) == 0)
    def _():
        xr_ref[...] = x0r_ref[...]
        xi_ref[...] = x0i_ref[...]

    u = u_ref[...]
    ub = u.astype(BF16)
    sr_s[...] = _dot(ub, br_ref[...])
    si_s[...] = _dot(ub, bi_ref[...])
    a_re = ar_ref[...]
    a_im = ai_ref[...]

    def body(t, carry):
        xr, xi = carry
        r = pl.ds(pl.multiple_of(t * rows, rows), rows)
        nr = a_re * xr - a_im * xi + sr_s[r, :]
        ni = a_re * xi + a_im * xr + si_s[r, :]
        sr_s[r, :] = nr
        si_s[r, :] = ni
        return nr, ni

    xr, xi = lax.fori_loop(0, steps, body, (xr_ref[...], xi_ref[...]))
    xr_ref[...] = xr
    xi_ref[...] = xi
    y = _dot(sr_s[...].astype(BF16), cr_ref[...]) - _dot(si_s[...].astype(BF16), ci_ref[...]) + d_ref[...] * u
    hg = _dot(_gelu(y).astype(BF16), glu_ref[...])
    o_ref[...] = hg[:, 0:BR_WIDTH] * _sigmoid(hg[:, BR_WIDTH:2 * BR_WIDTH])


def _s5(u, x0r, x0i, params, steps_total, rows, steps):
    nt = steps_total // steps
    tile = steps * rows
    return pl.pallas_call(
        functools.partial(_s5_kernel, steps=steps, rows=rows),
        grid=(nt,),
        in_specs=[pl.BlockSpec((tile, BR_WIDTH), lambda t: (t, 0)), _full(x0r.shape), _full(x0i.shape)]
                 + [_full(a.shape) for a in params],
        out_specs=[pl.BlockSpec((tile, BR_WIDTH), lambda t: (t, 0)), _full(x0r.shape), _full(x0i.shape)],
        out_shape=[jax.ShapeDtypeStruct((steps_total * rows, BR_WIDTH), F32),
                   jax.ShapeDtypeStruct(x0r.shape, F32), jax.ShapeDtypeStruct(x0i.shape, F32)],
        scratch_shapes=[pltpu.VMEM((tile, S5_STATE), F32)] * 2,
        compiler_params=_cparams("arbitrary"),
        name="s5",
    )(u, x0r, x0i, *params)


def _s5_params(lam_re, lam_im, b_re, b_im, c_re, c_im, d, log_dt, glu_w):
    lr = jnp.minimum(lam_re, -1e-4)
    li = lam_im
    dt = jnp.exp(log_dt)[:, None]
    mag = jnp.exp(lr * dt)
    ab_re, ab_im = mag * jnp.cos(li * dt), mag * jnp.sin(li * dt)
    den = lr * lr + li * li
    z_re = ((ab_re - 1.0) * lr + ab_im * li) / den
    z_im = (ab_im * lr - (ab_re - 1.0) * li) / den
    bb_re = z_re[..., None] * b_re - z_im[..., None] * b_im
    bb_im = z_re[..., None] * b_im + z_im[..., None] * b_re
    eye = jnp.eye(S5_GROUPS, dtype=F32)

    def pack_in(bb):
        return jnp.einsum('gph,gk->ghkp', bb, eye).reshape(BR_WIDTH, S5_STATE).astype(BF16)

    def pack_out(c):
        return jnp.einsum('ghp,gk->gpkh', c, eye).reshape(S5_STATE, BR_WIDTH).astype(BF16)

    return (ab_re.reshape(1, S5_STATE), ab_im.reshape(1, S5_STATE), pack_in(bb_re), pack_in(bb_im),
            pack_out(c_re), pack_out(c_im), d.reshape(1, BR_WIDTH), glu_w.astype(BF16))


def _expand_mats():
    r = _iota2((HEAD_DIM, HEAD_DIM * HEAD_DIM), 0)
    c = _iota2((HEAD_DIM, HEAD_DIM * HEAD_DIM), 1)
    rep_k = jnp.where(r == (c >> 6), 1.0, 0.0).astype(BF16)
    rep_v = jnp.where(r == (c & (HEAD_DIM - 1)), 1.0, 0.0).astype(BF16)
    return rep_k, rep_v


def _gla_decode_kernel(q_ref, k_ref, v_ref, lf_ref, s_ref, o_ref, sn_ref):
    rep_k, rep_v = _expand_mats()
    q, k, v, lf = q_ref[0], k_ref[0], v_ref[0], lf_ref[0]
    s = s_ref[...]
    dec = jnp.exp(lf)
    qe_x = _dot_xl(q * dec, rep_k, 2)
    o = _dot_nt_xl(qe_x * s, rep_v, 2) + jnp.sum(q * k, axis=-1, keepdims=True) * v
    sn_ref[...] = _dot_xl(dec, rep_k, 3) * s + _dot_xl(k, rep_k, 2) * _dot_xl(v, rep_v, 2)
    o_ref[0] = o


def _gdn_decode_kernel(q_ref, k_ref, v_ref, g_ref, b_ref, s_ref, o_ref, sn_ref):
    rep_k, rep_v = _expand_mats()
    q, k, v = q_ref[0], k_ref[0], v_ref[0]
    eg = jnp.exp(g_ref[0])
    beta = b_ref[0]
    s = s_ref[...]
    k_x = _dot_xl(k, rep_k, 2)
    ks = _dot_nt_xl(k_x * s, rep_v, 2)
    qs = _dot_nt_xl(_dot_xl(q, rep_k, 2) * s, rep_v, 2)
    u = beta * v - (beta * eg) * ks
    o_ref[0] = eg * qs + jnp.sum(q * k, axis=-1, keepdims=True) * u
    sn_ref[...] = eg * s + k_x * _dot_xl(u, rep_v, 2)


def _decode_call(kernel, vecs, state, name):
    rows = state.shape[0]
    hw = HEAD_DIM * HEAD_DIM
    return pl.pallas_call(
        kernel,
        grid=(N_HEADS,),
        in_specs=[pl.BlockSpec((1, rows, a.shape[2]), lambda h: (h, 0, 0)) for a in vecs]
                 + [pl.BlockSpec((rows, hw), lambda h: (0, h))],
        out_specs=[pl.BlockSpec((1, rows, HEAD_DIM), lambda h: (h, 0, 0)), pl.BlockSpec((rows, hw), lambda h: (0, h))],
        out_shape=[jax.ShapeDtypeStruct((N_HEADS, rows, HEAD_DIM), F32), jax.ShapeDtypeStruct(state.shape, F32)],
        compiler_params=_cparams("parallel"),
        name=name,
    )(*vecs, state)


def _decode_prep_kernel(pa_ref, pb_ref, pd_ref, conv_ref, cw_ref, alog_ref, dtb_ref, gkw_ref, gkb_ref, lb_ref,
                        a_ref, ag_ref, ab_ref, b_ref, d_ref, convn_ref):
    w = BR_WIDTH
    ones_blk = _block_ones()
    raw = pa_ref[:, 0:3 * w]
    y = raw * cw_ref[CONV_W - 1:CONV_W, :]
    for j in range(CONV_W - 1):
        y = y + conv_ref[j] * cw_ref[j:j + 1, :]
    convn_ref[0] = conv_ref[1]
    convn_ref[1] = conv_ref[2]
    convn_ref[2] = raw
    qkv = _silu(y)
    a_ref[0] = _l2norm_heads(qkv[:, 0:w], ones_blk) * HEAD_DIM ** -0.5
    a_ref[1] = _l2norm_heads(qkv[:, w:2 * w], ones_blk)
    a_ref[2] = qkv[:, 2 * w:3 * w]
    g, beta = _gdn_prep(pa_ref[:, 4 * w:4 * w + LANE], alog_ref, dtb_ref)
    ag_ref[...] = g
    ab_ref[...] = beta
    b_ref[0] = pb_ref[:, 0:w] * HEAD_DIM ** -0.5
    b_ref[1] = pb_ref[:, w:2 * w]
    b_ref[2] = pb_ref[:, 2 * w:3 * w]
    b_ref[3] = _log_sigmoid(_dot(pb_ref[:, 4 * w:4 * w + LANE], gkw_ref[...]) + gkb_ref[...]) * (1.0 / GLA_GATE_NORM)
    lb = lb_ref[...]
    xf = pd_ref[:, w:2 * w]
    d_ref[0] = _silu(pd_ref[:, 0:w]) * HEAD_DIM ** -0.5
    d_ref[1] = (1.0 - lb) * _sigmoid(-xf)
    d_ref[2] = pd_ref[:, 2 * w:3 * w]
    d_ref[3] = _hgrn_log_forget(xf, lb)


def _decode_prep(pa, pb, pd, conv, params):
    rows = pa.shape[0]
    w = BR_WIDTH
    ins = (pa, pb, pd, conv) + tuple(params)
    shapes = [(3, rows, w), (rows, w), (rows, w), (4, rows, w), (4, rows, w), (CONV_W - 1, rows, 3 * w)]
    return pl.pallas_call(
        _decode_prep_kernel,
        grid=(1,),
        in_specs=[_full(a.shape) for a in ins],
        out_specs=[_full(s) for s in shapes],
        out_shape=[jax.ShapeDtypeStruct(s, F32) for s in shapes],
        compiler_params=_cparams("arbitrary"),
        name="decode_prep",
    )(*ins)


def _decode_post_kernel(oa_ref, ob_ref, od_ref, pa_ref, pb_ref, pd_ref, nwa_ref, nwb_ref, nwd_ref,
                        a_ref, b_ref, d_ref):
    w = BR_WIDTH
    ones_blk = _block_ones()
    a_ref[...] = _head_norm_gate(oa_ref[...], pa_ref[:, 3 * w:4 * w], nwa_ref[...], ones_blk)
    b_ref[...] = _head_norm_gate(ob_ref[...], pb_ref[:, 3 * w:4 * w], nwb_ref[...], ones_blk)
    d_ref[...] = _head_norm_gate(od_ref[...], pd_ref[:, 3 * w:4 * w], nwd_ref[...], ones_blk)


def _decode_post(oa, ob, od, pa, pb, pd, nwa, nwb, nwd):
    ins = (oa, ob, od, pa, pb, pd, nwa, nwb, nwd)
    shp = oa.shape
    return pl.pallas_call(
        _decode_post_kernel,
        grid=(1,),
        in_specs=[_full(a.shape) for a in ins],
        out_specs=[_full(shp)] * 3,
        out_shape=[jax.ShapeDtypeStruct(shp, F32)] * 3,
        compiler_params=_cparams("arbitrary"),
        name="decode_post",
    )(*ins)


def _to_heads(x):
    return jnp.transpose(x.reshape(x.shape[0], N_HEADS, HEAD_DIM), (1, 0, 2))


def _from_heads(x):
    return jnp.transpose(x, (1, 0, 2)).reshape(x.shape[1], BR_WIDTH)


def _head_scalar(x):
    return jnp.transpose(x[:, ::HEAD_DIM], (1, 0))[:, :, None]


def _pack_w_in(w_in):
    sizes = (3 * BR_WIDTH, N_HEADS, N_HEADS, BR_WIDTH, BR_WIDTH, BR_WIDTH, BR_WIDTH, GLA_GATE_RANK, BR_WIDTH,
             BR_WIDTH, BR_WIDTH, BR_WIDTH, BR_WIDTH, BR_WIDTH, N_BRANCH * D_MODEL)
    parts, off = [], 0
    for n in sizes:
        parts.append(w_in[:, off:off + n])
        off += n
    (a_qkv, a_alpha, a_beta, a_gate, b_q, b_k, b_v, b_gk, b_gate, c_u, d_q, d_f, d_i, d_gate, merge) = parts
    zpad = lambda n: jnp.zeros((D_MODEL, n), w_in.dtype)
    mix = jnp.concatenate([a_qkv, a_gate, a_alpha, a_beta, zpad(LANE - 2 * N_HEADS),
                           b_q, b_k, b_v, b_gate, b_gk, zpad(LANE - GLA_GATE_RANK),
                           c_u, d_q, d_f, d_i, d_gate], axis=1)
    return mix.astype(BF16), merge.astype(BF16)


def _hgrn_lower_bounds(logits):
    p = jax.nn.softmax(logits, axis=0)
    return jnp.cumsum(p, axis=0) - p[0]


def _row(x):
    return x.reshape(1, -1)


def _rep_heads(x):
    return jnp.repeat(x, HEAD_DIM).reshape(1, BR_WIDTH)


def kernel(x_prompt, x_sample, state_gdn_conv, state_gdn, state_gla, state_s5_re, state_s5_im, state_hgrn, norm1_w, w_in, gdn_conv_w, gdn_a_log, gdn_dt_bias, gdn_norm_w, gla_gk_w, gla_gk_b, gla_norm_w, s5_lambda_re, s5_lambda_im, s5_b_re, s5_b_im, s5_c_re, s5_c_im, s5_d, s5_log_dt, s5_glu_w, hgrn_lb_logits, hgrn_norm_w, w_branch, w_out, norm2_w, ffn_w_gate, ffn_w_up, ffn_w_down, final_norm_w):
    bsz, seq, _ = x_prompt.shape
    dbs = x_sample.shape[0]
    tm_p = 512 if (bsz * seq) % 512 == 0 else bsz * seq
    tt = 256 if seq % 256 == 0 else seq
    s5_steps = 128 if seq % 128 == 0 else seq
    lb_all = _hgrn_lower_bounds(hgrn_lb_logits)
    hp = x_prompt.reshape(bsz * seq, D_MODEL)
    hs = x_sample.reshape(dbs, D_MODEL)
    zeros_p = jnp.zeros((bsz, S5_STATE), F32)
    fw = _row(final_norm_w)
    outs_p = [[] for _ in range(6)]
    outs_s = [[] for _ in range(6)]
    yp = ys = None
    for i in range(DEPTH):
        w_mix, w_merge = _pack_w_in(w_in[i])
        nw1, nw2 = _row(norm1_w[i]), _row(norm2_w[i])
        gkw = jnp.concatenate([gla_gk_w[i], jnp.zeros((LANE - GLA_GATE_RANK, BR_WIDTH), F32)], axis=0)
        gdn_params = (gdn_conv_w[i], _rep_heads(gdn_a_log[i]), _rep_heads(gdn_dt_bias[i]), _row(gdn_norm_w[i]))
        gla_params = (gkw, _row(gla_gk_b[i]), _row(gla_norm_w[i]))
        hgrn_params = (_row(lb_all[i]), _row(hgrn_norm_w[i]))
        s5_params = _s5_params(s5_lambda_re[i], s5_lambda_im[i], s5_b_re[i], s5_b_im[i], s5_c_re[i], s5_c_im[i],
                               s5_d[i], s5_log_dt[i], s5_glu_w[i])
        wb, wo = w_branch[i].astype(BF16), w_out[i].astype(BF16)
        wg, wu, wd = ffn_w_gate[i].astype(BF16), ffn_w_up[i].astype(BF16), ffn_w_down[i].astype(BF16)

        pa, pb, pc, pd = _in_proj(hp, nw1, w_mix, tm_p)
        o_a, st_a, conv_p = _gdn_prompt(pa, gdn_params, bsz, seq, tt)
        o_b, st_b = _linear_prompt(_gla_prompt_kernel, pb, gla_params, bsz, seq, tt)
        o_d, st_d = _linear_prompt(_hgrn_prompt_kernel, pd, hgrn_params, bsz, seq, tt)
        u_tm = jnp.transpose(pc.reshape(bsz, seq, BR_WIDTH), (1, 0, 2)).reshape(seq * bsz, BR_WIDTH)
        o_c_tm, xr_p, xi_p = _s5(u_tm, zeros_p, zeros_p, s5_params, seq, bsz, s5_steps)
        o_c = jnp.transpose(o_c_tm.reshape(seq, bsz, BR_WIDTH), (1, 0, 2)).reshape(bsz * seq, BR_WIDTH)
        hp = _merge(hp, o_a, o_b, o_c, o_d, nw1, w_merge, wb, wo, tm_p)
        hp, yp = _ffn(hp, nw2, wg, wu, wd, fw, tm_p)
        for lst, s in zip(outs_p, (conv_p[:, 5:8, :], _state_from_blockdiag(st_a), _state_from_blockdiag(st_b),
                                   xr_p.reshape(bsz, S5_GROUPS, S5_P), xi_p.reshape(bsz, S5_GROUPS, S5_P),
                                   _state_from_blockdiag(st_d))):
            lst.append(s)

        pa, pb, pc, pd = _in_proj(hs, nw1, w_mix, dbs)
        conv_in = jnp.transpose(state_gdn_conv[i], (1, 0, 2))
        va, g_a, beta_a, vb, vd, conv_s = _decode_prep(
            pa, pb, pd, conv_in,
            (gdn_conv_w[i], gdn_params[1], gdn_params[2], gkw, gla_params[1], hgrn_params[0]))
        flat = lambda s: s.reshape(dbs, N_HEADS * HEAD_DIM * HEAD_DIM)
        oa_h, sa = _decode_call(_gdn_decode_kernel,
                                [_to_heads(va[0]), _to_heads(va[1]), _to_heads(va[2]), _head_scalar(g_a), _head_scalar(beta_a)],
                                flat(state_gdn[i]), "gdn_decode")
        ob_h, sb = _decode_call(_gla_decode_kernel, [_to_heads(vb[j]) for j in range(4)], flat(state_gla[i]), "gla_decode")
        od_h, sd = _decode_call(_gla_decode_kernel, [_to_heads(vd[j]) for j in range(4)], flat(state_hgrn[i]), "hgrn_decode")
        o_a, o_b, o_d = _decode_post(_from_heads(oa_h), _from_heads(ob_h), _from_heads(od_h), pa, pb, pd,
                                     gdn_params[3], gla_params[2], hgrn_params[1])
        o_c, xr_s, xi_s = _s5(pc, state_s5_re[i].reshape(dbs, S5_STATE), state_s5_im[i].reshape(dbs, S5_STATE),
                              s5_params, 1, dbs, 1)
        hs = _merge(hs, o_a, o_b, o_c, o_d, nw1, w_merge, wb, wo, dbs)
        hs, ys = _ffn(hs, nw2, wg, wu, wd, fw, dbs)
        st5 = lambda s: s.reshape(dbs, N_HEADS, HEAD_DIM, HEAD_DIM)
        for lst, s in zip(outs_s, (jnp.transpose(conv_s, (1, 0, 2)), st5(sa), st5(sb),
                                   xr_s.reshape(dbs, S5_GROUPS, S5_P), xi_s.reshape(dbs, S5_GROUPS, S5_P), st5(sd))):
            lst.append(s)

    return (yp.reshape(bsz, seq, D_MODEL), ys.reshape(dbs, 1, D_MODEL),
            *[jnp.stack(l) for l in outs_p], *[jnp.stack(l) for l in outs_s])
```

```python
import functools
import math

import jax
import jax.numpy as jnp
from jax import lax
from jax.experimental import pallas as pl
from jax.experimental.pallas import tpu as pltpu

F32 = jnp.float32
BF16 = jnp.bfloat16

D_MODEL = 1024
DEPTH = 4
N_BRANCH = 4
BR_WIDTH = D_MODEL // N_BRANCH
HEAD_DIM = 64
N_HEADS = BR_WIDTH // HEAD_DIM
CONV_W = 4
CHUNK = 64
SUB = 16
GLA_GATE_RANK = 16
GLA_GATE_NORM = 16.0
S5_GROUP = 16
S5_GROUPS = BR_WIDTH // S5_GROUP
S5_P = 64
S5_STATE = S5_GROUPS * S5_P
D_FF = -(-8 * D_MODEL // (3 * 256)) * 256
FF_TILE = 256
EPS = 1e-6
EXP_CAP = 60.0
LANE = 128
SEG_A = 3 * BR_WIDTH + BR_WIDTH + LANE
SEG_B = 4 * BR_WIDTH + LANE
SEG_C = BR_WIDTH
SEG_D = 4 * BR_WIDTH
SEG_OFF = (0, SEG_A, SEG_A + SEG_B, SEG_A + SEG_B + SEG_C, SEG_A + SEG_B + SEG_C + SEG_D)
N_MIX = SEG_OFF[-1]
VMEM_LIMIT = 56 * 1024 * 1024


def _cparams(*sem):
    return pltpu.CompilerParams(dimension_semantics=sem, vmem_limit_bytes=VMEM_LIMIT)


def _full(shape):
    n = len(shape)
    return pl.BlockSpec(shape, lambda *_: (0,) * n)


def _rms(x, w):
    return x * lax.rsqrt(jnp.mean(x * x, axis=-1, keepdims=True) + EPS) * w


def _sigmoid(x):
    return 1.0 / (1.0 + jnp.exp(-x))


def _silu(x):
    return x * _sigmoid(x)


def _softplus(x):
    return jnp.maximum(x, 0.0) + jnp.log1p(jnp.exp(-jnp.abs(x)))


def _log_sigmoid(x):
    return -_softplus(-x)


def _logaddexp(a, b):
    return jnp.maximum(a, b) + jnp.log1p(jnp.exp(-jnp.abs(a - b)))


def _gelu(x):
    return 0.5 * x * (1.0 + jnp.tanh(math.sqrt(2.0 / math.pi) * (x + 0.044715 * x * x * x)))


def _dot(a, b):
    return jnp.dot(a.astype(BF16), b.astype(BF16), preferred_element_type=F32)


def _dot_nt(a, b):
    return lax.dot_general(a.astype(BF16), b.astype(BF16), (((1,), (1,)), ((), ())), preferred_element_type=F32)


def _dot_tn(a, b):
    return lax.dot_general(a.astype(BF16), b.astype(BF16), (((0,), (0,)), ((), ())), preferred_element_type=F32)


def _split(x, terms):
    parts, rest = [], x
    for t in range(terms):
        p = rest.astype(BF16)
        parts.append(p)
        if t + 1 < terms:
            rest = rest - p.astype(F32)
    return parts


def _dot_xl(x, c, terms=3):
    return sum(jnp.dot(p, c, preferred_element_type=F32) for p in _split(x, terms))


def _dot_xr(c, x, terms=3):
    return sum(jnp.dot(c, p, preferred_element_type=F32) for p in _split(x, terms))


def _dot_nt_xl(x, c, terms=3):
    return sum(lax.dot_general(p, c, (((1,), (1,)), ((), ())), preferred_element_type=F32) for p in _split(x, terms))


def _iota2(shape, dim):
    return lax.broadcasted_iota(jnp.int32, shape, dim)


def _head_masks():
    r = _iota2((CHUNK, N_HEADS * CHUNK), 0)
    c = _iota2((CHUNK, N_HEADS * CHUNK), 1) & (CHUNK - 1)
    rr = _iota2((N_HEADS * CHUNK, N_HEADS * HEAD_DIM), 0)
    cc = _iota2((N_HEADS * CHUNK, N_HEADS * HEAD_DIM), 1)
    block = (rr >> 6) == (cc >> 6)
    one = lambda mask: jnp.where(mask, 1.0, 0.0)
    return dict(causal=r >= c, strict=r > c, eye=r == c, same_sub=(r >> 4) == (c >> 4), block=block,
                block16=one(block).astype(BF16), sub16=one((rr >> 4) == (cc >> 4)).astype(BF16),
                eye_sub=one(_iota2((SUB, BR_WIDTH), 0) == (_iota2((SUB, BR_WIDTH), 1) & (SUB - 1))))


def _chunk_tri(tt):
    r = _iota2((tt, tt), 0)
    c = _iota2((tt, tt), 1)
    return jnp.where((r >= c) & ((r >> 6) == (c >> 6)), 1.0, 0.0).astype(BF16)


def _chunk_ones(tt):
    r = _iota2((tt, tt), 0) >> 6
    c = _iota2((tt, tt), 1) >> 6
    return jnp.where(r == c, 1.0, 0.0).astype(BF16)


def _block_ones():
    return _chunk_ones(BR_WIDTH)


def _blockdiag(x, block16):
    return jnp.concatenate([x.astype(BF16)] * N_HEADS, axis=0) * block16


def _head_norm_gate(o, gate, w, ones_blk):
    ms = _dot_xl(o * o, ones_blk, 2) * (1.0 / HEAD_DIM)
    return o * lax.rsqrt(ms + EPS) * w * _silu(gate)


def _in_proj_kernel(x_ref, nw_ref, w_ref, pa_ref, pb_ref, pc_ref, pd_ref):
    xn = _rms(x_ref[...], nw_ref[...]).astype(BF16)
    for i, ref in enumerate((pa_ref, pb_ref, pc_ref, pd_ref)):
        ref[...] = _dot(xn, w_ref[:, SEG_OFF[i]:SEG_OFF[i + 1]])


def _time_major_spec(tm, seq):
    nt = seq // tm
    return pl.BlockSpec((tm, BR_WIDTH), lambda i: (i % nt, i // nt))


def _in_proj(x, nw, w, tm, seq):
    t = x.shape[0]
    widths = (SEG_A, SEG_B, SEG_C, SEG_D)
    out_specs = [pl.BlockSpec((tm, n), lambda i: (i, 0)) for n in widths]
    out_shape = [jax.ShapeDtypeStruct((t, n), F32) for n in widths]
    out_specs[2] = _time_major_spec(tm, seq)
    out_shape[2] = jax.ShapeDtypeStruct((seq, (t // seq) * BR_WIDTH), F32)
    return pl.pallas_call(
        _in_proj_kernel,
        grid=(t // tm,),
        in_specs=[pl.BlockSpec((tm, D_MODEL), lambda i: (i, 0)), _full((1, D_MODEL)), _full((D_MODEL, N_MIX))],
        out_specs=out_specs,
        out_shape=out_shape,
        compiler_params=_cparams("parallel"),
        name="in_proj",
    )(x, nw, w)


def _merge_kernel(h_ref, oa_ref, ob_ref, oc_ref, od_ref, nw_ref, wm_ref, wb_ref, wo_ref, out_ref):
    h = h_ref[...]
    xn = _rms(h, nw_ref[...]).astype(BF16)
    o_refs = (oa_ref, ob_ref, oc_ref, od_ref)
    proj = lambda k: (_dot(xn, wm_ref[:, k * D_MODEL:(k + 1) * D_MODEL]), _dot(o_refs[k][...], wb_ref[k]))
    acc = jnp.zeros(h.shape, F32)
    nxt = proj(0)
    for k in range(N_BRANCH):
        gate, br = nxt
        if k + 1 < N_BRANCH:
            nxt = proj(k + 1)
        acc = acc + _sigmoid(gate) * br
    out_ref[...] = h + _dot(acc, wo_ref[...])


def _merge(h, oa, ob, oc, od, nw, wm, wb, wo, tm, seq):
    t = h.shape[0]
    row = lambda n: pl.BlockSpec((tm, n), lambda i: (i, 0))
    return pl.pallas_call(
        _merge_kernel,
        grid=(t // tm,),
        in_specs=[row(D_MODEL), row(BR_WIDTH), row(BR_WIDTH), _time_major_spec(tm, seq), row(BR_WIDTH)] + [
            _full((1, D_MODEL)), _full((D_MODEL, N_BRANCH * D_MODEL)),
            _full((N_BRANCH, BR_WIDTH, D_MODEL)), _full((D_MODEL, D_MODEL))],
        out_specs=row(D_MODEL),
        out_shape=jax.ShapeDtypeStruct((t, D_MODEL), F32),
        compiler_params=_cparams("parallel"),
        name="merge",
    )(h, oa, ob, oc, od, nw, wm, wb, wo)


def _ffn_kernel(h_ref, nw_ref, wg_ref, wu_ref, wd_ref, fw_ref, out_ref, *, final):
    h = h_ref[...]
    hn = _rms(h, nw_ref[...]).astype(BF16)
    n_tiles = D_FF // FF_TILE
    cols = [slice(c * FF_TILE, (c + 1) * FF_TILE) for c in range(n_tiles)]
    up = lambda sl: (_dot(hn, wg_ref[:, sl]), _dot(hn, wu_ref[:, sl]))
    acc = jnp.zeros(h.shape, F32)
    nxt = up(cols[0])
    for c in range(n_tiles):
        g, u = nxt
        if c + 1 < n_tiles:
            nxt = up(cols[c + 1])
        acc = acc + _dot(_silu(g) * u, wd_ref[cols[c], :])
    hnew = h + acc
    out_ref[...] = _rms(hnew, fw_ref[...]) if final else hnew


def _ffn(h, nw, wg, wu, wd, fw, tm, final):
    t = h.shape[0]
    row = pl.BlockSpec((tm, D_MODEL), lambda i: (i, 0))
    return pl.pallas_call(
        functools.partial(_ffn_kernel, final=final),
        grid=(t // tm,),
        in_specs=[row, _full((1, D_MODEL)), _full((D_MODEL, D_FF)), _full((D_MODEL, D_FF)),
                  _full((D_FF, D_MODEL)), _full((1, D_MODEL))],
        out_specs=row,
        out_shape=jax.ShapeDtypeStruct((t, D_MODEL), F32),
        compiler_params=_cparams("parallel"),
        name="ffn",
    )(h, nw, wg, wu, wd, fw)


def _gla_chunk(q, k, v, b, st, m):
    each = lambda f, *xs: [f(*a) for a in zip(*xs)]
    blk = m["block16"]
    b_last = each(lambda x: x[CHUNK - 1:CHUNK, :], b)
    rows = []
    for i0 in range(0, CHUNK, SUB):
        q_i = each(lambda qq, x: qq[i0:i0 + SUB, :] * jnp.exp(x[i0:i0 + SUB, :] - x[i0:i0 + 1, :]), q, b)
        k_i = each(lambda kk, x: kk * jnp.exp(jnp.minimum(x[i0:i0 + 1, :] - x, EXP_CAP)), k, b)
        rows.append(each(lambda a, c: _dot_nt(a, _blockdiag(c, blk)), q_i, k_i))
    attn = each(lambda *r: jnp.where(m["causal"], jnp.concatenate(r, axis=0), 0.0), *rows)
    o_loc = each(lambda a, vv: _dot(a, _blockdiag(vv, blk)), attn, v)
    kv = each(lambda vv, kk, x, xl: jnp.where(m["block"], _dot_tn(vv, kk * jnp.exp(xl - x)), 0.0), v, k, b, b_last)
    qe = each(lambda qq, x: qq * jnp.exp(x), q, b)
    outs = []
    for c in range(len(q)):
        outs.append(_dot_nt(qe[c], st) + o_loc[c])
        st = st * jnp.exp(b_last[c]) + kv[c]
    return outs, st


def _write_state(out_ref, st):
    r = _iota2((BR_WIDTH, BR_WIDTH), 0)
    c = _iota2((BR_WIDTH, BR_WIDTH), 1)
    eye = jnp.where(r == c, 1.0, 0.0).astype(BF16)
    nt_dot = lambda p: lax.dot_general(eye, p, (((1,), (1,)), ((), ())), preferred_element_type=F32)
    st_t = sum(nt_dot(p) for p in _split(st, 3))
    for h in range(N_HEADS):
        out_ref[0, h] = st_t[h * HEAD_DIM:(h + 1) * HEAD_DIM, h * HEAD_DIM:(h + 1) * HEAD_DIM]


def _gla_scan(q_s, k_s, v_s, lf_s, o_s, st_s, sout_ref, tt):
    m = _head_masks()
    b_all = _dot_xr(_chunk_tri(tt), lf_s[...], 3)
    rows = [slice(c * CHUNK, (c + 1) * CHUNK) for c in range(tt // CHUNK)]
    chunks = lambda ref: [ref[r, :] for r in rows]
    outs, st = _gla_chunk(chunks(q_s), chunks(k_s), chunks(v_s), [b_all[r, :] for r in rows], st_s[...], m)
    for r, o in zip(rows, outs):
        o_s[r, :] = o
    st_s[...] = st

    @pl.when(pl.program_id(1) == pl.num_programs(1) - 1)
    def _():
        _write_state(sout_ref, st_s[...])


def _gla_prompt_kernel(p_ref, gkw_ref, gkb_ref, nw_ref, o_ref, sout_ref, st_s, q_s, k_s, v_s, lf_s, o_s, *, tt):
    @pl.when(pl.program_id(1) == 0)
    def _():
        st_s[...] = jnp.zeros(st_s.shape, F32)

    w = BR_WIDTH
    q_s[...] = p_ref[:, 0:w] * HEAD_DIM ** -0.5
    k_s[...] = p_ref[:, w:2 * w]
    v_s[...] = p_ref[:, 2 * w:3 * w]
    lf_s[...] = _log_sigmoid(_dot(p_ref[:, 4 * w:4 * w + LANE], gkw_ref[...]) + gkb_ref[...]) * (1.0 / GLA_GATE_NORM)
    _gla_scan(q_s, k_s, v_s, lf_s, o_s, st_s, sout_ref, tt)
    o_ref[...] = _head_norm_gate(o_s[...], p_ref[:, 3 * w:4 * w], nw_ref[...], _block_ones())


def _hgrn_prompt_kernel(p_ref, lb_ref, nw_ref, o_ref, sout_ref, st_s, q_s, k_s, v_s, lf_s, o_s, *, tt):
    @pl.when(pl.program_id(1) == 0)
    def _():
        st_s[...] = jnp.zeros(st_s.shape, F32)

    w = BR_WIDTH
    lb = lb_ref[...]
    xf = p_ref[:, w:2 * w]
    q_s[...] = _silu(p_ref[:, 0:w]) * HEAD_DIM ** -0.5
    k_s[...] = (1.0 - lb) * _sigmoid(-xf)
    v_s[...] = p_ref[:, 2 * w:3 * w]
    lf_s[...] = _hgrn_log_forget(xf, lb)
    _gla_scan(q_s, k_s, v_s, lf_s, o_s, st_s, sout_ref, tt)
    o_ref[...] = _head_norm_gate(o_s[...], p_ref[:, 3 * w:4 * w], nw_ref[...], _block_ones())


def _hgrn_log_forget(xf, lb):
    ls_pos = _log_sigmoid(xf)
    pos = lb > 0
    lb_safe = jnp.where(pos, lb, 1.0)
    mixed = _logaddexp(ls_pos, jnp.log(lb_safe) + _log_sigmoid(-xf))
    return jnp.where(pos, mixed, ls_pos)


def _linear_prompt(kernel, p, params, bsz, seq, tt):
    nt = seq // tt
    width = p.shape[1]
    return pl.pallas_call(
        functools.partial(kernel, tt=tt),
        grid=(bsz, nt),
        in_specs=[pl.BlockSpec((tt, width), lambda b, t: (b * nt + t, 0))] + [_full(a.shape) for a in params],
        out_specs=[pl.BlockSpec((tt, BR_WIDTH), lambda b, t: (b * nt + t, 0)), _STATE_SPEC],
        out_shape=[jax.ShapeDtypeStruct((bsz * seq, BR_WIDTH), F32),
                   jax.ShapeDtypeStruct((bsz, N_HEADS, HEAD_DIM, HEAD_DIM), F32)],
        scratch_shapes=[pltpu.VMEM((BR_WIDTH, BR_WIDTH), F32)] + [pltpu.VMEM((tt, BR_WIDTH), F32)] * 5,
        compiler_params=_cparams("parallel", "arbitrary"),
        name=kernel.__name__.strip("_"),
    )(p, *params)


_STATE_SPEC = pl.BlockSpec((1, N_HEADS, HEAD_DIM, HEAD_DIM), lambda b, t: (b, 0, 0, 0))


def _mm_split(a, b, tile_mask):
    a_hi, a_lo = _split(a, 2)
    b_hi, b_lo = (jnp.concatenate([p] * (BR_WIDTH // b.shape[0]), axis=0) * tile_mask for p in _split(b, 2))
    dot = functools.partial(jnp.dot, preferred_element_type=F32)
    return dot(a_hi, b_hi) + (dot(a_lo, b_hi) + dot(a_hi, b_lo))


def _unit_lower_inverse(n, m):
    each = lambda f, *xs: [f(*a) for a in zip(*xs)]
    eye_c, sub, blk = m["eye_sub"], m["sub16"], m["block16"]
    nd = each(lambda a: jnp.where(m["same_sub"], a, 0.0), n)
    low = each(lambda a, d: a - d, n, nd)
    c = each(lambda d: d[0:SUB] + d[SUB:2 * SUB] + d[2 * SUB:3 * SUB] + d[3 * SUB:4 * SUB], nd)
    p = each(lambda a: _mm_split(a, a, sub), c)
    x = each(lambda a: eye_c - a, c)
    for _ in range(2):
        xp = each(lambda a, b: _mm_split(jnp.concatenate([a, b], axis=0), b, sub), x, p)
        x = each(lambda a, b: a + b[0:SUB], x, xp)
        p = each(lambda b: b[SUB:2 * SUB], xp)
    x = each(lambda a, b: a + _mm_split(a, b, sub), x, p)
    dinv = each(lambda a: jnp.where(m["same_sub"], jnp.concatenate([a] * (CHUNK // SUB), axis=0), 0.0), x)
    mm = each(lambda a, b: _mm_split(a, b, blk), dinv, low)
    m2 = each(lambda a: _mm_split(a, a, blk), mm)
    e = each(lambda d, a: d + _mm_split(a, d, blk), dinv, m2)
    return each(lambda a, b: b - _mm_split(a, b, blk), mm, e)


def _gdn_chunk_local(q, k, v, gam_i, gam_j, bexp, m):
    each = lambda f, *xs: [f(*a) for a in zip(*xs)]
    blk = m["block16"]
    decay = each(lambda gi, gj: jnp.where(m["causal"], jnp.exp(jnp.minimum(gi - gj, 0.0)), 0.0), gam_i, gam_j)
    scores = each(lambda kk, qq: _dot_nt(jnp.concatenate([kk, qq], axis=0), _blockdiag(kk, blk)), k, q)
    attn = each(lambda s, d: s[CHUNK:2 * CHUNK] * d, scores, decay)
    n = each(lambda b, s, d: jnp.where(m["strict"], b * s[0:CHUNK] * d, 0.0), bexp, scores, decay)
    tinv = _unit_lower_inverse(n, m)
    u_v = each(lambda t, b, vv: _dot(t, _blockdiag(b * vv, blk)), tinv, bexp, v)
    w_k = each(lambda t, b, gi, kk: _dot(t, _blockdiag(b * jnp.exp(gi) * kk, blk)), tinv, bexp, gam_i, k)
    return u_v, w_k, attn


def _gdn_chunk_state(q, k, gam_i, u_v, w_k, attn, st, m):
    gam_last = gam_i[CHUNK - 1:CHUNK, :]
    ws = _dot_nt(jnp.concatenate([w_k, q], axis=0), st)
    u = u_v - ws[0:CHUNK]
    o = jnp.exp(gam_i) * ws[CHUNK:2 * CHUNK] + _dot(attn, _blockdiag(u, m["block16"]))
    kd = k * jnp.exp(gam_last - gam_i)
    st_new = st * jnp.exp(gam_last) + jnp.where(m["block"], _dot_tn(u, kd), 0.0)
    return o, st_new


def _l2norm_heads(x, ones_blk):
    return x * lax.rsqrt(_dot_xl(x * x, ones_blk, 2) + EPS)


def _gdn_prep(a_alpha_beta, alog_ref, dtb_ref):
    r = _iota2((LANE, BR_WIDTH), 0)
    c = _iota2((LANE, BR_WIDTH), 1) >> 6
    sel_a = jnp.where(r == c, 1.0, 0.0).astype(BF16)
    sel_b = jnp.where(r == c + N_HEADS, 1.0, 0.0).astype(BF16)
    alpha = _dot_xl(a_alpha_beta, sel_a, 3)
    beta = _sigmoid(_dot_xl(a_alpha_beta, sel_b, 3))
    g = -jnp.exp(alog_ref[...]) * _softplus(alpha + dtb_ref[...])
    return g, beta


def _gdn_prompt_kernel(p_ref, cw_ref, alog_ref, dtb_ref, nw_ref, o_ref, sout_ref, conv_ref,
                       st_s, xp_s, q_s, k_s, v_s, g_s, b_s, o_s, *, tt):
    w = BR_WIDTH
    t = pl.program_id(1)

    @pl.when(t == 0)
    def _():
        st_s[...] = jnp.zeros(st_s.shape, F32)
        xp_s[0:8, :] = jnp.zeros((8, 3 * w), F32)

    @pl.when(t > 0)
    def _():
        xp_s[0:8, :] = xp_s[tt:tt + 8, :]

    xp_s[8:8 + tt, :] = p_ref[:, 0:3 * w]
    conv_ref[0] = xp_s[tt + 8 - (CONV_W - 1):tt + 8, :]
    y = xp_s[8:8 + tt, :] * cw_ref[CONV_W - 1:CONV_W, :]
    for j in range(CONV_W - 1):
        y = y + xp_s[5 + j:5 + j + tt, :] * cw_ref[j:j + 1, :]
    qkv = _silu(y)
    ones_blk = _block_ones()
    q_s[...] = _l2norm_heads(qkv[:, 0:w], ones_blk) * HEAD_DIM ** -0.5
    k_s[...] = _l2norm_heads(qkv[:, w:2 * w], ones_blk)
    v_s[...] = qkv[:, 2 * w:3 * w]
    g, beta = _gdn_prep(p_ref[:, 4 * w:4 * w + LANE], alog_ref, dtb_ref)
    gam_i = _dot_xr(_chunk_tri(tt), g, 3)
    lane_j = _iota2((tt, w), 1) & (CHUNK - 1)
    row_j = _iota2((tt, w), 0) & (CHUNK - 1)
    g_s[...] = gam_i
    b_s[...] = _dot_xr(_chunk_ones(tt), jnp.where(lane_j == row_j, gam_i, 0.0), 3)

    m = _head_masks()
    n_chunks = tt // CHUNK
    rows = [slice(c * CHUNK, (c + 1) * CHUNK) for c in range(n_chunks)]
    chunks = lambda ref: [ref[r, :] for r in rows]
    u_v, w_k, attn = _gdn_chunk_local(chunks(q_s), chunks(k_s), chunks(v_s), chunks(g_s), chunks(b_s),
                                      [beta[r, :] for r in rows], m)
    st = st_s[...]
    for c, r in enumerate(rows):
        o, st = _gdn_chunk_state(q_s[r, :], k_s[r, :], g_s[r, :], u_v[c], w_k[c], attn[c], st, m)
        o_s[r, :] = o
    st_s[...] = st
    o_ref[...] = _head_norm_gate(o_s[...], p_ref[:, 3 * w:4 * w], nw_ref[...], ones_blk)

    @pl.when(t == pl.num_programs(1) - 1)
    def _():
        _write_state(sout_ref, st_s[...])


def _gdn_prompt(p, params, bsz, seq, tt):
    nt = seq // tt
    w = BR_WIDTH
    return pl.pallas_call(
        functools.partial(_gdn_prompt_kernel, tt=tt),
        grid=(bsz, nt),
        in_specs=[pl.BlockSpec((tt, SEG_A), lambda b, t: (b * nt + t, 0))] + [_full(a.shape) for a in params],
        out_specs=[pl.BlockSpec((tt, w), lambda b, t: (b * nt + t, 0)),
                   _STATE_SPEC,
                   pl.BlockSpec((1, CONV_W - 1, 3 * w), lambda b, t: (b, 0, 0))],
        out_shape=[jax.ShapeDtypeStruct((bsz * seq, w), F32),
                   jax.ShapeDtypeStruct((bsz, N_HEADS, HEAD_DIM, HEAD_DIM), F32),
                   jax.ShapeDtypeStruct((bsz, CONV_W - 1, 3 * w), F32)],
        scratch_shapes=[pltpu.VMEM((w, w), F32), pltpu.VMEM((tt + 8, 3 * w), F32)] + [pltpu.VMEM((tt, w), F32)] * 6,
        compiler_params=_cparams("parallel", "arbitrary"),
        name="gdn_prompt",
    )(p, *params)


def _s5_kernel(u_ref, x0r_ref, x0i_ref, ar_ref, ai_ref, br_ref, bi_ref, cr_ref, ci_ref, d_ref, glu_ref,
               o_ref, xr_ref, xi_ref, sr_s, si_s, *, steps, rows):
    @pl.when(pl.program_id(0) == 0)
    def _():
        xr_ref[...] = x0r_ref[...]
        xi_ref[...] = x0i_ref[...]

    u = u_ref[...]
    ub = u.astype(BF16)
    sr_s[...] = _dot(ub, br_ref[...])
    si_s[...] = _dot(ub, bi_ref[...])
    a_re = ar_ref[...]
    a_im = ai_ref[...]

    def body(t, carry):
        xr, xi = carry
        r = pl.ds(pl.multiple_of(t * rows, rows), rows)
        nr = a_re * xr - a_im * xi + sr_s[r, :]
        ni = a_re * xi + a_im * xr + si_s[r, :]
        sr_s[r, :] = nr
        si_s[r, :] = ni
        return nr, ni

    xr, xi = lax.fori_loop(0, steps, body, (xr_ref[...], xi_ref[...]))
    xr_ref[...] = xr
    xi_ref[...] = xi
    y = _dot(sr_s[...].astype(BF16), cr_ref[...]) - _dot(si_s[...].astype(BF16), ci_ref[...]) + d_ref[...] * u
    hg = _dot(_gelu(y).astype(BF16), glu_ref[...])
    o_ref[...] = hg[:, 0:BR_WIDTH] * _sigmoid(hg[:, BR_WIDTH:2 * BR_WIDTH])


def _s5(u, x0r, x0i, params, steps_total, rows, steps):
    nt = steps_total // steps
    tile = steps * rows
    return pl.pallas_call(
        functools.partial(_s5_kernel, steps=steps, rows=rows),
        grid=(nt,),
        in_specs=[pl.BlockSpec((tile, BR_WIDTH), lambda t: (t, 0)), _full(x0r.shape), _full(x0i.shape)]
                 + [_full(a.shape) for a in params],
        out_specs=[pl.BlockSpec((tile, BR_WIDTH), lambda t: (t, 0)), _full(x0r.shape), _full(x0i.shape)],
        out_shape=[jax.ShapeDtypeStruct((steps_total * rows, BR_WIDTH), F32),
                   jax.ShapeDtypeStruct(x0r.shape, F32), jax.ShapeDtypeStruct(x0i.shape, F32)],
        scratch_shapes=[pltpu.VMEM((tile, S5_STATE), F32)] * 2,
        compiler_params=_cparams("arbitrary"),
        name="s5",
    )(u, x0r, x0i, *params)


def _s5_params(lam_re, lam_im, b_re, b_im, c_re, c_im, d, log_dt, glu_w):
    lr = jnp.minimum(lam_re, -1e-4)
    li = lam_im
    dt = jnp.exp(log_dt)[:, None]
    mag = jnp.exp(lr * dt)
    ab_re, ab_im = mag * jnp.cos(li * dt), mag * jnp.sin(li * dt)
    den = lr * lr + li * li
    z_re = ((ab_re - 1.0) * lr + ab_im * li) / den
    z_im = (ab_im * lr - (ab_re - 1.0) * li) / den
    bb_re = z_re[..., None] * b_re - z_im[..., None] * b_im
    bb_im = z_re[..., None] * b_im + z_im[..., None] * b_re
    eye = jnp.eye(S5_GROUPS, dtype=F32)

    def pack_in(bb):
        return jnp.einsum('gph,gk->ghkp', bb, eye).reshape(BR_WIDTH, S5_STATE).astype(BF16)

    def pack_out(c):
        return jnp.einsum('ghp,gk->gpkh', c, eye).reshape(S5_STATE, BR_WIDTH).astype(BF16)

    return (ab_re.reshape(1, S5_STATE), ab_im.reshape(1, S5_STATE), pack_in(bb_re), pack_in(bb_im),
            pack_out(c_re), pack_out(c_im), d.reshape(1, BR_WIDTH), glu_w.astype(BF16))


def _expand_mats():
    r = _iota2((HEAD_DIM, HEAD_DIM * HEAD_DIM), 0)
    c = _iota2((HEAD_DIM, HEAD_DIM * HEAD_DIM), 1)
    rep_k = jnp.where(r == (c >> 6), 1.0, 0.0).astype(BF16)
    rep_v = jnp.where(r == (c & (HEAD_DIM - 1)), 1.0, 0.0).astype(BF16)
    return rep_k, rep_v


def _gla_decode_kernel(q_ref, k_ref, v_ref, lf_ref, s_ref, o_ref, sn_ref):
    rep_k, rep_v = _expand_mats()
    q, k, v, lf = q_ref[0, 0], k_ref[0, 0], v_ref[0, 0], lf_ref[0, 0]
    s = s_ref[...]
    dec = jnp.exp(lf)
    qe_x = _dot_xl(q * dec, rep_k, 2)
    o = _dot_nt_xl(qe_x * s, rep_v, 2) + jnp.sum(q * k, axis=-1, keepdims=True) * v
    sn_ref[...] = _dot_xl(dec, rep_k, 3) * s + _dot_xl(k, rep_k, 2) * _dot_xl(v, rep_v, 2)
    o_ref[0] = o


def _gdn_decode_kernel(q_ref, k_ref, v_ref, g_ref, b_ref, s_ref, o_ref, sn_ref):
    rep_k, rep_v = _expand_mats()
    q, k, v = q_ref[0, 0], k_ref[0, 0], v_ref[0, 0]
    eg = jnp.exp(g_ref[0, 0])
    beta = b_ref[0, 0]
    s = s_ref[...]
    k_x = _dot_xl(k, rep_k, 2)
    ks = _dot_nt_xl(k_x * s, rep_v, 2)
    qs = _dot_nt_xl(_dot_xl(q, rep_k, 2) * s, rep_v, 2)
    u = beta * v - (beta * eg) * ks
    o_ref[0] = eg * qs + jnp.sum(q * k, axis=-1, keepdims=True) * u
    sn_ref[...] = eg * s + k_x * _dot_xl(u, rep_v, 2)


def _decode_call(kernel, packs, state, name):
    rows = state.shape[0]
    hw = HEAD_DIM * HEAD_DIM
    vec_specs, vecs = [], []
    for a in packs:
        for j in range(a.shape[0]):
            vec_specs.append(pl.BlockSpec((1, 1, rows, a.shape[3]), lambda h, j=j: (j, h, 0, 0)))
            vecs.append(a)
    return pl.pallas_call(
        kernel,
        grid=(N_HEADS,),
        in_specs=vec_specs + [pl.BlockSpec((rows, hw), lambda h: (0, h))],
        out_specs=[pl.BlockSpec((1, rows, HEAD_DIM), lambda h: (h, 0, 0)), pl.BlockSpec((rows, hw), lambda h: (0, h))],
        out_shape=[jax.ShapeDtypeStruct((N_HEADS, rows, HEAD_DIM), F32), jax.ShapeDtypeStruct(state.shape, F32)],
        compiler_params=_cparams("parallel"),
        name=name,
    )(*vecs, state)


def _put_heads(ref, j, x):
    for h in range(N_HEADS):
        ref[j, h] = x[:, h * HEAD_DIM:(h + 1) * HEAD_DIM]


def _decode_prep_kernel(pa_ref, pb_ref, pd_ref, conv_ref, cw_ref, alog_ref, dtb_ref, gkw_ref, gkb_ref, lb_ref,
                        a_ref, ag_ref, b_ref, d_ref, convn_ref):
    w = BR_WIDTH
    cw = 3 * w
    ones_blk = _block_ones()
    raw = pa_ref[:, 0:cw]
    y = raw * cw_ref[CONV_W - 1:CONV_W, :]
    for j in range(CONV_W - 1):
        y = y + conv_ref[:, j * cw:(j + 1) * cw] * cw_ref[j:j + 1, :]
    convn_ref[:, 0:(CONV_W - 2) * cw] = conv_ref[:, cw:(CONV_W - 1) * cw]
    convn_ref[:, (CONV_W - 2) * cw:(CONV_W - 1) * cw] = raw
    qkv = _silu(y)
    _put_heads(a_ref, 0, _l2norm_heads(qkv[:, 0:w], ones_blk) * HEAD_DIM ** -0.5)
    _put_heads(a_ref, 1, _l2norm_heads(qkv[:, w:2 * w], ones_blk))
    _put_heads(a_ref, 2, qkv[:, 2 * w:3 * w])
    g, beta = _gdn_prep(pa_ref[:, 4 * w:4 * w + LANE], alog_ref, dtb_ref)
    for h in range(N_HEADS):
        ag_ref[0, h] = g[:, h * HEAD_DIM:h * HEAD_DIM + 1]
        ag_ref[1, h] = beta[:, h * HEAD_DIM:h * HEAD_DIM + 1]
    _put_heads(b_ref, 0, pb_ref[:, 0:w] * HEAD_DIM ** -0.5)
    _put_heads(b_ref, 1, pb_ref[:, w:2 * w])
    _put_heads(b_ref, 2, pb_ref[:, 2 * w:3 * w])
    _put_heads(b_ref, 3, _log_sigmoid(_dot(pb_ref[:, 4 * w:4 * w + LANE], gkw_ref[...]) + gkb_ref[...])
               * (1.0 / GLA_GATE_NORM))
    lb = lb_ref[...]
    xf = pd_ref[:, w:2 * w]
    _put_heads(d_ref, 0, _silu(pd_ref[:, 0:w]) * HEAD_DIM ** -0.5)
    _put_heads(d_ref, 1, (1.0 - lb) * _sigmoid(-xf))
    _put_heads(d_ref, 2, pd_ref[:, 2 * w:3 * w])
    _put_heads(d_ref, 3, _hgrn_log_forget(xf, lb))


def _decode_prep(pa, pb, pd, conv, params):
    rows = pa.shape[0]
    ins = (pa, pb, pd, conv) + tuple(params)
    per_head = lambda n, width: (n, N_HEADS, rows, width)
    shapes = [per_head(3, HEAD_DIM), per_head(2, 1), per_head(4, HEAD_DIM), per_head(4, HEAD_DIM), conv.shape]
    return pl.pallas_call(
        _decode_prep_kernel,
        grid=(1,),
        in_specs=[_full(a.shape) for a in ins],
        out_specs=[_full(s) for s in shapes],
        out_shape=[jax.ShapeDtypeStruct(s, F32) for s in shapes],
        compiler_params=_cparams("arbitrary"),
        name="decode_prep",
    )(*ins)


def _decode_post_kernel(oa_ref, ob_ref, od_ref, pa_ref, pb_ref, pd_ref, nwa_ref, nwb_ref, nwd_ref,
                        a_ref, b_ref, d_ref):
    w = BR_WIDTH
    ones_blk = _block_ones()
    heads = lambda ref: jnp.concatenate([ref[h] for h in range(N_HEADS)], axis=1)
    a_ref[...] = _head_norm_gate(heads(oa_ref), pa_ref[:, 3 * w:4 * w], nwa_ref[...], ones_blk)
    b_ref[...] = _head_norm_gate(heads(ob_ref), pb_ref[:, 3 * w:4 * w], nwb_ref[...], ones_blk)
    d_ref[...] = _head_norm_gate(heads(od_ref), pd_ref[:, 3 * w:4 * w], nwd_ref[...], ones_blk)


def _decode_post(oa, ob, od, pa, pb, pd, nwa, nwb, nwd):
    ins = (oa, ob, od, pa, pb, pd, nwa, nwb, nwd)
    shp = (oa.shape[1], BR_WIDTH)
    return pl.pallas_call(
        _decode_post_kernel,
        grid=(1,),
        in_specs=[_full(a.shape) for a in ins],
        out_specs=[_full(shp)] * 3,
        out_shape=[jax.ShapeDtypeStruct(shp, F32)] * 3,
        compiler_params=_cparams("arbitrary"),
        name="decode_post",
    )(*ins)


def _pack_w_in(w_in):
    sizes = (3 * BR_WIDTH, N_HEADS, N_HEADS, BR_WIDTH, BR_WIDTH, BR_WIDTH, BR_WIDTH, GLA_GATE_RANK, BR_WIDTH,
             BR_WIDTH, BR_WIDTH, BR_WIDTH, BR_WIDTH, BR_WIDTH, N_BRANCH * D_MODEL)
    w_in = w_in.astype(BF16)
    parts, off = [], 0
    for n in sizes:
        parts.append(w_in[..., off:off + n])
        off += n
    (a_qkv, a_alpha, a_beta, a_gate, b_q, b_k, b_v, b_gk, b_gate, c_u, d_q, d_f, d_i, d_gate, merge) = parts
    zpad = lambda n: jnp.zeros(w_in.shape[:-1] + (n,), BF16)
    mix = jnp.concatenate([a_qkv, a_gate, a_alpha, a_beta, zpad(LANE - 2 * N_HEADS),
                           b_q, b_k, b_v, b_gate, b_gk, zpad(LANE - GLA_GATE_RANK),
                           c_u, d_q, d_f, d_i, d_gate], axis=-1)
    return mix, merge


def _tile_sizes(seq):
    pick = lambda n: n if seq % n == 0 else seq
    return pick(512), pick(256), pick(128)


def _hgrn_lower_bounds(logits):
    p = jax.nn.softmax(logits, axis=0)
    return jnp.cumsum(p, axis=0) - p[0]


def _row(x):
    return x.reshape(1, -1)


def _rep_heads(x):
    return jnp.repeat(x, HEAD_DIM).reshape(1, BR_WIDTH)


def kernel(x_prompt, x_sample, state_gdn_conv, state_gdn, state_gla, state_s5_re, state_s5_im, state_hgrn, norm1_w, w_in, gdn_conv_w, gdn_a_log, gdn_dt_bias, gdn_norm_w, gla_gk_w, gla_gk_b, gla_norm_w, s5_lambda_re, s5_lambda_im, s5_b_re, s5_b_im, s5_c_re, s5_c_im, s5_d, s5_log_dt, s5_glu_w, hgrn_lb_logits, hgrn_norm_w, w_branch, w_out, norm2_w, ffn_w_gate, ffn_w_up, ffn_w_down, final_norm_w):
    bsz, seq, _ = x_prompt.shape
    dbs = x_sample.shape[0]
    tm_p, tt, s5_steps = _tile_sizes(seq)
    lb_all = _hgrn_lower_bounds(hgrn_lb_logits)
    w_mix_all, w_merge_all = _pack_w_in(w_in)
    gkw_all = jnp.concatenate([gla_gk_w, jnp.zeros((DEPTH, LANE - GLA_GATE_RANK, BR_WIDTH), F32)], axis=1)
    wb_all, wo_all = w_branch.astype(BF16), w_out.astype(BF16)
    wg_all, wu_all, wd_all = ffn_w_gate.astype(BF16), ffn_w_up.astype(BF16), ffn_w_down.astype(BF16)
    hp = x_prompt.reshape(bsz * seq, D_MODEL)
    hs = x_sample.reshape(dbs, D_MODEL)
    zeros_p = jnp.zeros((bsz, S5_STATE), F32)
    fw = _row(final_norm_w)
    outs_p = [[] for _ in range(6)]
    outs_s = [[] for _ in range(6)]
    for i in range(DEPTH):
        final = i == DEPTH - 1
        w_mix, w_merge, gkw = w_mix_all[i], w_merge_all[i], gkw_all[i]
        nw1, nw2 = _row(norm1_w[i]), _row(norm2_w[i])
        gdn_params = (gdn_conv_w[i], _rep_heads(gdn_a_log[i]), _rep_heads(gdn_dt_bias[i]), _row(gdn_norm_w[i]))
        gla_params = (gkw, _row(gla_gk_b[i]), _row(gla_norm_w[i]))
        hgrn_params = (_row(lb_all[i]), _row(hgrn_norm_w[i]))
        s5_params = _s5_params(s5_lambda_re[i], s5_lambda_im[i], s5_b_re[i], s5_b_im[i], s5_c_re[i], s5_c_im[i],
                               s5_d[i], s5_log_dt[i], s5_glu_w[i])
        wb, wo, wg, wu, wd = wb_all[i], wo_all[i], wg_all[i], wu_all[i], wd_all[i]

        pa, pb, pc, pd = _in_proj(hp, nw1, w_mix, tm_p, seq)
        o_a, st_a, conv_p = _gdn_prompt(pa, gdn_params, bsz, seq, tt)
        o_b, st_b = _linear_prompt(_gla_prompt_kernel, pb, gla_params, bsz, seq, tt)
        o_d, st_d = _linear_prompt(_hgrn_prompt_kernel, pd, hgrn_params, bsz, seq, tt)
        o_c, xr_p, xi_p = _s5(pc.reshape(seq * bsz, BR_WIDTH), zeros_p, zeros_p, s5_params, seq, bsz, s5_steps)
        hp = _merge(hp, o_a, o_b, o_c.reshape(seq, bsz * BR_WIDTH), o_d, nw1, w_merge, wb, wo, tm_p, seq)
        hp = _ffn(hp, nw2, wg, wu, wd, fw, tm_p, final)
        for lst, s in zip(outs_p, (conv_p, st_a, st_b, xr_p.reshape(bsz, S5_GROUPS, S5_P),
                                   xi_p.reshape(bsz, S5_GROUPS, S5_P), st_d)):
            lst.append(s)

        pa, pb, pc, pd = _in_proj(hs, nw1, w_mix, dbs, dbs)
        va, vg, vb, vd, conv_s = _decode_prep(
            pa, pb, pd, state_gdn_conv[i].reshape(dbs, (CONV_W - 1) * 3 * BR_WIDTH),
            (gdn_conv_w[i], gdn_params[1], gdn_params[2], gkw, gla_params[1], hgrn_params[0]))
        flat = lambda s: s.reshape(dbs, N_HEADS * HEAD_DIM * HEAD_DIM)
        oa_h, sa = _decode_call(_gdn_decode_kernel, [va, vg], flat(state_gdn[i]), "gdn_decode")
        ob_h, sb = _decode_call(_gla_decode_kernel, [vb], flat(state_gla[i]), "gla_decode")
        od_h, sd = _decode_call(_gla_decode_kernel, [vd], flat(state_hgrn[i]), "hgrn_decode")
        o_a, o_b, o_d = _decode_post(oa_h, ob_h, od_h, pa, pb, pd, gdn_params[3], gla_params[2], hgrn_params[1])
        o_c, xr_s, xi_s = _s5(pc, state_s5_re[i].reshape(dbs, S5_STATE), state_s5_im[i].reshape(dbs, S5_STATE),
                              s5_params, 1, dbs, 1)
        hs = _merge(hs, o_a, o_b, o_c, o_d, nw1, w_merge, wb, wo, dbs, dbs)
        hs = _ffn(hs, nw2, wg, wu, wd, fw, dbs, final)
        st5 = lambda s: s.reshape(dbs, N_HEADS, HEAD_DIM, HEAD_DIM)
        for lst, s in zip(outs_s, (conv_s.reshape(dbs, CONV_W - 1, 3 * BR_WIDTH), st5(sa), st5(sb),
                                   xr_s.reshape(dbs, S5_GROUPS, S5_P), xi_s.reshape(dbs, S5_GROUPS, S5_P), st5(sd))):
            lst.append(s)

    return (hp.reshape(bsz, seq, D_MODEL), hs.reshape(dbs, 1, D_MODEL),
            *[jnp.stack(l) for l in outs_p], *[jnp.stack(l) for l in outs_s])
```

```python
import functools
import math

import jax
import jax.numpy as jnp
from jax import lax
from jax.experimental import pallas as pl
from jax.experimental.pallas import tpu as pltpu

F32 = jnp.float32
BF16 = jnp.bfloat16

D_MODEL = 1024
DEPTH = 4
N_BRANCH = 4
BR_WIDTH = D_MODEL // N_BRANCH
HEAD_DIM = 64
N_HEADS = BR_WIDTH // HEAD_DIM
CONV_W = 4
CHUNK = 64
SUB = 16
GLA_GATE_RANK = 16
GLA_GATE_NORM = 16.0
S5_GROUP = 16
S5_GROUPS = BR_WIDTH // S5_GROUP
S5_P = 64
S5_STATE = S5_GROUPS * S5_P
D_FF = -(-8 * D_MODEL // (3 * 256)) * 256
FF_TILE = 256
EPS = 1e-6
EXP_CAP = 60.0
LANE = 128
SEG_A = 3 * BR_WIDTH + BR_WIDTH + LANE
SEG_B = 4 * BR_WIDTH + LANE
SEG_C = BR_WIDTH
SEG_D = 4 * BR_WIDTH
SEG_OFF = (0, SEG_A, SEG_A + SEG_B, SEG_A + SEG_B + SEG_C, SEG_A + SEG_B + SEG_C + SEG_D)
N_MIX = SEG_OFF[-1]
PREP_A = 6 * BR_WIDTH
PREP_B = 5 * BR_WIDTH
VMEM_LIMIT = 56 * 1024 * 1024


def _cparams(*sem):
    return pltpu.CompilerParams(dimension_semantics=sem, vmem_limit_bytes=VMEM_LIMIT)


def _full(shape):
    n = len(shape)
    return pl.BlockSpec(shape, lambda *_: (0,) * n)


def _rms(x, w):
    return x * lax.rsqrt(jnp.mean(x * x, axis=-1, keepdims=True) + EPS) * w


def _sigmoid(x):
    return 1.0 / (1.0 + jnp.exp(-x))


def _silu(x):
    return x * _sigmoid(x)


def _softplus(x):
    return jnp.maximum(x, 0.0) + jnp.log(1.0 + jnp.exp(-jnp.abs(x)))


def _log_sigmoid(x):
    return -_softplus(-x)


def _gelu(x):
    return 0.5 * x * (1.0 + jnp.tanh(math.sqrt(2.0 / math.pi) * (x + 0.044715 * x * x * x)))


def _dot(a, b):
    return jnp.dot(a.astype(BF16), b.astype(BF16), preferred_element_type=F32)


def _dot_nt(a, b):
    return lax.dot_general(a.astype(BF16), b.astype(BF16), (((1,), (1,)), ((), ())), preferred_element_type=F32)


def _dot_tn(a, b):
    return lax.dot_general(a.astype(BF16), b.astype(BF16), (((0,), (0,)), ((), ())), preferred_element_type=F32)


def _split(x, terms):
    parts, rest = [], x
    for t in range(terms):
        p = rest.astype(BF16)
        parts.append(p)
        if t + 1 < terms:
            rest = rest - p.astype(F32)
    return parts


def _dot_xl(x, c, terms=3):
    return sum(jnp.dot(p, c, preferred_element_type=F32) for p in _split(x, terms))


def _dot_xr(c, x, terms=3):
    return sum(jnp.dot(c, p, preferred_element_type=F32) for p in _split(x, terms))


def _dot_nt_xl(x, c, terms=3):
    return sum(lax.dot_general(p, c, (((1,), (1,)), ((), ())), preferred_element_type=F32) for p in _split(x, terms))


def _iota2(shape, dim):
    return lax.broadcasted_iota(jnp.int32, shape, dim)


def _head_masks():
    r = _iota2((CHUNK, N_HEADS * CHUNK), 0)
    c = _iota2((CHUNK, N_HEADS * CHUNK), 1) & (CHUNK - 1)
    rr = _iota2((N_HEADS * CHUNK, N_HEADS * HEAD_DIM), 0)
    cc = _iota2((N_HEADS * CHUNK, N_HEADS * HEAD_DIM), 1)
    block = (rr >> 6) == (cc >> 6)
    one = lambda mask: jnp.where(mask, 1.0, 0.0)
    return dict(causal=r >= c, strict=r > c, eye=r == c, same_sub=(r >> 4) == (c >> 4), block=block,
                block16=one(block).astype(BF16), sub16=one((rr >> 4) == (cc >> 4)).astype(BF16),
                eye_sub=one(_iota2((SUB, BR_WIDTH), 0) == (_iota2((SUB, BR_WIDTH), 1) & (SUB - 1))))


def _chunk_tri(tt):
    r = _iota2((tt, tt), 0)
    c = _iota2((tt, tt), 1)
    return jnp.where((r >= c) & ((r >> 6) == (c >> 6)), 1.0, 0.0).astype(BF16)


def _chunk_ones(tt):
    r = _iota2((tt, tt), 0) >> 6
    c = _iota2((tt, tt), 1) >> 6
    return jnp.where(r == c, 1.0, 0.0).astype(BF16)


def _block_ones():
    return _chunk_ones(BR_WIDTH)


def _blockdiag(x, block16):
    return jnp.concatenate([x.astype(BF16)] * N_HEADS, axis=0) * block16


def _head_norm_gate(o, silu_gate, w, ones_blk):
    ms = _dot_xl(o * o, ones_blk, 2) * (1.0 / HEAD_DIM)
    return o * lax.rsqrt(ms + EPS) * w * silu_gate


def _in_proj_kernel(x_ref, nw_ref, w_ref, pa_ref, pb_ref, pc_ref, pd_ref):
    xn = _rms(x_ref[...], nw_ref[...]).astype(BF16)
    for i, ref in enumerate((pa_ref, pb_ref, pc_ref, pd_ref)):
        ref[...] = _dot(xn, w_ref[:, SEG_OFF[i]:SEG_OFF[i + 1]])


def _time_major_spec(tm, seq):
    nt = seq // tm
    return pl.BlockSpec((tm, BR_WIDTH), lambda i: (i % nt, i // nt))


def _in_proj_prompt_kernel(x_ref, halo_ref, nw_ref, w_ref, cw_ref, alog_ref, dtb_ref, gkw_ref, gkb_ref, lb_ref,
                           ga_ref, gb_ref, pc_ref, gd_ref, conv_ref, *, nt):
    w = BR_WIDTH
    seg = lambda i: w_ref[:, SEG_OFF[i]:SEG_OFF[i + 1]]
    xn = _rms(x_ref[...], nw_ref[...]).astype(BF16)
    pd = _dot(xn, seg(3))
    pa = _dot(xn, seg(0))
    hist = _dot(_rms(halo_ref[...], nw_ref[...]), w_ref[:, 0:3 * w])
    for j, val in enumerate(_hgrn_prep(pd, lb_ref)):
        gd_ref[:, j * w:(j + 1) * w] = val
    pb = _dot(xn, seg(1))
    raw = pa[:, 0:3 * w]
    hist = jnp.where((pl.program_id(0) % nt) == 0, 0.0, hist)
    rows = raw.shape[0]
    conv_ref[0] = raw[rows - (CONV_W - 1):rows, :]
    q, k, v = _gdn_conv_qkv(raw, hist, cw_ref)
    g, beta = _gdn_prep(pa[:, 4 * w:4 * w + LANE], alog_ref, dtb_ref)
    for j, val in enumerate((q, k, v, _silu(pa[:, 3 * w:4 * w]), g, beta)):
        ga_ref[:, j * w:(j + 1) * w] = val
    pc_ref[...] = _dot(xn, seg(2))
    for j, val in enumerate(_gla_prep(pb, gkw_ref, gkb_ref)):
        gb_ref[:, j * w:(j + 1) * w] = val


def _in_proj_prompt(x, nw, w, params, tm, seq):
    t = x.shape[0]
    nt = seq // tm
    bsz = t // seq
    row = lambda n: pl.BlockSpec((tm, n), lambda i: (i, 0))
    halo = pl.BlockSpec((8, D_MODEL), lambda i: (jnp.maximum(i * (tm // 8) - 1, 0), 0))
    return pl.pallas_call(
        functools.partial(_in_proj_prompt_kernel, nt=nt),
        grid=(t // tm,),
        in_specs=[row(D_MODEL), halo, _full((1, D_MODEL)), _full((D_MODEL, N_MIX))] + [_full(a.shape) for a in params],
        out_specs=[row(PREP_A), row(PREP_B), _time_major_spec(tm, seq), row(PREP_B),
                   pl.BlockSpec((1, CONV_W - 1, 3 * BR_WIDTH), lambda i: (i // nt, 0, 0))],
        out_shape=[jax.ShapeDtypeStruct((t, PREP_A), F32), jax.ShapeDtypeStruct((t, PREP_B), F32),
                   jax.ShapeDtypeStruct((seq, bsz * BR_WIDTH), F32), jax.ShapeDtypeStruct((t, PREP_B), F32),
                   jax.ShapeDtypeStruct((bsz, CONV_W - 1, 3 * BR_WIDTH), F32)],
        compiler_params=_cparams("arbitrary"),
        name="in_proj_prompt",
    )(x, x, nw, w, *params)


def _in_proj(x, nw, w, tm, seq):
    t = x.shape[0]
    widths = (SEG_A, SEG_B, SEG_C, SEG_D)
    out_specs = [pl.BlockSpec((tm, n), lambda i: (i, 0)) for n in widths]
    out_shape = [jax.ShapeDtypeStruct((t, n), F32) for n in widths]
    out_specs[2] = _time_major_spec(tm, seq)
    out_shape[2] = jax.ShapeDtypeStruct((seq, (t // seq) * BR_WIDTH), F32)
    return pl.pallas_call(
        _in_proj_kernel,
        grid=(t // tm,),
        in_specs=[pl.BlockSpec((tm, D_MODEL), lambda i: (i, 0)), _full((1, D_MODEL)), _full((D_MODEL, N_MIX))],
        out_specs=out_specs,
        out_shape=out_shape,
        compiler_params=_cparams("parallel"),
        name="in_proj",
    )(x, nw, w)


def _merge_kernel(h_ref, oa_ref, ob_ref, oc_ref, od_ref, *rest, head_norm):
    if head_norm:
        gates, gains, rest = rest[0:3], rest[3:6], rest[6:]
    nw_ref, wm_ref, wb_ref, wo_ref, out_ref = rest
    h = h_ref[...]
    xn = _rms(h, nw_ref[...]).astype(BF16)
    o_vals = [oa_ref[...], ob_ref[...], oc_ref[...], od_ref[...]]
    if head_norm:
        ones_blk = _block_ones()
        for k, g_ref, w_ref in zip((0, 1, 3), gates, gains):
            o_vals[k] = _head_norm_gate(o_vals[k], g_ref[...], w_ref[...], ones_blk)
    proj = lambda k: (_dot(xn, wm_ref[:, k * D_MODEL:(k + 1) * D_MODEL]), _dot(o_vals[k], wb_ref[k]))
    acc = jnp.zeros(h.shape, F32)
    nxt = proj(0)
    for k in range(N_BRANCH):
        gate, br = nxt
        if k + 1 < N_BRANCH:
            nxt = proj(k + 1)
        acc = acc + _sigmoid(gate) * br
    out_ref[...] = h + _dot(acc, wo_ref[...])


def _merge(h, oa, ob, oc, od, nw, wm, wb, wo, tm, seq, prepared=None, gains=None):
    t = h.shape[0]
    row = lambda n: pl.BlockSpec((tm, n), lambda i: (i, 0))
    head_norm = prepared is not None
    extra, extra_specs = (), []
    if head_norm:
        extra = tuple(prepared) + tuple(gains)
        extra_specs = [pl.BlockSpec((tm, BR_WIDTH), lambda i: (i, 3))] * 3 + [_full((1, BR_WIDTH))] * 3
    return pl.pallas_call(
        functools.partial(_merge_kernel, head_norm=head_norm),
        grid=(t // tm,),
        in_specs=[row(D_MODEL), row(BR_WIDTH), row(BR_WIDTH), _time_major_spec(tm, seq), row(BR_WIDTH)]
                 + extra_specs + [_full((1, D_MODEL)), _full((D_MODEL, N_BRANCH * D_MODEL)),
                                  _full((N_BRANCH, BR_WIDTH, D_MODEL)), _full((D_MODEL, D_MODEL))],
        out_specs=row(D_MODEL),
        out_shape=jax.ShapeDtypeStruct((t, D_MODEL), F32),
        compiler_params=_cparams("parallel"),
        name="merge",
    )(h, oa, ob, oc, od, *extra, nw, wm, wb, wo)


def _ffn_kernel(h_ref, nw_ref, wg_ref, wu_ref, wd_ref, fw_ref, out_ref, *, final):
    h = h_ref[...]
    hn = _rms(h, nw_ref[...]).astype(BF16)
    n_tiles = D_FF // FF_TILE
    cols = [slice(c * FF_TILE, (c + 1) * FF_TILE) for c in range(n_tiles)]
    up = lambda sl: (_dot(hn, wg_ref[:, sl]), _dot(hn, wu_ref[:, sl]))
    acc = jnp.zeros(h.shape, F32)
    nxt = up(cols[0])
    for c in range(n_tiles):
        g, u = nxt
        if c + 1 < n_tiles:
            nxt = up(cols[c + 1])
        acc = acc + _dot(_silu(g) * u, wd_ref[cols[c], :])
    hnew = h + acc
    out_ref[...] = _rms(hnew, fw_ref[...]) if final else hnew


def _ffn(h, nw, wg, wu, wd, fw, tm, final):
    t = h.shape[0]
    row = pl.BlockSpec((tm, D_MODEL), lambda i: (i, 0))
    return pl.pallas_call(
        functools.partial(_ffn_kernel, final=final),
        grid=(t // tm,),
        in_specs=[row, _full((1, D_MODEL)), _full((D_MODEL, D_FF)), _full((D_MODEL, D_FF)),
                  _full((D_FF, D_MODEL)), _full((1, D_MODEL))],
        out_specs=row,
        out_shape=jax.ShapeDtypeStruct((t, D_MODEL), F32),
        compiler_params=_cparams("parallel"),
        name="ffn",
    )(h, nw, wg, wu, wd, fw)


def _gla_chunk(q, k, v, b, st, m):
    each = lambda f, *xs: [f(*a) for a in zip(*xs)]
    blk = m["block16"]
    b_last = each(lambda x: x[CHUNK - 1:CHUNK, :], b)
    rows = []
    for i0 in range(0, CHUNK, SUB):
        q_i = each(lambda qq, x: qq[i0:i0 + SUB, :] * jnp.exp(x[i0:i0 + SUB, :] - x[i0:i0 + 1, :]), q, b)
        k_i = each(lambda kk, x: kk * jnp.exp(jnp.minimum(x[i0:i0 + 1, :] - x, EXP_CAP)), k, b)
        rows.append(each(lambda a, c: _dot_nt(a, _blockdiag(c, blk)), q_i, k_i))
    attn = each(lambda *r: jnp.where(m["causal"], jnp.concatenate(r, axis=0), 0.0), *rows)
    o_loc = each(lambda a, vv: _dot(a, _blockdiag(vv, blk)), attn, v)
    kv = each(lambda vv, kk, x, xl: jnp.where(m["block"], _dot_tn(vv, kk * jnp.exp(xl - x)), 0.0), v, k, b, b_last)
    qe = each(lambda qq, x: qq * jnp.exp(x), q, b)
    outs = []
    for c in range(len(q)):
        outs.append(_dot_nt(qe[c], st) + o_loc[c])
        st = st * jnp.exp(b_last[c]) + kv[c]
    return outs, st


def _write_state(out_ref, st):
    r = _iota2((BR_WIDTH, BR_WIDTH), 0)
    c = _iota2((BR_WIDTH, BR_WIDTH), 1)
    eye = jnp.where(r == c, 1.0, 0.0).astype(BF16)
    nt_dot = lambda p: lax.dot_general(eye, p, (((1,), (1,)), ((), ())), preferred_element_type=F32)
    st_t = sum(nt_dot(p) for p in _split(st, 3))
    for h in range(N_HEADS):
        out_ref[0, h] = st_t[h * HEAD_DIM:(h + 1) * HEAD_DIM, h * HEAD_DIM:(h + 1) * HEAD_DIM]


def _gla_prep(pb, gkw_ref, gkb_ref):
    w = BR_WIDTH
    lf = _log_sigmoid(_dot(pb[:, 4 * w:4 * w + LANE], gkw_ref[...]) + gkb_ref[...]) * (1.0 / GLA_GATE_NORM)
    return pb[:, 0:w] * HEAD_DIM ** -0.5, pb[:, w:2 * w], pb[:, 2 * w:3 * w], _silu(pb[:, 3 * w:4 * w]), lf


def _hgrn_prep(pd, lb_ref):
    w = BR_WIDTH
    lb = lb_ref[...]
    xf = pd[:, w:2 * w]
    return (_silu(pd[:, 0:w]) * HEAD_DIM ** -0.5, (1.0 - lb) * _sigmoid(-xf), pd[:, 2 * w:3 * w],
            _silu(pd[:, 3 * w:4 * w]), _hgrn_log_forget(xf, lb))


def _linear_prompt_kernel(p_ref, o_ref, sout_ref, st_s, *, tt):
    @pl.when(pl.program_id(1) == 0)
    def _():
        st_s[...] = jnp.zeros(st_s.shape, F32)

    w = BR_WIDTH
    m = _head_masks()
    b_all = _dot_xr(_chunk_tri(tt), p_ref[:, 4 * w:5 * w], 3)
    rows = [slice(c * CHUNK, (c + 1) * CHUNK) for c in range(tt // CHUNK)]
    part = lambda j: [p_ref[r, j * w:(j + 1) * w] for r in rows]
    outs, st = _gla_chunk(part(0), part(1), part(2), [b_all[r, :] for r in rows], st_s[...], m)
    st_s[...] = st
    o_ref[...] = jnp.concatenate(outs, axis=0)

    @pl.when(pl.program_id(1) == pl.num_programs(1) - 1)
    def _():
        _write_state(sout_ref, st_s[...])


def _hgrn_log_forget(xf, lb):
    ls_pos = _log_sigmoid(xf)
    pos = lb > 0
    lb_safe = jnp.where(pos, lb, 1.0)
    mixed = ls_pos + _softplus(jnp.log(lb_safe) - xf)
    return jnp.where(pos, mixed, ls_pos)


def _linear_prompt(p, bsz, seq, tt, name):
    nt = seq // tt
    return pl.pallas_call(
        functools.partial(_linear_prompt_kernel, tt=tt),
        grid=(bsz, nt),
        in_specs=[pl.BlockSpec((tt, PREP_B), lambda b, t: (b * nt + t, 0))],
        out_specs=[pl.BlockSpec((tt, BR_WIDTH), lambda b, t: (b * nt + t, 0)), _STATE_SPEC],
        out_shape=[jax.ShapeDtypeStruct((bsz * seq, BR_WIDTH), F32),
                   jax.ShapeDtypeStruct((bsz, N_HEADS, HEAD_DIM, HEAD_DIM), F32)],
        scratch_shapes=[pltpu.VMEM((BR_WIDTH, BR_WIDTH), F32)],
        compiler_params=_cparams("parallel", "arbitrary"),
        name=name,
    )(p)


_STATE_SPEC = pl.BlockSpec((1, N_HEADS, HEAD_DIM, HEAD_DIM), lambda b, t: (b, 0, 0, 0))


def _mm_split(a, b, tile_mask):
    a_hi, a_lo = _split(a, 2)
    b_hi, b_lo = (jnp.concatenate([p] * (BR_WIDTH // b.shape[0]), axis=0) * tile_mask for p in _split(b, 2))
    dot = functools.partial(jnp.dot, preferred_element_type=F32)
    return dot(a_hi, b_hi) + (dot(a_lo, b_hi) + dot(a_hi, b_lo))


def _unit_lower_inverse(n, m):
    each = lambda f, *xs: [f(*a) for a in zip(*xs)]
    eye_c, sub, blk = m["eye_sub"], m["sub16"], m["block16"]
    nd = each(lambda a: jnp.where(m["same_sub"], a, 0.0), n)
    low = each(lambda a, d: a - d, n, nd)
    c = each(lambda d: d[0:SUB] + d[SUB:2 * SUB] + d[2 * SUB:3 * SUB] + d[3 * SUB:4 * SUB], nd)
    p = each(lambda a: _mm_split(a, a, sub), c)
    x = each(lambda a: eye_c - a, c)
    for _ in range(2):
        xp = each(lambda a, b: _mm_split(jnp.concatenate([a, b], axis=0), b, sub), x, p)
        x = each(lambda a, b: a + b[0:SUB], x, xp)
        p = each(lambda b: b[SUB:2 * SUB], xp)
    x = each(lambda a, b: a + _mm_split(a, b, sub), x, p)
    dinv = each(lambda a: jnp.where(m["same_sub"], jnp.concatenate([a] * (CHUNK // SUB), axis=0), 0.0), x)
    mm = each(lambda a, b: _mm_split(a, b, blk), dinv, low)
    m2 = each(lambda a: _mm_split(a, a, blk), mm)
    e = each(lambda d, a: d + _mm_split(a, d, blk), dinv, m2)
    return each(lambda a, b: b - _mm_split(a, b, blk), mm, e)


def _gdn_chunk_local(q, k, v, gam_i, gam_j, bexp, m):
    each = lambda f, *xs: [f(*a) for a in zip(*xs)]
    blk = m["block16"]
    decay = each(lambda gi, gj: jnp.where(m["causal"], jnp.exp(jnp.minimum(gi - gj, 0.0)), 0.0), gam_i, gam_j)
    scores = each(lambda kk, qq: _dot_nt(jnp.concatenate([kk, qq], axis=0), _blockdiag(kk, blk)), k, q)
    attn = each(lambda s, d: s[CHUNK:2 * CHUNK] * d, scores, decay)
    n = each(lambda b, s, d: jnp.where(m["strict"], b * s[0:CHUNK] * d, 0.0), bexp, scores, decay)
    tinv = _unit_lower_inverse(n, m)
    u_v = each(lambda t, b, vv: _dot(t, _blockdiag(b * vv, blk)), tinv, bexp, v)
    w_k = each(lambda t, b, gi, kk: _dot(t, _blockdiag(b * jnp.exp(gi) * kk, blk)), tinv, bexp, gam_i, k)
    return u_v, w_k, attn


def _gdn_chunk_state(q, k, gam_i, u_v, w_k, attn, st, m):
    gam_last = gam_i[CHUNK - 1:CHUNK, :]
    ws = _dot_nt(jnp.concatenate([w_k, q], axis=0), st)
    u = u_v - ws[0:CHUNK]
    o = jnp.exp(gam_i) * ws[CHUNK:2 * CHUNK] + _dot(attn, _blockdiag(u, m["block16"]))
    kd = k * jnp.exp(gam_last - gam_i)
    st_new = st * jnp.exp(gam_last) + jnp.where(m["block"], _dot_tn(u, kd), 0.0)
    return o, st_new


def _l2norm_heads(x, ones_blk):
    return x * lax.rsqrt(_dot_xl(x * x, ones_blk, 2) + EPS)


def _gdn_prep(a_alpha_beta, alog_ref, dtb_ref):
    r = _iota2((LANE, BR_WIDTH), 0)
    c = _iota2((LANE, BR_WIDTH), 1) >> 6
    sel_a = jnp.where(r == c, 1.0, 0.0).astype(BF16)
    sel_b = jnp.where(r == c + N_HEADS, 1.0, 0.0).astype(BF16)
    alpha = _dot_xl(a_alpha_beta, sel_a, 3)
    beta = _sigmoid(_dot_xl(a_alpha_beta, sel_b, 3))
    g = -jnp.exp(alog_ref[...]) * _softplus(alpha + dtb_ref[...])
    return g, beta


def _gdn_conv_qkv(raw, hist, cw_ref):
    w = BR_WIDTH
    rows = raw.shape[0]
    xp = jnp.concatenate([hist, raw], axis=0)
    y = raw * cw_ref[CONV_W - 1:CONV_W, :]
    for j in range(CONV_W - 1):
        start = 8 - (CONV_W - 1) + j
        y = y + xp[start:start + rows, :] * cw_ref[j:j + 1, :]
    qkv = _silu(y)
    ones_blk = _block_ones()
    return (_l2norm_heads(qkv[:, 0:w], ones_blk) * HEAD_DIM ** -0.5, _l2norm_heads(qkv[:, w:2 * w], ones_blk),
            qkv[:, 2 * w:3 * w])


def _gdn_prompt_kernel(p_ref, o_ref, sout_ref, st_s, *, tt):
    w = BR_WIDTH
    t = pl.program_id(1)

    @pl.when(t == 0)
    def _():
        st_s[...] = jnp.zeros(st_s.shape, F32)

    gam_i = _dot_xr(_chunk_tri(tt), p_ref[:, 4 * w:5 * w], 3)
    lane_j = _iota2((tt, w), 1) & (CHUNK - 1)
    row_j = _iota2((tt, w), 0) & (CHUNK - 1)
    gam_j = _dot_xr(_chunk_ones(tt), jnp.where(lane_j == row_j, gam_i, 0.0), 3)

    m = _head_masks()
    rows = [slice(c * CHUNK, (c + 1) * CHUNK) for c in range(tt // CHUNK)]
    part = lambda j: [p_ref[r, j * w:(j + 1) * w] for r in rows]
    q, k = part(0), part(1)
    gam_i = [gam_i[r, :] for r in rows]
    u_v, w_k, attn = _gdn_chunk_local(q, k, part(2), gam_i, [gam_j[r, :] for r in rows], part(5), m)
    st = st_s[...]
    outs = []
    for c in range(len(rows)):
        o, st = _gdn_chunk_state(q[c], k[c], gam_i[c], u_v[c], w_k[c], attn[c], st, m)
        outs.append(o)
    st_s[...] = st
    o_ref[...] = jnp.concatenate(outs, axis=0)

    @pl.when(t == pl.num_programs(1) - 1)
    def _():
        _write_state(sout_ref, st_s[...])


def _gdn_prompt(p, bsz, seq, tt):
    nt = seq // tt
    w = BR_WIDTH
    return pl.pallas_call(
        functools.partial(_gdn_prompt_kernel, tt=tt),
        grid=(bsz, nt),
        in_specs=[pl.BlockSpec((tt, PREP_A), lambda b, t: (b * nt + t, 0))],
        out_specs=[pl.BlockSpec((tt, w), lambda b, t: (b * nt + t, 0)), _STATE_SPEC],
        out_shape=[jax.ShapeDtypeStruct((bsz * seq, w), F32),
                   jax.ShapeDtypeStruct((bsz, N_HEADS, HEAD_DIM, HEAD_DIM), F32)],
        scratch_shapes=[pltpu.VMEM((w, w), F32)],
        compiler_params=_cparams("parallel", "arbitrary"),
        name="gdn_prompt",
    )(p)


def _s5_kernel(u_ref, x0r_ref, x0i_ref, ar_ref, ai_ref, br_ref, bi_ref, cr_ref, ci_ref, d_ref, glu_ref,
               o_ref, xr_ref, xi_ref, sr_s, si_s, *, steps, rows):
    @pl.when(pl.program_id(0) == 0)
    def _():
        xr_ref[...] = x0r_ref[...]
        xi_ref[...] = x0i_ref[...]

    u = u_ref[...]
    ub = u.astype(BF16)
    sr_s[...] = _dot(ub, br_ref[...])
    si_s[...] = _dot(ub, bi_ref[...])
    a_re = ar_ref[...]
    a_im = ai_ref[...]

    def body(t, carry):
        xr, xi = carry
        r = pl.ds(pl.multiple_of(t * rows, rows), rows)
        nr = a_re * xr - a_im * xi + sr_s[r, :]
        ni = a_re * xi + a_im * xr + si_s[r, :]
        sr_s[r, :] = nr
        si_s[r, :] = ni
        return nr, ni

    xr, xi = lax.fori_loop(0, steps, body, (xr_ref[...], xi_ref[...]))
    xr_ref[...] = xr
    xi_ref[...] = xi
    y = _dot(sr_s[...].astype(BF16), cr_ref[...]) - _dot(si_s[...].astype(BF16), ci_ref[...]) + d_ref[...] * u
    hg = _dot(_gelu(y).astype(BF16), glu_ref[...])
    o_ref[...] = hg[:, 0:BR_WIDTH] * _sigmoid(hg[:, BR_WIDTH:2 * BR_WIDTH])


def _s5(u, x0r, x0i, params, steps_total, rows, steps):
    nt = steps_total // steps
    tile = steps * rows
    return pl.pallas_call(
        functools.partial(_s5_kernel, steps=steps, rows=rows),
        grid=(nt,),
        in_specs=[pl.BlockSpec((tile, BR_WIDTH), lambda t: (t, 0)), _full(x0r.shape), _full(x0i.shape)]
                 + [_full(a.shape) for a in params],
        out_specs=[pl.BlockSpec((tile, BR_WIDTH), lambda t: (t, 0)), _full(x0r.shape), _full(x0i.shape)],
        out_shape=[jax.ShapeDtypeStruct((steps_total * rows, BR_WIDTH), F32),
                   jax.ShapeDtypeStruct(x0r.shape, F32), jax.ShapeDtypeStruct(x0i.shape, F32)],
        scratch_shapes=[pltpu.VMEM((tile, S5_STATE), F32)] * 2,
        compiler_params=_cparams("arbitrary"),
        name="s5",
    )(u, x0r, x0i, *params)


def _s5_params(lam_re, lam_im, b_re, b_im, c_re, c_im, d, log_dt, glu_w):
    lr = jnp.minimum(lam_re, -1e-4)
    li = lam_im
    dt = jnp.exp(log_dt)[:, None]
    mag = jnp.exp(lr * dt)
    ab_re, ab_im = mag * jnp.cos(li * dt), mag * jnp.sin(li * dt)
    den = lr * lr + li * li
    z_re = ((ab_re - 1.0) * lr + ab_im * li) / den
    z_im = (ab_im * lr - (ab_re - 1.0) * li) / den
    bb_re = z_re[..., None] * b_re - z_im[..., None] * b_im
    bb_im = z_re[..., None] * b_im + z_im[..., None] * b_re
    eye = jnp.eye(S5_GROUPS, dtype=F32)

    def pack_in(bb):
        return jnp.einsum('gph,gk->ghkp', bb, eye).reshape(BR_WIDTH, S5_STATE).astype(BF16)

    def pack_out(c):
        return jnp.einsum('ghp,gk->gpkh', c, eye).reshape(S5_STATE, BR_WIDTH).astype(BF16)

    return (ab_re.reshape(1, S5_STATE), ab_im.reshape(1, S5_STATE), pack_in(bb_re), pack_in(bb_im),
            pack_out(c_re), pack_out(c_im), d.reshape(1, BR_WIDTH), glu_w.astype(BF16))


def _expand_mats():
    r = _iota2((HEAD_DIM, HEAD_DIM * HEAD_DIM), 0)
    c = _iota2((HEAD_DIM, HEAD_DIM * HEAD_DIM), 1)
    rep_k = jnp.where(r == (c >> 6), 1.0, 0.0).astype(BF16)
    rep_v = jnp.where(r == (c & (HEAD_DIM - 1)), 1.0, 0.0).astype(BF16)
    return rep_k, rep_v


def _gla_decode_kernel(q_ref, k_ref, v_ref, lf_ref, s_ref, o_ref, sn_ref):
    rep_k, rep_v = _expand_mats()
    q, k, v, lf = q_ref[0, 0], k_ref[0, 0], v_ref[0, 0], lf_ref[0, 0]
    s = s_ref[...]
    dec = jnp.exp(lf)
    qe_x = _dot_xl(q * dec, rep_k, 2)
    o = _dot_nt_xl(qe_x * s, rep_v, 2) + jnp.sum(q * k, axis=-1, keepdims=True) * v
    sn_ref[...] = _dot_xl(dec, rep_k, 3) * s + _dot_xl(k, rep_k, 2) * _dot_xl(v, rep_v, 2)
    o_ref[0] = o


def _gdn_decode_kernel(q_ref, k_ref, v_ref, g_ref, b_ref, s_ref, o_ref, sn_ref):
    rep_k, rep_v = _expand_mats()
    q, k, v = q_ref[0, 0], k_ref[0, 0], v_ref[0, 0]
    eg = jnp.exp(g_ref[0, 0])
    beta = b_ref[0, 0]
    s = s_ref[...]
    k_x = _dot_xl(k, rep_k, 2)
    ks = _dot_nt_xl(k_x * s, rep_v, 2)
    qs = _dot_nt_xl(_dot_xl(q, rep_k, 2) * s, rep_v, 2)
    u = beta * v - (beta * eg) * ks
    o_ref[0] = eg * qs + jnp.sum(q * k, axis=-1, keepdims=True) * u
    sn_ref[...] = eg * s + k_x * _dot_xl(u, rep_v, 2)


def _decode_call(kernel, packs, state, name):
    rows = state.shape[0]
    hw = HEAD_DIM * HEAD_DIM
    vec_specs, vecs = [], []
    for a in packs:
        for j in range(a.shape[0]):
            vec_specs.append(pl.BlockSpec((1, 1, rows, a.shape[3]), lambda h, j=j: (j, h, 0, 0)))
            vecs.append(a)
    return pl.pallas_call(
        kernel,
        grid=(N_HEADS,),
        in_specs=vec_specs + [pl.BlockSpec((rows, hw), lambda h: (0, h))],
        out_specs=[pl.BlockSpec((1, rows, HEAD_DIM), lambda h: (h, 0, 0)), pl.BlockSpec((rows, hw), lambda h: (0, h))],
        out_shape=[jax.ShapeDtypeStruct((N_HEADS, rows, HEAD_DIM), F32), jax.ShapeDtypeStruct(state.shape, F32)],
        compiler_params=_cparams("parallel"),
        name=name,
    )(*vecs, state)


def _put_heads(ref, j, x):
    for h in range(N_HEADS):
        ref[j, h] = x[:, h * HEAD_DIM:(h + 1) * HEAD_DIM]


def _decode_prep_kernel(pa_ref, pb_ref, pd_ref, conv_ref, cw_ref, alog_ref, dtb_ref, gkw_ref, gkb_ref, lb_ref,
                        a_ref, ag_ref, b_ref, d_ref, convn_ref):
    w = BR_WIDTH
    cw = 3 * w
    ones_blk = _block_ones()
    raw = pa_ref[:, 0:cw]
    y = raw * cw_ref[CONV_W - 1:CONV_W, :]
    for j in range(CONV_W - 1):
        y = y + conv_ref[:, j * cw:(j + 1) * cw] * cw_ref[j:j + 1, :]
    convn_ref[:, 0:(CONV_W - 2) * cw] = conv_ref[:, cw:(CONV_W - 1) * cw]
    convn_ref[:, (CONV_W - 2) * cw:(CONV_W - 1) * cw] = raw
    qkv = _silu(y)
    _put_heads(a_ref, 0, _l2norm_heads(qkv[:, 0:w], ones_blk) * HEAD_DIM ** -0.5)
    _put_heads(a_ref, 1, _l2norm_heads(qkv[:, w:2 * w], ones_blk))
    _put_heads(a_ref, 2, qkv[:, 2 * w:3 * w])
    g, beta = _gdn_prep(pa_ref[:, 4 * w:4 * w + LANE], alog_ref, dtb_ref)
    for h in range(N_HEADS):
        ag_ref[0, h] = g[:, h * HEAD_DIM:h * HEAD_DIM + 1]
        ag_ref[1, h] = beta[:, h * HEAD_DIM:h * HEAD_DIM + 1]
    for ref, (q, k, v, _, lf) in ((b_ref, _gla_prep(pb_ref[...], gkw_ref, gkb_ref)),
                                  (d_ref, _hgrn_prep(pd_ref[...], lb_ref))):
        for j, val in enumerate((q, k, v, lf)):
            _put_heads(ref, j, val)


def _decode_prep(pa, pb, pd, conv, params):
    rows = pa.shape[0]
    ins = (pa, pb, pd, conv) + tuple(params)
    per_head = lambda n, width: (n, N_HEADS, rows, width)
    shapes = [per_head(3, HEAD_DIM), per_head(2, 1), per_head(4, HEAD_DIM), per_head(4, HEAD_DIM), conv.shape]
    return pl.pallas_call(
        _decode_prep_kernel,
        grid=(1,),
        in_specs=[_full(a.shape) for a in ins],
        out_specs=[_full(s) for s in shapes],
        out_shape=[jax.ShapeDtypeStruct(s, F32) for s in shapes],
        compiler_params=_cparams("arbitrary"),
        name="decode_prep",
    )(*ins)


def _decode_post_kernel(oa_ref, ob_ref, od_ref, pa_ref, pb_ref, pd_ref, nwa_ref, nwb_ref, nwd_ref,
                        a_ref, b_ref, d_ref):
    w = BR_WIDTH
    ones_blk = _block_ones()
    heads = lambda ref: jnp.concatenate([ref[h] for h in range(N_HEADS)], axis=1)
    for out, o, p, nw in ((a_ref, oa_ref, pa_ref, nwa_ref), (b_ref, ob_ref, pb_ref, nwb_ref),
                          (d_ref, od_ref, pd_ref, nwd_ref)):
        out[...] = _head_norm_gate(heads(o), _silu(p[:, 3 * w:4 * w]), nw[...], ones_blk)


def _decode_post(oa, ob, od, pa, pb, pd, nwa, nwb, nwd):
    ins = (oa, ob, od, pa, pb, pd, nwa, nwb, nwd)
    shp = (oa.shape[1], BR_WIDTH)
    return pl.pallas_call(
        _decode_post_kernel,
        grid=(1,),
        in_specs=[_full(a.shape) for a in ins],
        out_specs=[_full(shp)] * 3,
        out_shape=[jax.ShapeDtypeStruct(shp, F32)] * 3,
        compiler_params=_cparams("arbitrary"),
        name="decode_post",
    )(*ins)


def _pack_w_in(w_in):
    sizes = (3 * BR_WIDTH, N_HEADS, N_HEADS, BR_WIDTH, BR_WIDTH, BR_WIDTH, BR_WIDTH, GLA_GATE_RANK, BR_WIDTH,
             BR_WIDTH, BR_WIDTH, BR_WIDTH, BR_WIDTH, BR_WIDTH, N_BRANCH * D_MODEL)
    w_in = w_in.astype(BF16)
    parts, off = [], 0
    for n in sizes:
        parts.append(w_in[..., off:off + n])
        off += n
    (a_qkv, a_alpha, a_beta, a_gate, b_q, b_k, b_v, b_gk, b_gate, c_u, d_q, d_f, d_i, d_gate, merge) = parts
    zpad = lambda n: jnp.zeros(w_in.shape[:-1] + (n,), BF16)
    mix = jnp.concatenate([a_qkv, a_gate, a_alpha, a_beta, zpad(LANE - 2 * N_HEADS),
                           b_q, b_k, b_v, b_gate, b_gk, zpad(LANE - GLA_GATE_RANK),
                           c_u, d_q, d_f, d_i, d_gate], axis=-1)
    return mix, merge


def _tile_sizes(seq):
    pick = lambda n: n if seq % n == 0 else seq
    return pick(512), pick(256), pick(128)


def _hgrn_lower_bounds(logits):
    p = jax.nn.softmax(logits, axis=0)
    return jnp.cumsum(p, axis=0) - p[0]


def _row(x):
    return x.reshape(1, -1)


def _rep_heads(x):
    return jnp.repeat(x, HEAD_DIM).reshape(1, BR_WIDTH)


def kernel(x_prompt, x_sample, state_gdn_conv, state_gdn, state_gla, state_s5_re, state_s5_im, state_hgrn, norm1_w, w_in, gdn_conv_w, gdn_a_log, gdn_dt_bias, gdn_norm_w, gla_gk_w, gla_gk_b, gla_norm_w, s5_lambda_re, s5_lambda_im, s5_b_re, s5_b_im, s5_c_re, s5_c_im, s5_d, s5_log_dt, s5_glu_w, hgrn_lb_logits, hgrn_norm_w, w_branch, w_out, norm2_w, ffn_w_gate, ffn_w_up, ffn_w_down, final_norm_w):
    bsz, seq, _ = x_prompt.shape
    dbs = x_sample.shape[0]
    tm_p, tt, s5_steps = _tile_sizes(seq)
    lb_all = _hgrn_lower_bounds(hgrn_lb_logits)
    w_mix_all, w_merge_all = _pack_w_in(w_in)
    gkw_all = jnp.concatenate([gla_gk_w, jnp.zeros((DEPTH, LANE - GLA_GATE_RANK, BR_WIDTH), F32)], axis=1)
    wb_all, wo_all = w_branch.astype(BF16), w_out.astype(BF16)
    wg_all, wu_all, wd_all = ffn_w_gate.astype(BF16), ffn_w_up.astype(BF16), ffn_w_down.astype(BF16)
    hp = x_prompt.reshape(bsz * seq, D_MODEL)
    hs = x_sample.reshape(dbs, D_MODEL)
    zeros_p = jnp.zeros((bsz, S5_STATE), F32)
    fw = _row(final_norm_w)
    outs_p = [[] for _ in range(6)]
    outs_s = [[] for _ in range(6)]
    for i in range(DEPTH):
        final = i == DEPTH - 1
        w_mix, w_merge, gkw = w_mix_all[i], w_merge_all[i], gkw_all[i]
        nw1, nw2 = _row(norm1_w[i]), _row(norm2_w[i])
        gdn_params = (gdn_conv_w[i], _rep_heads(gdn_a_log[i]), _rep_heads(gdn_dt_bias[i]), _row(gdn_norm_w[i]))
        gla_params = (gkw, _row(gla_gk_b[i]), _row(gla_norm_w[i]))
        hgrn_params = (_row(lb_all[i]), _row(hgrn_norm_w[i]))
        s5_params = _s5_params(s5_lambda_re[i], s5_lambda_im[i], s5_b_re[i], s5_b_im[i], s5_c_re[i], s5_c_im[i],
                               s5_d[i], s5_log_dt[i], s5_glu_w[i])
        prep_params = (gdn_conv_w[i], gdn_params[1], gdn_params[2], gkw, gla_params[1], hgrn_params[0])
        wb, wo, wg, wu, wd = wb_all[i], wo_all[i], wg_all[i], wu_all[i], wd_all[i]

        ga, gb, pc, gd, conv_p = _in_proj_prompt(hp, nw1, w_mix, prep_params, tm_p, seq)
        o_a, st_a = _gdn_prompt(ga, bsz, seq, tt)
        o_b, st_b = _linear_prompt(gb, bsz, seq, tt, "gla_prompt")
        o_d, st_d = _linear_prompt(gd, bsz, seq, tt, "hgrn_prompt")
        o_c, xr_p, xi_p = _s5(pc.reshape(seq * bsz, BR_WIDTH), zeros_p, zeros_p, s5_params, seq, bsz, s5_steps)
        hp = _merge(hp, o_a, o_b, o_c.reshape(seq, bsz * BR_WIDTH), o_d, nw1, w_merge, wb, wo, tm_p, seq,
                    prepared=(ga, gb, gd), gains=(gdn_params[3], gla_params[2], hgrn_params[1]))
        hp = _ffn(hp, nw2, wg, wu, wd, fw, tm_p, final)
        for lst, s in zip(outs_p, (conv_p, st_a, st_b, xr_p.reshape(bsz, S5_GROUPS, S5_P),
                                   xi_p.reshape(bsz, S5_GROUPS, S5_P), st_d)):
            lst.append(s)

        pa, pb, pc, pd = _in_proj(hs, nw1, w_mix, dbs, dbs)
        va, vg, vb, vd, conv_s = _decode_prep(
            pa, pb, pd, state_gdn_conv[i].reshape(dbs, (CONV_W - 1) * 3 * BR_WIDTH), prep_params)
        flat = lambda s: s.reshape(dbs, N_HEADS * HEAD_DIM * HEAD_DIM)
        oa_h, sa = _decode_call(_gdn_decode_kernel, [va, vg], flat(state_gdn[i]), "gdn_decode")
        ob_h, sb = _decode_call(_gla_decode_kernel, [vb], flat(state_gla[i]), "gla_decode")
        od_h, sd = _decode_call(_gla_decode_kernel, [vd], flat(state_hgrn[i]), "hgrn_decode")
        o_a, o_b, o_d = _decode_post(oa_h, ob_h, od_h, pa, pb, pd, gdn_params[3], gla_params[2], hgrn_params[1])
        o_c, xr_s, xi_s = _s5(pc, state_s5_re[i].reshape(dbs, S5_STATE), state_s5_im[i].reshape(dbs, S5_STATE),
                              s5_params, 1, dbs, 1)
        hs = _merge(hs, o_a, o_b, o_c, o_d, nw1, w_merge, wb, wo, dbs, dbs)
        hs = _ffn(hs, nw2, wg, wu, wd, fw, dbs, final)
        st5 = lambda s: s.reshape(dbs, N_HEADS, HEAD_DIM, HEAD_DIM)
        for lst, s in zip(outs_s, (conv_s.reshape(dbs, CONV_W - 1, 3 * BR_WIDTH), st5(sa), st5(sb),
                                   xr_s.reshape(dbs, S5_GROUPS, S5_P), xi_s.reshape(dbs, S5_GROUPS, S5_P), st5(sd))):
            lst.append(s)

    return (hp.reshape(bsz, seq, D_MODEL), hs.reshape(dbs, 1, D_MODEL),
            *[jnp.stack(l) for l in outs_p], *[jnp.stack(l) for l in outs_s])
```

```python
import functools
import math

import jax
import jax.numpy as jnp
from jax import lax
from jax.experimental import pallas as pl
from jax.experimental.pallas import tpu as pltpu

F32 = jnp.float32
BF16 = jnp.bfloat16

D_MODEL = 1024
DEPTH = 4
N_BRANCH = 4
BR_WIDTH = D_MODEL // N_BRANCH
HEAD_DIM = 64
N_HEADS = BR_WIDTH // HEAD_DIM
CONV_W = 4
CHUNK = 64
SUB = 16
GLA_GATE_RANK = 16
GLA_GATE_NORM = 16.0
S5_GROUP = 16
S5_GROUPS = BR_WIDTH // S5_GROUP
S5_P = 64
S5_STATE = S5_GROUPS * S5_P
D_FF = -(-8 * D_MODEL // (3 * 256)) * 256
FF_TILE = 256
EPS = 1e-6
SEQ_PER_STEP = 4
STREAM_SKEW = 1
EXP_CAP = 60.0
LANE = 128
SEG_A = 3 * BR_WIDTH + BR_WIDTH + LANE
SEG_B = 4 * BR_WIDTH + LANE
SEG_C = BR_WIDTH
SEG_D = 4 * BR_WIDTH
SEG_OFF = (0, SEG_A, SEG_A + SEG_B, SEG_A + SEG_B + SEG_C, SEG_A + SEG_B + SEG_C + SEG_D)
N_MIX = SEG_OFF[-1]
PREP_A = 6 * BR_WIDTH
PREP_B = 5 * BR_WIDTH
VMEM_LIMIT = 56 * 1024 * 1024


def _cparams(*sem):
    return pltpu.CompilerParams(dimension_semantics=sem, vmem_limit_bytes=VMEM_LIMIT)


def _full(shape):
    n = len(shape)
    return pl.BlockSpec(shape, lambda *_: (0,) * n)


def _rms(x, w):
    return x * lax.rsqrt(jnp.mean(x * x, axis=-1, keepdims=True) + EPS) * w


def _sigmoid(x):
    return 1.0 / (1.0 + jnp.exp(-x))


def _silu(x):
    return x * _sigmoid(x)


def _softplus(x):
    return jnp.maximum(x, 0.0) + jnp.log(1.0 + jnp.exp(-jnp.abs(x)))


def _log_sigmoid(x):
    return -_softplus(-x)


def _gelu(x):
    return 0.5 * x * (1.0 + jnp.tanh(math.sqrt(2.0 / math.pi) * (x + 0.044715 * x * x * x)))


def _dot(a, b):
    return jnp.dot(a.astype(BF16), b.astype(BF16), preferred_element_type=F32)


def _dot_nt(a, b):
    return lax.dot_general(a.astype(BF16), b.astype(BF16), (((1,), (1,)), ((), ())), preferred_element_type=F32)


def _dot_tn(a, b):
    return lax.dot_general(a.astype(BF16), b.astype(BF16), (((0,), (0,)), ((), ())), preferred_element_type=F32)


def _split(x, terms):
    parts, rest = [], x
    for t in range(terms):
        p = rest.astype(BF16)
        parts.append(p)
        if t + 1 < terms:
            rest = rest - p.astype(F32)
    return parts


def _dot_xl(x, c, terms=3):
    return sum(jnp.dot(p, c, preferred_element_type=F32) for p in _split(x, terms))


def _dot_xr(c, x, terms=3):
    return sum(jnp.dot(c, p, preferred_element_type=F32) for p in _split(x, terms))


def _dot_nt_xl(x, c, terms=3):
    return sum(lax.dot_general(p, c, (((1,), (1,)), ((), ())), preferred_element_type=F32) for p in _split(x, terms))


def _iota2(shape, dim):
    return lax.broadcasted_iota(jnp.int32, shape, dim)


def _head_masks():
    r = _iota2((CHUNK, N_HEADS * CHUNK), 0)
    c = _iota2((CHUNK, N_HEADS * CHUNK), 1) & (CHUNK - 1)
    rr = _iota2((N_HEADS * CHUNK, N_HEADS * HEAD_DIM), 0)
    cc = _iota2((N_HEADS * CHUNK, N_HEADS * HEAD_DIM), 1)
    block = (rr >> 6) == (cc >> 6)
    one = lambda mask: jnp.where(mask, 1.0, 0.0)
    return dict(causal=r >= c, strict=r > c, eye=r == c, same_sub=(r >> 4) == (c >> 4), block=block,
                block16=one(block).astype(BF16), sub16=one((rr >> 4) == (cc >> 4)).astype(BF16),
                eye_sub=one(_iota2((SUB, BR_WIDTH), 0) == (_iota2((SUB, BR_WIDTH), 1) & (SUB - 1))))


def _chunk_tri(tt):
    r = _iota2((tt, tt), 0)
    c = _iota2((tt, tt), 1)
    return jnp.where((r >= c) & ((r >> 6) == (c >> 6)), 1.0, 0.0).astype(BF16)


def _chunk_ones(tt):
    r = _iota2((tt, tt), 0) >> 6
    c = _iota2((tt, tt), 1) >> 6
    return jnp.where(r == c, 1.0, 0.0).astype(BF16)


def _block_ones():
    return _chunk_ones(BR_WIDTH)


def _blockdiag(x, block16):
    return jnp.concatenate([x.astype(BF16)] * N_HEADS, axis=0) * block16


def _head_norm_gate(o, silu_gate, w, ones_blk):
    ms = _dot_xl(o * o, ones_blk, 2) * (1.0 / HEAD_DIM)
    return o * lax.rsqrt(ms + EPS) * w * silu_gate


def _in_proj_kernel(x_ref, nw_ref, w_ref, pa_ref, pb_ref, pc_ref, pd_ref):
    xn = _rms(x_ref[...], nw_ref[...]).astype(BF16)
    for i, ref in enumerate((pa_ref, pb_ref, pc_ref, pd_ref)):
        ref[...] = _dot(xn, w_ref[:, SEG_OFF[i]:SEG_OFF[i + 1]])


def _time_major_spec(tm, seq):
    nt = seq // tm
    return pl.BlockSpec((tm, BR_WIDTH), lambda i: (i % nt, i // nt))


def _in_proj_prompt_kernel(x_ref, halo_ref, nw_ref, w_ref, cw_ref, alog_ref, dtb_ref, gkw_ref, gkb_ref, lb_ref,
                           ga_ref, gb_ref, pc_ref, gd_ref, conv_ref, *, nt):
    w = BR_WIDTH
    seg = lambda i: w_ref[:, SEG_OFF[i]:SEG_OFF[i + 1]]
    xn = _rms(x_ref[...], nw_ref[...]).astype(BF16)
    pd = _dot(xn, seg(3))
    pa = _dot(xn, seg(0))
    hist = _dot(_rms(halo_ref[...], nw_ref[...]), w_ref[:, 0:3 * w])
    for j, val in enumerate(_hgrn_prep(pd, lb_ref)):
        gd_ref[:, j * w:(j + 1) * w] = val
    pb = _dot(xn, seg(1))
    raw = pa[:, 0:3 * w]
    hist = jnp.where((pl.program_id(0) % nt) == 0, 0.0, hist)
    rows = raw.shape[0]
    conv_ref[0] = raw[rows - (CONV_W - 1):rows, :]
    q, k, v = _gdn_conv_qkv(raw, hist, cw_ref)
    g, beta = _gdn_prep(pa[:, 4 * w:4 * w + LANE], alog_ref, dtb_ref)
    for j, val in enumerate((q, k, v, _silu(pa[:, 3 * w:4 * w]), g, beta)):
        ga_ref[:, j * w:(j + 1) * w] = val
    pc_ref[...] = _dot(xn, seg(2))
    for j, val in enumerate(_gla_prep(pb, gkw_ref, gkb_ref)):
        gb_ref[:, j * w:(j + 1) * w] = val


def _in_proj_prompt(x, nw, w, params, tm, seq):
    t = x.shape[0]
    nt = seq // tm
    bsz = t // seq
    row = lambda n: pl.BlockSpec((tm, n), lambda i: (i, 0))
    halo = pl.BlockSpec((8, D_MODEL), lambda i: (jnp.maximum(i * (tm // 8) - 1, 0), 0))
    return pl.pallas_call(
        functools.partial(_in_proj_prompt_kernel, nt=nt),
        grid=(t // tm,),
        in_specs=[row(D_MODEL), halo, _full((1, D_MODEL)), _full((D_MODEL, N_MIX))] + [_full(a.shape) for a in params],
        out_specs=[row(PREP_A), row(PREP_B), _time_major_spec(tm, seq), row(PREP_B),
                   pl.BlockSpec((1, CONV_W - 1, 3 * BR_WIDTH), lambda i: (i // nt, 0, 0))],
        out_shape=[jax.ShapeDtypeStruct((t, PREP_A), F32), jax.ShapeDtypeStruct((t, PREP_B), F32),
                   jax.ShapeDtypeStruct((seq, bsz * BR_WIDTH), F32), jax.ShapeDtypeStruct((t, PREP_B), F32),
                   jax.ShapeDtypeStruct((bsz, CONV_W - 1, 3 * BR_WIDTH), F32)],
        compiler_params=_cparams("arbitrary"),
        name="in_proj_prompt",
    )(x, x, nw, w, *params)


def _in_proj(x, nw, w, tm, seq):
    t = x.shape[0]
    widths = (SEG_A, SEG_B, SEG_C, SEG_D)
    out_specs = [pl.BlockSpec((tm, n), lambda i: (i, 0)) for n in widths]
    out_shape = [jax.ShapeDtypeStruct((t, n), F32) for n in widths]
    out_specs[2] = _time_major_spec(tm, seq)
    out_shape[2] = jax.ShapeDtypeStruct((seq, (t // seq) * BR_WIDTH), F32)
    return pl.pallas_call(
        _in_proj_kernel,
        grid=(t // tm,),
        in_specs=[pl.BlockSpec((tm, D_MODEL), lambda i: (i, 0)), _full((1, D_MODEL)), _full((D_MODEL, N_MIX))],
        out_specs=out_specs,
        out_shape=out_shape,
        compiler_params=_cparams("parallel"),
        name="in_proj",
    )(x, nw, w)


def _merge_kernel(h_ref, oa_ref, ob_ref, oc_ref, od_ref, *rest, head_norm):
    if head_norm:
        gates, gains, rest = rest[0:3], rest[3:6], rest[6:]
    nw_ref, wm_ref, wb_ref, wo_ref, out_ref = rest
    h = h_ref[...]
    xn = _rms(h, nw_ref[...]).astype(BF16)
    o_vals = [oa_ref[...], ob_ref[...], oc_ref[...], od_ref[...]]
    if head_norm:
        ones_blk = _block_ones()
        for k, g_ref, w_ref in zip((0, 1, 3), gates, gains):
            o_vals[k] = _head_norm_gate(o_vals[k], g_ref[...], w_ref[...], ones_blk)
    proj = lambda k: (_dot(xn, wm_ref[:, k * D_MODEL:(k + 1) * D_MODEL]), _dot(o_vals[k], wb_ref[k]))
    acc = jnp.zeros(h.shape, F32)
    nxt = proj(0)
    for k in range(N_BRANCH):
        gate, br = nxt
        if k + 1 < N_BRANCH:
            nxt = proj(k + 1)
        acc = acc + _sigmoid(gate) * br
    out_ref[...] = h + _dot(acc, wo_ref[...])


def _merge(h, oa, ob, oc, od, nw, wm, wb, wo, tm, seq, prepared=None, gains=None):
    t = h.shape[0]
    row = lambda n: pl.BlockSpec((tm, n), lambda i: (i, 0))
    head_norm = prepared is not None
    extra, extra_specs = (), []
    if head_norm:
        extra = tuple(prepared) + tuple(gains)
        extra_specs = [pl.BlockSpec((tm, BR_WIDTH), lambda i: (i, 3))] * 3 + [_full((1, BR_WIDTH))] * 3
    return pl.pallas_call(
        functools.partial(_merge_kernel, head_norm=head_norm),
        grid=(t // tm,),
        in_specs=[row(D_MODEL), row(BR_WIDTH), row(BR_WIDTH), _time_major_spec(tm, seq), row(BR_WIDTH)]
                 + extra_specs + [_full((1, D_MODEL)), _full((D_MODEL, N_BRANCH * D_MODEL)),
                                  _full((N_BRANCH, BR_WIDTH, D_MODEL)), _full((D_MODEL, D_MODEL))],
        out_specs=row(D_MODEL),
        out_shape=jax.ShapeDtypeStruct((t, D_MODEL), F32),
        compiler_params=_cparams("parallel"),
        name="merge",
    )(h, oa, ob, oc, od, *extra, nw, wm, wb, wo)


def _ffn_kernel(h_ref, nw_ref, wg_ref, wu_ref, wd_ref, fw_ref, out_ref, *, final):
    h = h_ref[...]
    hn = _rms(h, nw_ref[...]).astype(BF16)
    n_tiles = D_FF // FF_TILE
    cols = [slice(c * FF_TILE, (c + 1) * FF_TILE) for c in range(n_tiles)]
    up = lambda sl: (_dot(hn, wg_ref[:, sl]), _dot(hn, wu_ref[:, sl]))
    acc = jnp.zeros(h.shape, F32)
    nxt = up(cols[0])
    for c in range(n_tiles):
        g, u = nxt
        if c + 1 < n_tiles:
            nxt = up(cols[c + 1])
        acc = acc + _dot(_silu(g) * u, wd_ref[cols[c], :])
    hnew = h + acc
    out_ref[...] = _rms(hnew, fw_ref[...]) if final else hnew


def _ffn(h, nw, wg, wu, wd, fw, tm, final):
    t = h.shape[0]
    row = pl.BlockSpec((tm, D_MODEL), lambda i: (i, 0))
    return pl.pallas_call(
        functools.partial(_ffn_kernel, final=final),
        grid=(t // tm,),
        in_specs=[row, _full((1, D_MODEL)), _full((D_MODEL, D_FF)), _full((D_MODEL, D_FF)),
                  _full((D_FF, D_MODEL)), _full((1, D_MODEL))],
        out_specs=row,
        out_shape=jax.ShapeDtypeStruct((t, D_MODEL), F32),
        compiler_params=_cparams("parallel"),
        name="ffn",
    )(h, nw, wg, wu, wd, fw)


def _run_skewed(streams, skew):
    results = [None] * len(streams)
    live, step = [], 0
    pending = list(enumerate(streams))
    while pending or live:
        while pending and pending[0][0] * skew <= step:
            live.append(pending.pop(0))
        still = []
        for idx, gen in live:
            try:
                next(gen)
                still.append((idx, gen))
            except StopIteration as done:
                results[idx] = done.value
        live = still
        step += 1
    return results


def _gla_stream(load, st, m, tt):
    each = lambda f, *xs: [f(*a) for a in zip(*xs)]
    blk = m["block16"]
    b_all = _dot_xr(_chunk_tri(tt), load(4, slice(0, tt)), 3)
    yield
    rows = [slice(c * CHUNK, (c + 1) * CHUNK) for c in range(tt // CHUNK)]
    q, k, v = ([load(j, r) for r in rows] for j in range(3))
    b = [b_all[r, :] for r in rows]
    b_last = each(lambda x: x[CHUNK - 1:CHUNK, :], b)
    scores = []
    for i0 in range(0, CHUNK, SUB):
        q_i = each(lambda qq, x: qq[i0:i0 + SUB, :] * jnp.exp(x[i0:i0 + SUB, :] - x[i0:i0 + 1, :]), q, b)
        k_i = each(lambda kk, x: kk * jnp.exp(jnp.minimum(x[i0:i0 + 1, :] - x, EXP_CAP)), k, b)
        yield
        scores.append(each(lambda a, c: _dot_nt(a, _blockdiag(c, blk)), q_i, k_i))
    kv = each(lambda vv, kk, x, xl: jnp.where(m["block"], _dot_tn(vv, kk * jnp.exp(xl - x)), 0.0), v, k, b, b_last)
    yield
    attn = each(lambda *r: jnp.where(m["causal"], jnp.concatenate(r, axis=0), 0.0), *scores)
    qe = each(lambda qq, x: qq * jnp.exp(x), q, b)
    yield
    o_loc = each(lambda a, vv: _dot(a, _blockdiag(vv, blk)), attn, v)
    yield
    outs = []
    for c in range(len(q)):
        outs.append(_dot_nt(qe[c], st) + o_loc[c])
        st = st * jnp.exp(b_last[c]) + kv[c]
        yield
    return outs, st


def _write_state(out_ref, i, st):
    r = _iota2((BR_WIDTH, BR_WIDTH), 0)
    c = _iota2((BR_WIDTH, BR_WIDTH), 1)
    eye = jnp.where(r == c, 1.0, 0.0).astype(BF16)
    nt_dot = lambda p: lax.dot_general(eye, p, (((1,), (1,)), ((), ())), preferred_element_type=F32)
    st_t = sum(nt_dot(p) for p in _split(st, 3))
    for h in range(N_HEADS):
        out_ref[i, h] = st_t[h * HEAD_DIM:(h + 1) * HEAD_DIM, h * HEAD_DIM:(h + 1) * HEAD_DIM]


def _gla_prep(pb, gkw_ref, gkb_ref):
    w = BR_WIDTH
    lf = _log_sigmoid(_dot(pb[:, 4 * w:4 * w + LANE], gkw_ref[...]) + gkb_ref[...]) * (1.0 / GLA_GATE_NORM)
    return pb[:, 0:w] * HEAD_DIM ** -0.5, pb[:, w:2 * w], pb[:, 2 * w:3 * w], _silu(pb[:, 3 * w:4 * w]), lf


def _hgrn_prep(pd, lb_ref):
    w = BR_WIDTH
    lb = lb_ref[...]
    xf = pd[:, w:2 * w]
    return (_silu(pd[:, 0:w]) * HEAD_DIM ** -0.5, (1.0 - lb) * _sigmoid(-xf), pd[:, 2 * w:3 * w],
            _silu(pd[:, 3 * w:4 * w]), _hgrn_log_forget(xf, lb))


def _linear_prompt_kernel(p_ref, o_ref, sout_ref, st_s, *, tt):
    @pl.when(pl.program_id(1) == 0)
    def _():
        st_s[...] = jnp.zeros(st_s.shape, F32)

    w = BR_WIDTH
    m = _head_masks()
    n_seq = p_ref.shape[0]
    load = lambda i: (lambda j, r: p_ref[i, r, j * w:(j + 1) * w])
    results = _run_skewed([_gla_stream(load(i), st_s[i], m, tt) for i in range(n_seq)], STREAM_SKEW)
    for i, (outs, st) in enumerate(results):
        st_s[i] = st
        o_ref[i] = jnp.concatenate(outs, axis=0)

    @pl.when(pl.program_id(1) == pl.num_programs(1) - 1)
    def _():
        for i in range(n_seq):
            _write_state(sout_ref, i, st_s[i])


def _hgrn_log_forget(xf, lb):
    ls_pos = _log_sigmoid(xf)
    pos = lb > 0
    lb_safe = jnp.where(pos, lb, 1.0)
    mixed = ls_pos + _softplus(jnp.log(lb_safe) - xf)
    return jnp.where(pos, mixed, ls_pos)


def _mixer_call(kernel, p, bsz, seq, tt, name):
    nt = seq // tt
    width = p.shape[1]
    n_seq = SEQ_PER_STEP if bsz % SEQ_PER_STEP == 0 else 1
    o, st = pl.pallas_call(
        functools.partial(kernel, tt=tt),
        grid=(bsz // n_seq, nt),
        in_specs=[pl.BlockSpec((n_seq, tt, width), lambda b, t: (b, t, 0))],
        out_specs=[pl.BlockSpec((n_seq, tt, BR_WIDTH), lambda b, t: (b, t, 0)),
                   pl.BlockSpec((n_seq, N_HEADS, HEAD_DIM, HEAD_DIM), lambda b, t: (b, 0, 0, 0))],
        out_shape=[jax.ShapeDtypeStruct((bsz, seq, BR_WIDTH), F32),
                   jax.ShapeDtypeStruct((bsz, N_HEADS, HEAD_DIM, HEAD_DIM), F32)],
        scratch_shapes=[pltpu.VMEM((n_seq, BR_WIDTH, BR_WIDTH), F32)],
        compiler_params=_cparams("parallel", "arbitrary"),
        name=name,
    )(p.reshape(bsz, seq, width))
    return o.reshape(bsz * seq, BR_WIDTH), st


def _mm_split(a, b, tile_mask):
    a_hi, a_lo = _split(a, 2)
    b_hi, b_lo = (jnp.concatenate([p] * (BR_WIDTH // b.shape[0]), axis=0) * tile_mask for p in _split(b, 2))
    dot = functools.partial(jnp.dot, preferred_element_type=F32)
    return dot(a_hi, b_hi) + (dot(a_lo, b_hi) + dot(a_hi, b_lo))


def _unit_lower_inverse(n, m):
    each = lambda f, *xs: [f(*a) for a in zip(*xs)]
    eye_c, sub, blk = m["eye_sub"], m["sub16"], m["block16"]
    nd = each(lambda a: jnp.where(m["same_sub"], a, 0.0), n)
    low = each(lambda a, d: a - d, n, nd)
    c = each(lambda d: d[0:SUB] + d[SUB:2 * SUB] + d[2 * SUB:3 * SUB] + d[3 * SUB:4 * SUB], nd)
    p = each(lambda a: _mm_split(a, a, sub), c)
    x = each(lambda a: eye_c - a, c)
    yield
    for _ in range(2):
        xp = each(lambda a, b: _mm_split(jnp.concatenate([a, b], axis=0), b, sub), x, p)
        x = each(lambda a, b: a + b[0:SUB], x, xp)
        p = each(lambda b: b[SUB:2 * SUB], xp)
        yield
    x = each(lambda a, b: a + _mm_split(a, b, sub), x, p)
    dinv = each(lambda a: jnp.where(m["same_sub"], jnp.concatenate([a] * (CHUNK // SUB), axis=0), 0.0), x)
    yield
    mm = each(lambda a, b: _mm_split(a, b, blk), dinv, low)
    yield
    m2 = each(lambda a: _mm_split(a, a, blk), mm)
    yield
    e = each(lambda d, a: d + _mm_split(a, d, blk), dinv, m2)
    yield
    return each(lambda a, b: b - _mm_split(a, b, blk), mm, e)


def _gdn_stream(load, st, m, tt):
    each = lambda f, *xs: [f(*a) for a in zip(*xs)]
    w = BR_WIDTH
    blk = m["block16"]
    gam_all = _dot_xr(_chunk_tri(tt), load(4, slice(0, tt)), 3)
    yield
    lane_j = _iota2((tt, w), 1) & (CHUNK - 1)
    row_j = _iota2((tt, w), 0) & (CHUNK - 1)
    gam_t = _dot_xr(_chunk_ones(tt), jnp.where(lane_j == row_j, gam_all, 0.0), 3)
    yield
    rows = [slice(c * CHUNK, (c + 1) * CHUNK) for c in range(tt // CHUNK)]
    q, k, v, bexp = ([load(j, r) for r in rows] for j in (0, 1, 2, 5))
    gam_i = [gam_all[r, :] for r in rows]
    gam_j = [gam_t[r, :] for r in rows]
    decay = each(lambda gi, gj: jnp.where(m["causal"], jnp.exp(jnp.minimum(gi - gj, 0.0)), 0.0), gam_i, gam_j)
    scores = each(lambda kk, qq: _dot_nt(jnp.concatenate([kk, qq], axis=0), _blockdiag(kk, blk)), k, q)
    yield
    attn = each(lambda s, d: s[CHUNK:2 * CHUNK] * d, scores, decay)
    n = each(lambda b, s, d: jnp.where(m["strict"], b * s[0:CHUNK] * d, 0.0), bexp, scores, decay)
    tinv = yield from _unit_lower_inverse(n, m)
    yield
    u_v = each(lambda t, b, vv: _dot(t, _blockdiag(b * vv, blk)), tinv, bexp, v)
    w_k = each(lambda t, b, gi, kk: _dot(t, _blockdiag(b * jnp.exp(gi) * kk, blk)), tinv, bexp, gam_i, k)
    yield
    outs = []
    for c in range(len(rows)):
        gam_last = gam_i[c][CHUNK - 1:CHUNK, :]
        ws = _dot_nt(jnp.concatenate([w_k[c], q[c]], axis=0), st)
        u = u_v[c] - ws[0:CHUNK]
        yield
        outs.append(jnp.exp(gam_i[c]) * ws[CHUNK:2 * CHUNK] + _dot(attn[c], _blockdiag(u, blk)))
        kd = k[c] * jnp.exp(gam_last - gam_i[c])
        st = st * jnp.exp(gam_last) + jnp.where(m["block"], _dot_tn(u, kd), 0.0)
        yield
    return outs, st


def _l2norm_heads(x, ones_blk):
    return x * lax.rsqrt(_dot_xl(x * x, ones_blk, 2) + EPS)


def _gdn_prep(a_alpha_beta, alog_ref, dtb_ref):
    r = _iota2((LANE, BR_WIDTH), 0)
    c = _iota2((LANE, BR_WIDTH), 1) >> 6
    sel_a = jnp.where(r == c, 1.0, 0.0).astype(BF16)
    sel_b = jnp.where(r == c + N_HEADS, 1.0, 0.0).astype(BF16)
    alpha = _dot_xl(a_alpha_beta, sel_a, 3)
    beta = _sigmoid(_dot_xl(a_alpha_beta, sel_b, 3))
    g = -jnp.exp(alog_ref[...]) * _softplus(alpha + dtb_ref[...])
    return g, beta


def _gdn_conv_qkv(raw, hist, cw_ref):
    w = BR_WIDTH
    rows = raw.shape[0]
    xp = jnp.concatenate([hist, raw], axis=0)
    y = raw * cw_ref[CONV_W - 1:CONV_W, :]
    for j in range(CONV_W - 1):
        start = 8 - (CONV_W - 1) + j
        y = y + xp[start:start + rows, :] * cw_ref[j:j + 1, :]
    qkv = _silu(y)
    ones_blk = _block_ones()
    return (_l2norm_heads(qkv[:, 0:w], ones_blk) * HEAD_DIM ** -0.5, _l2norm_heads(qkv[:, w:2 * w], ones_blk),
            qkv[:, 2 * w:3 * w])


def _gdn_prompt_kernel(p_ref, o_ref, sout_ref, st_s, *, tt):
    t = pl.program_id(1)

    @pl.when(t == 0)
    def _():
        st_s[...] = jnp.zeros(st_s.shape, F32)

    w = BR_WIDTH
    m = _head_masks()
    n_seq = p_ref.shape[0]
    load = lambda i: (lambda j, r: p_ref[i, r, j * w:(j + 1) * w])
    results = _run_skewed([_gdn_stream(load(i), st_s[i], m, tt) for i in range(n_seq)], STREAM_SKEW)
    for i, (outs, st) in enumerate(results):
        st_s[i] = st
        o_ref[i] = jnp.concatenate(outs, axis=0)

    @pl.when(t == pl.num_programs(1) - 1)
    def _():
        for i in range(n_seq):
            _write_state(sout_ref, i, st_s[i])


def _s5_kernel(u_ref, x0r_ref, x0i_ref, ar_ref, ai_ref, br_ref, bi_ref, cr_ref, ci_ref, d_ref, glu_ref,
               o_ref, xr_ref, xi_ref, sr_s, si_s, *, steps, rows):
    @pl.when(pl.program_id(0) == 0)
    def _():
        xr_ref[...] = x0r_ref[...]
        xi_ref[...] = x0i_ref[...]

    u = u_ref[...]
    ub = u.astype(BF16)
    sr_s[...] = _dot(ub, br_ref[...])
    si_s[...] = _dot(ub, bi_ref[...])
    a_re = ar_ref[...]
    a_im = ai_ref[...]

    def body(t, carry):
        xr, xi = carry
        r = pl.ds(pl.multiple_of(t * rows, rows), rows)
        nr = a_re * xr - a_im * xi + sr_s[r, :]
        ni = a_re * xi + a_im * xr + si_s[r, :]
        sr_s[r, :] = nr
        si_s[r, :] = ni
        return nr, ni

    xr, xi = lax.fori_loop(0, steps, body, (xr_ref[...], xi_ref[...]))
    xr_ref[...] = xr
    xi_ref[...] = xi
    y = _dot(sr_s[...].astype(BF16), cr_ref[...]) - _dot(si_s[...].astype(BF16), ci_ref[...]) + d_ref[...] * u
    hg = _dot(_gelu(y).astype(BF16), glu_ref[...])
    o_ref[...] = hg[:, 0:BR_WIDTH] * _sigmoid(hg[:, BR_WIDTH:2 * BR_WIDTH])


def _s5(u, x0r, x0i, params, steps_total, rows, steps):
    nt = steps_total // steps
    tile = steps * rows
    return pl.pallas_call(
        functools.partial(_s5_kernel, steps=steps, rows=rows),
        grid=(nt,),
        in_specs=[pl.BlockSpec((tile, BR_WIDTH), lambda t: (t, 0)), _full(x0r.shape), _full(x0i.shape)]
                 + [_full(a.shape) for a in params],
        out_specs=[pl.BlockSpec((tile, BR_WIDTH), lambda t: (t, 0)), _full(x0r.shape), _full(x0i.shape)],
        out_shape=[jax.ShapeDtypeStruct((steps_total * rows, BR_WIDTH), F32),
                   jax.ShapeDtypeStruct(x0r.shape, F32), jax.ShapeDtypeStruct(x0i.shape, F32)],
        scratch_shapes=[pltpu.VMEM((tile, S5_STATE), F32)] * 2,
        compiler_params=_cparams("arbitrary"),
        name="s5",
    )(u, x0r, x0i, *params)


def _s5_params(lam_re, lam_im, b_re, b_im, c_re, c_im, d, log_dt, glu_w):
    lr = jnp.minimum(lam_re, -1e-4)
    li = lam_im
    dt = jnp.exp(log_dt)[:, None]
    mag = jnp.exp(lr * dt)
    ab_re, ab_im = mag * jnp.cos(li * dt), mag * jnp.sin(li * dt)
    den = lr * lr + li * li
    z_re = ((ab_re - 1.0) * lr + ab_im * li) / den
    z_im = (ab_im * lr - (ab_re - 1.0) * li) / den
    bb_re = z_re[..., None] * b_re - z_im[..., None] * b_im
    bb_im = z_re[..., None] * b_im + z_im[..., None] * b_re
    eye = jnp.eye(S5_GROUPS, dtype=F32)

    def pack_in(bb):
        return jnp.einsum('gph,gk->ghkp', bb, eye).reshape(BR_WIDTH, S5_STATE).astype(BF16)

    def pack_out(c):
        return jnp.einsum('ghp,gk->gpkh', c, eye).reshape(S5_STATE, BR_WIDTH).astype(BF16)

    return (ab_re.reshape(1, S5_STATE), ab_im.reshape(1, S5_STATE), pack_in(bb_re), pack_in(bb_im),
            pack_out(c_re), pack_out(c_im), d.reshape(1, BR_WIDTH), glu_w.astype(BF16))


def _expand_mats():
    r = _iota2((HEAD_DIM, HEAD_DIM * HEAD_DIM), 0)
    c = _iota2((HEAD_DIM, HEAD_DIM * HEAD_DIM), 1)
    rep_k = jnp.where(r == (c >> 6), 1.0, 0.0).astype(BF16)
    rep_v = jnp.where(r == (c & (HEAD_DIM - 1)), 1.0, 0.0).astype(BF16)
    return rep_k, rep_v


def _gla_decode_kernel(q_ref, k_ref, v_ref, lf_ref, s_ref, o_ref, sn_ref):
    rep_k, rep_v = _expand_mats()
    q, k, v, lf = q_ref[0, 0], k_ref[0, 0], v_ref[0, 0], lf_ref[0, 0]
    s = s_ref[...]
    dec = jnp.exp(lf)
    qe_x = _dot_xl(q * dec, rep_k, 2)
    o = _dot_nt_xl(qe_x * s, rep_v, 2) + jnp.sum(q * k, axis=-1, keepdims=True) * v
    sn_ref[...] = _dot_xl(dec, rep_k, 3) * s + _dot_xl(k, rep_k, 2) * _dot_xl(v, rep_v, 2)
    o_ref[0] = o


def _gdn_decode_kernel(q_ref, k_ref, v_ref, g_ref, b_ref, s_ref, o_ref, sn_ref):
    rep_k, rep_v = _expand_mats()
    q, k, v = q_ref[0, 0], k_ref[0, 0], v_ref[0, 0]
    eg = jnp.exp(g_ref[0, 0])
    beta = b_ref[0, 0]
    s = s_ref[...]
    k_x = _dot_xl(k, rep_k, 2)
    ks = _dot_nt_xl(k_x * s, rep_v, 2)
    qs = _dot_nt_xl(_dot_xl(q, rep_k, 2) * s, rep_v, 2)
    u = beta * v - (beta * eg) * ks
    o_ref[0] = eg * qs + jnp.sum(q * k, axis=-1, keepdims=True) * u
    sn_ref[...] = eg * s + k_x * _dot_xl(u, rep_v, 2)


def _decode_call(kernel, packs, state, name):
    rows = state.shape[0]
    hw = HEAD_DIM * HEAD_DIM
    vec_specs, vecs = [], []
    for a in packs:
        for j in range(a.shape[0]):
            vec_specs.append(pl.BlockSpec((1, 1, rows, a.shape[3]), lambda h, j=j: (j, h, 0, 0)))
            vecs.append(a)
    return pl.pallas_call(
        kernel,
        grid=(N_HEADS,),
        in_specs=vec_specs + [pl.BlockSpec((rows, hw), lambda h: (0, h))],
        out_specs=[pl.BlockSpec((1, rows, HEAD_DIM), lambda h: (h, 0, 0)), pl.BlockSpec((rows, hw), lambda h: (0, h))],
        out_shape=[jax.ShapeDtypeStruct((N_HEADS, rows, HEAD_DIM), F32), jax.ShapeDtypeStruct(state.shape, F32)],
        compiler_params=_cparams("parallel"),
        name=name,
    )(*vecs, state)


def _put_heads(ref, j, x):
    for h in range(N_HEADS):
        ref[j, h] = x[:, h * HEAD_DIM:(h + 1) * HEAD_DIM]


def _decode_prep_kernel(pa_ref, pb_ref, pd_ref, conv_ref, cw_ref, alog_ref, dtb_ref, gkw_ref, gkb_ref, lb_ref,
                        a_ref, ag_ref, b_ref, d_ref, convn_ref):
    w = BR_WIDTH
    cw = 3 * w
    ones_blk = _block_ones()
    raw = pa_ref[:, 0:cw]
    y = raw * cw_ref[CONV_W - 1:CONV_W, :]
    for j in range(CONV_W - 1):
        y = y + conv_ref[:, j * cw:(j + 1) * cw] * cw_ref[j:j + 1, :]
    convn_ref[:, 0:(CONV_W - 2) * cw] = conv_ref[:, cw:(CONV_W - 1) * cw]
    convn_ref[:, (CONV_W - 2) * cw:(CONV_W - 1) * cw] = raw
    qkv = _silu(y)
    _put_heads(a_ref, 0, _l2norm_heads(qkv[:, 0:w], ones_blk) * HEAD_DIM ** -0.5)
    _put_heads(a_ref, 1, _l2norm_heads(qkv[:, w:2 * w], ones_blk))
    _put_heads(a_ref, 2, qkv[:, 2 * w:3 * w])
    g, beta = _gdn_prep(pa_ref[:, 4 * w:4 * w + LANE], alog_ref, dtb_ref)
    for h in range(N_HEADS):
        ag_ref[0, h] = g[:, h * HEAD_DIM:h * HEAD_DIM + 1]
        ag_ref[1, h] = beta[:, h * HEAD_DIM:h * HEAD_DIM + 1]
    for ref, (q, k, v, _, lf) in ((b_ref, _gla_prep(pb_ref[...], gkw_ref, gkb_ref)),
                                  (d_ref, _hgrn_prep(pd_ref[...], lb_ref))):
        for j, val in enumerate((q, k, v, lf)):
            _put_heads(ref, j, val)


def _decode_prep(pa, pb, pd, conv, params):
    rows = pa.shape[0]
    ins = (pa, pb, pd, conv) + tuple(params)
    per_head = lambda n, width: (n, N_HEADS, rows, width)
    shapes = [per_head(3, HEAD_DIM), per_head(2, 1), per_head(4, HEAD_DIM), per_head(4, HEAD_DIM), conv.shape]
    return pl.pallas_call(
        _decode_prep_kernel,
        grid=(1,),
        in_specs=[_full(a.shape) for a in ins],
        out_specs=[_full(s) for s in shapes],
        out_shape=[jax.ShapeDtypeStruct(s, F32) for s in shapes],
        compiler_params=_cparams("arbitrary"),
        name="decode_prep",
    )(*ins)


def _decode_post_kernel(oa_ref, ob_ref, od_ref, pa_ref, pb_ref, pd_ref, nwa_ref, nwb_ref, nwd_ref,
                        a_ref, b_ref, d_ref):
    w = BR_WIDTH
    ones_blk = _block_ones()
    heads = lambda ref: jnp.concatenate([ref[h] for h in range(N_HEADS)], axis=1)
    for out, o, p, nw in ((a_ref, oa_ref, pa_ref, nwa_ref), (b_ref, ob_ref, pb_ref, nwb_ref),
                          (d_ref, od_ref, pd_ref, nwd_ref)):
        out[...] = _head_norm_gate(heads(o), _silu(p[:, 3 * w:4 * w]), nw[...], ones_blk)


def _decode_post(oa, ob, od, pa, pb, pd, nwa, nwb, nwd):
    ins = (oa, ob, od, pa, pb, pd, nwa, nwb, nwd)
    shp = (oa.shape[1], BR_WIDTH)
    return pl.pallas_call(
        _decode_post_kernel,
        grid=(1,),
        in_specs=[_full(a.shape) for a in ins],
        out_specs=[_full(shp)] * 3,
        out_shape=[jax.ShapeDtypeStruct(shp, F32)] * 3,
        compiler_params=_cparams("arbitrary"),
        name="decode_post",
    )(*ins)


def _pack_w_in(w_in):
    sizes = (3 * BR_WIDTH, N_HEADS, N_HEADS, BR_WIDTH, BR_WIDTH, BR_WIDTH, BR_WIDTH, GLA_GATE_RANK, BR_WIDTH,
             BR_WIDTH, BR_WIDTH, BR_WIDTH, BR_WIDTH, BR_WIDTH, N_BRANCH * D_MODEL)
    w_in = w_in.astype(BF16)
    parts, off = [], 0
    for n in sizes:
        parts.append(w_in[..., off:off + n])
        off += n
    (a_qkv, a_alpha, a_beta, a_gate, b_q, b_k, b_v, b_gk, b_gate, c_u, d_q, d_f, d_i, d_gate, merge) = parts
    zpad = lambda n: jnp.zeros(w_in.shape[:-1] + (n,), BF16)
    mix = jnp.concatenate([a_qkv, a_gate, a_alpha, a_beta, zpad(LANE - 2 * N_HEADS),
                           b_q, b_k, b_v, b_gate, b_gk, zpad(LANE - GLA_GATE_RANK),
                           c_u, d_q, d_f, d_i, d_gate], axis=-1)
    return mix, merge


def _tile_sizes(seq):
    pick = lambda n: n if seq % n == 0 else seq
    return pick(512), pick(256), pick(128)


def _hgrn_lower_bounds(logits):
    p = jax.nn.softmax(logits, axis=0)
    return jnp.cumsum(p, axis=0) - p[0]


def _row(x):
    return x.reshape(1, -1)


def _rep_heads(x):
    return jnp.repeat(x, HEAD_DIM).reshape(1, BR_WIDTH)


def kernel(x_prompt, x_sample, state_gdn_conv, state_gdn, state_gla, state_s5_re, state_s5_im, state_hgrn, norm1_w, w_in, gdn_conv_w, gdn_a_log, gdn_dt_bias, gdn_norm_w, gla_gk_w, gla_gk_b, gla_norm_w, s5_lambda_re, s5_lambda_im, s5_b_re, s5_b_im, s5_c_re, s5_c_im, s5_d, s5_log_dt, s5_glu_w, hgrn_lb_logits, hgrn_norm_w, w_branch, w_out, norm2_w, ffn_w_gate, ffn_w_up, ffn_w_down, final_norm_w):
    bsz, seq, _ = x_prompt.shape
    dbs = x_sample.shape[0]
    tm_p, tt, s5_steps = _tile_sizes(seq)
    lb_all = _hgrn_lower_bounds(hgrn_lb_logits)
    w_mix_all, w_merge_all = _pack_w_in(w_in)
    gkw_all = jnp.concatenate([gla_gk_w, jnp.zeros((DEPTH, LANE - GLA_GATE_RANK, BR_WIDTH), F32)], axis=1)
    wb_all, wo_all = w_branch.astype(BF16), w_out.astype(BF16)
    wg_all, wu_all, wd_all = ffn_w_gate.astype(BF16), ffn_w_up.astype(BF16), ffn_w_down.astype(BF16)
    hp = x_prompt.reshape(bsz * seq, D_MODEL)
    hs = x_sample.reshape(dbs, D_MODEL)
    zeros_p = jnp.zeros((bsz, S5_STATE), F32)
    fw = _row(final_norm_w)
    outs_p = [[] for _ in range(6)]
    outs_s = [[] for _ in range(6)]
    for i in range(DEPTH):
        final = i == DEPTH - 1
        w_mix, w_merge, gkw = w_mix_all[i], w_merge_all[i], gkw_all[i]
        nw1, nw2 = _row(norm1_w[i]), _row(norm2_w[i])
        gdn_params = (gdn_conv_w[i], _rep_heads(gdn_a_log[i]), _rep_heads(gdn_dt_bias[i]), _row(gdn_norm_w[i]))
        gla_params = (gkw, _row(gla_gk_b[i]), _row(gla_norm_w[i]))
        hgrn_params = (_row(lb_all[i]), _row(hgrn_norm_w[i]))
        s5_params = _s5_params(s5_lambda_re[i], s5_lambda_im[i], s5_b_re[i], s5_b_im[i], s5_c_re[i], s5_c_im[i],
                               s5_d[i], s5_log_dt[i], s5_glu_w[i])
        prep_params = (gdn_conv_w[i], gdn_params[1], gdn_params[2], gkw, gla_params[1], hgrn_params[0])
        wb, wo, wg, wu, wd = wb_all[i], wo_all[i], wg_all[i], wu_all[i], wd_all[i]

        ga, gb, pc, gd, conv_p = _in_proj_prompt(hp, nw1, w_mix, prep_params, tm_p, seq)
        o_a, st_a = _mixer_call(_gdn_prompt_kernel, ga, bsz, seq, tt, "gdn_prompt")
        o_b, st_b = _mixer_call(_linear_prompt_kernel, gb, bsz, seq, tt, "gla_prompt")
        o_d, st_d = _mixer_call(_linear_prompt_kernel, gd, bsz, seq, tt, "hgrn_prompt")
        o_c, xr_p, xi_p = _s5(pc.reshape(seq * bsz, BR_WIDTH), zeros_p, zeros_p, s5_params, seq, bsz, s5_steps)
        hp = _merge(hp, o_a, o_b, o_c.reshape(seq, bsz * BR_WIDTH), o_d, nw1, w_merge, wb, wo, tm_p, seq,
                    prepared=(ga, gb, gd), gains=(gdn_params[3], gla_params[2], hgrn_params[1]))
        hp = _ffn(hp, nw2, wg, wu, wd, fw, tm_p, final)
        for lst, s in zip(outs_p, (conv_p, st_a, st_b, xr_p.reshape(bsz, S5_GROUPS, S5_P),
                                   xi_p.reshape(bsz, S5_GROUPS, S5_P), st_d)):
            lst.append(s)

        pa, pb, pc, pd = _in_proj(hs, nw1, w_mix, dbs, dbs)
        va, vg, vb, vd, conv_s = _decode_prep(
            pa, pb, pd, state_gdn_conv[i].reshape(dbs, (CONV_W - 1) * 3 * BR_WIDTH), prep_params)
        flat = lambda s: s.reshape(dbs, N_HEADS * HEAD_DIM * HEAD_DIM)
        oa_h, sa = _decode_call(_gdn_decode_kernel, [va, vg], flat(state_gdn[i]), "gdn_decode")
        ob_h, sb = _decode_call(_gla_decode_kernel, [vb], flat(state_gla[i]), "gla_decode")
        od_h, sd = _decode_call(_gla_decode_kernel, [vd], flat(state_hgrn[i]), "hgrn_decode")
        o_a, o_b, o_d = _decode_post(oa_h, ob_h, od_h, pa, pb, pd, gdn_params[3], gla_params[2], hgrn_params[1])
        o_c, xr_s, xi_s = _s5(pc, state_s5_re[i].reshape(dbs, S5_STATE), state_s5_im[i].reshape(dbs, S5_STATE),
                              s5_params, 1, dbs, 1)
        hs = _merge(hs, o_a, o_b, o_c, o_d, nw1, w_merge, wb, wo, dbs, dbs)
        hs = _ffn(hs, nw2, wg, wu, wd, fw, dbs, final)
        st5 = lambda s: s.reshape(dbs, N_HEADS, HEAD_DIM, HEAD_DIM)
        for lst, s in zip(outs_s, (conv_s.reshape(dbs, CONV_W - 1, 3 * BR_WIDTH), st5(sa), st5(sb),
                                   xr_s.reshape(dbs, S5_GROUPS, S5_P), xi_s.reshape(dbs, S5_GROUPS, S5_P), st5(sd))):
            lst.append(s)

    return (hp.reshape(bsz, seq, D_MODEL), hs.reshape(dbs, 1, D_MODEL),
            *[jnp.stack(l) for l in outs_p], *[jnp.stack(l) for l in outs_s])
```

```python
import functools
import math

import jax
import jax.numpy as jnp
from jax import lax
from jax.experimental import pallas as pl
from jax.experimental.pallas import tpu as pltpu

F32 = jnp.float32
BF16 = jnp.bfloat16

D_MODEL = 1024
DEPTH = 4
N_BRANCH = 4
BR_WIDTH = D_MODEL // N_BRANCH
HEAD_DIM = 64
N_HEADS = BR_WIDTH // HEAD_DIM
CONV_W = 4
CHUNK = 64
SUB = 16
GLA_GATE_RANK = 16
GLA_GATE_NORM = 16.0
S5_GROUP = 16
S5_GROUPS = BR_WIDTH // S5_GROUP
S5_P = 64
S5_STATE = S5_GROUPS * S5_P
D_FF = -(-8 * D_MODEL // (3 * 256)) * 256
FF_TILE = 256
S5_UNROLL = 8
DECODE_ROWS = 8
EPS = 1e-6
SEQ_PER_STEP = 4
STREAM_SKEW = 1
EXP_CAP = 60.0
LANE = 128
SEG_A = 3 * BR_WIDTH + BR_WIDTH + LANE
SEG_B = 4 * BR_WIDTH + LANE
SEG_C = BR_WIDTH
SEG_D = 4 * BR_WIDTH
SEG_OFF = (0, SEG_A, SEG_A + SEG_B, SEG_A + SEG_B + SEG_C, SEG_A + SEG_B + SEG_C + SEG_D)
N_MIX = SEG_OFF[-1]
PREP_A = 6 * BR_WIDTH
PREP_B = 5 * BR_WIDTH
VMEM_LIMIT = 56 * 1024 * 1024


def _cparams(*sem):
    return pltpu.CompilerParams(dimension_semantics=sem, vmem_limit_bytes=VMEM_LIMIT)


def _full(shape):
    n = len(shape)
    return pl.BlockSpec(shape, lambda *_: (0,) * n)


def _rms(x, w):
    return x * lax.rsqrt(jnp.mean(x * x, axis=-1, keepdims=True) + EPS) * w


def _sigmoid(x):
    return 1.0 / (1.0 + jnp.exp(-x))


def _silu(x):
    return x * _sigmoid(x)


def _softplus(x):
    return jnp.maximum(x, 0.0) + jnp.log(1.0 + jnp.exp(-jnp.abs(x)))


def _log_sigmoid(x):
    return -_softplus(-x)


def _gelu(x):
    return 0.5 * x * (1.0 + jnp.tanh(math.sqrt(2.0 / math.pi) * (x + 0.044715 * x * x * x)))


def _dot(a, b):
    return jnp.dot(a.astype(BF16), b.astype(BF16), preferred_element_type=F32)


def _dot_nt(a, b):
    return lax.dot_general(a.astype(BF16), b.astype(BF16), (((1,), (1,)), ((), ())), preferred_element_type=F32)


def _dot_tn(a, b):
    return lax.dot_general(a.astype(BF16), b.astype(BF16), (((0,), (0,)), ((), ())), preferred_element_type=F32)


def _split(x, terms):
    parts, rest = [], x
    for t in range(terms):
        p = rest.astype(BF16)
        parts.append(p)
        if t + 1 < terms:
            rest = rest - p.astype(F32)
    return parts


def _dot_xl(x, c, terms=3):
    return sum(jnp.dot(p, c, preferred_element_type=F32) for p in _split(x, terms))


def _dot_xr(c, x, terms=3):
    return sum(jnp.dot(c, p, preferred_element_type=F32) for p in _split(x, terms))


def _dot_nt_xl(x, c, terms=3):
    return sum(lax.dot_general(p, c, (((1,), (1,)), ((), ())), preferred_element_type=F32) for p in _split(x, terms))


def _iota2(shape, dim):
    return lax.broadcasted_iota(jnp.int32, shape, dim)


def _head_masks():
    r = _iota2((CHUNK, N_HEADS * CHUNK), 0)
    c = _iota2((CHUNK, N_HEADS * CHUNK), 1) & (CHUNK - 1)
    rr = _iota2((N_HEADS * CHUNK, N_HEADS * HEAD_DIM), 0)
    cc = _iota2((N_HEADS * CHUNK, N_HEADS * HEAD_DIM), 1)
    block = (rr >> 6) == (cc >> 6)
    one = lambda mask: jnp.where(mask, 1.0, 0.0)
    return dict(causal=r >= c, strict=r > c, eye=r == c, same_sub=(r >> 4) == (c >> 4), block=block,
                block16=one(block).astype(BF16), sub16=one((rr >> 4) == (cc >> 4)).astype(BF16),
                eye_sub=one(_iota2((SUB, BR_WIDTH), 0) == (_iota2((SUB, BR_WIDTH), 1) & (SUB - 1))))


def _chunk_tri(tt):
    r = _iota2((tt, tt), 0)
    c = _iota2((tt, tt), 1)
    return jnp.where((r >= c) & ((r >> 6) == (c >> 6)), 1.0, 0.0).astype(BF16)


def _chunk_ones(tt):
    r = _iota2((tt, tt), 0) >> 6
    c = _iota2((tt, tt), 1) >> 6
    return jnp.where(r == c, 1.0, 0.0).astype(BF16)


def _block_ones():
    return _chunk_ones(BR_WIDTH)


def _blockdiag(x, block16):
    return jnp.concatenate([x.astype(BF16)] * N_HEADS, axis=0) * block16


def _head_norm_gate(o, silu_gate, w, ones_blk):
    ms = _dot_xl(o * o, ones_blk, 2) * (1.0 / HEAD_DIM)
    return o * lax.rsqrt(ms + EPS) * w * silu_gate


def _in_proj_kernel(x_ref, nw_ref, w_ref, pa_ref, pb_ref, pc_ref, pd_ref):
    xn = _rms(x_ref[...], nw_ref[...]).astype(BF16)
    for i, ref in enumerate((pa_ref, pb_ref, pc_ref, pd_ref)):
        ref[...] = _dot(xn, w_ref[:, SEG_OFF[i]:SEG_OFF[i + 1]])


def _time_major_spec(tm, seq):
    nt = seq // tm
    return pl.BlockSpec((tm, BR_WIDTH), lambda i: (i % nt, i // nt))


def _in_proj_prompt_kernel(x_ref, halo_ref, nw_ref, w_ref, cw_ref, alog_ref, dtb_ref, gkw_ref, gkb_ref, lb_ref,
                           ga_ref, gb_ref, pc_ref, gd_ref, conv_ref, *, nt):
    w = BR_WIDTH
    seg = lambda i: w_ref[:, SEG_OFF[i]:SEG_OFF[i + 1]]
    xn = _rms(x_ref[...], nw_ref[...]).astype(BF16)
    pd = _dot(xn, seg(3))
    pa = _dot(xn, seg(0))
    hist = _dot(_rms(halo_ref[...], nw_ref[...]), w_ref[:, 0:3 * w])
    for j, val in enumerate(_hgrn_prep(pd, lb_ref)):
        gd_ref[:, j * w:(j + 1) * w] = val
    pb = _dot(xn, seg(1))
    raw = pa[:, 0:3 * w]
    hist = jnp.where((pl.program_id(0) % nt) == 0, 0.0, hist)
    rows = raw.shape[0]
    conv_ref[0] = raw[rows - (CONV_W - 1):rows, :]
    q, k, v = _gdn_conv_qkv(raw, hist, cw_ref)
    g, beta = _gdn_prep(pa[:, 4 * w:4 * w + LANE], alog_ref, dtb_ref)
    for j, val in enumerate((q, k, v, _silu(pa[:, 3 * w:4 * w]), g, beta)):
        ga_ref[:, j * w:(j + 1) * w] = val
    pc_ref[...] = _dot(xn, seg(2))
    for j, val in enumerate(_gla_prep(pb, gkw_ref, gkb_ref)):
        gb_ref[:, j * w:(j + 1) * w] = val


def _in_proj_prompt(x, nw, w, params, tm, seq):
    t = x.shape[0]
    nt = seq // tm
    bsz = t // seq
    row = lambda n: pl.BlockSpec((tm, n), lambda i: (i, 0))
    halo = pl.BlockSpec((8, D_MODEL), lambda i: (jnp.maximum(i * (tm // 8) - 1, 0), 0))
    return pl.pallas_call(
        functools.partial(_in_proj_prompt_kernel, nt=nt),
        grid=(t // tm,),
        in_specs=[row(D_MODEL), halo, _full((1, D_MODEL)), _full((D_MODEL, N_MIX))] + [_full(a.shape) for a in params],
        out_specs=[row(PREP_A), row(PREP_B), _time_major_spec(tm, seq), row(PREP_B),
                   pl.BlockSpec((1, CONV_W - 1, 3 * BR_WIDTH), lambda i: (i // nt, 0, 0))],
        out_shape=[jax.ShapeDtypeStruct((t, PREP_A), F32), jax.ShapeDtypeStruct((t, PREP_B), F32),
                   jax.ShapeDtypeStruct((seq, bsz * BR_WIDTH), F32), jax.ShapeDtypeStruct((t, PREP_B), F32),
                   jax.ShapeDtypeStruct((bsz, CONV_W - 1, 3 * BR_WIDTH), F32)],
        compiler_params=_cparams("arbitrary"),
        name="in_proj_prompt",
    )(x, x, nw, w, *params)


def _in_proj(x, nw, w, tm, seq):
    t = x.shape[0]
    widths = (SEG_A, SEG_B, SEG_C, SEG_D)
    out_specs = [pl.BlockSpec((tm, n), lambda i: (i, 0)) for n in widths]
    out_shape = [jax.ShapeDtypeStruct((t, n), F32) for n in widths]
    out_specs[2] = _time_major_spec(tm, seq)
    out_shape[2] = jax.ShapeDtypeStruct((seq, (t // seq) * BR_WIDTH), F32)
    return pl.pallas_call(
        _in_proj_kernel,
        grid=(t // tm,),
        in_specs=[pl.BlockSpec((tm, D_MODEL), lambda i: (i, 0)), _full((1, D_MODEL)), _full((D_MODEL, N_MIX))],
        out_specs=out_specs,
        out_shape=out_shape,
        compiler_params=_cparams("parallel"),
        name="in_proj",
    )(x, nw, w)


def _merge_kernel(h_ref, oa_ref, ob_ref, oc_ref, od_ref, *rest, head_norm):
    if head_norm:
        gates, gains, rest = rest[0:3], rest[3:6], rest[6:]
    nw_ref, wm_ref, wb_ref, wo_ref, out_ref = rest
    h = h_ref[...]
    xn = _rms(h, nw_ref[...]).astype(BF16)
    o_vals = [oa_ref[...], ob_ref[...], oc_ref[...], od_ref[...]]
    if head_norm:
        ones_blk = _block_ones()
        for k, g_ref, w_ref in zip((0, 1, 3), gates, gains):
            o_vals[k] = _head_norm_gate(o_vals[k], g_ref[...], w_ref[...], ones_blk)
    proj = lambda k: (_dot(xn, wm_ref[:, k * D_MODEL:(k + 1) * D_MODEL]), _dot(o_vals[k], wb_ref[k]))
    acc = jnp.zeros(h.shape, F32)
    nxt = proj(0)
    for k in range(N_BRANCH):
        gate, br = nxt
        if k + 1 < N_BRANCH:
            nxt = proj(k + 1)
        acc = acc + _sigmoid(gate) * br
    out_ref[...] = h + _dot(acc, wo_ref[...])


def _merge(h, oa, ob, oc, od, nw, wm, wb, wo, tm, seq, prepared=None, gains=None):
    t = h.shape[0]
    row = lambda n: pl.BlockSpec((tm, n), lambda i: (i, 0))
    head_norm = prepared is not None
    extra, extra_specs = (), []
    if head_norm:
        extra = tuple(prepared) + tuple(gains)
        extra_specs = [pl.BlockSpec((tm, BR_WIDTH), lambda i: (i, 3))] * 3 + [_full((1, BR_WIDTH))] * 3
    return pl.pallas_call(
        functools.partial(_merge_kernel, head_norm=head_norm),
        grid=(t // tm,),
        in_specs=[row(D_MODEL), row(BR_WIDTH), row(BR_WIDTH), _time_major_spec(tm, seq), row(BR_WIDTH)]
                 + extra_specs + [_full((1, D_MODEL)), _full((D_MODEL, N_BRANCH * D_MODEL)),
                                  _full((N_BRANCH, BR_WIDTH, D_MODEL)), _full((D_MODEL, D_MODEL))],
        out_specs=row(D_MODEL),
        out_shape=jax.ShapeDtypeStruct((t, D_MODEL), F32),
        compiler_params=_cparams("parallel"),
        name="merge",
    )(h, oa, ob, oc, od, *extra, nw, wm, wb, wo)


def _ffn_kernel(h_ref, nw_ref, wg_ref, wu_ref, wd_ref, fw_ref, out_ref, *, final):
    h = h_ref[...]
    hn = _rms(h, nw_ref[...]).astype(BF16)
    n_tiles = D_FF // FF_TILE
    cols = [slice(c * FF_TILE, (c + 1) * FF_TILE) for c in range(n_tiles)]
    up = lambda sl: (_dot(hn, wg_ref[:, sl]), _dot(hn, wu_ref[:, sl]))
    acc = jnp.zeros(h.shape, F32)
    nxt = up(cols[0])
    for c in range(n_tiles):
        g, u = nxt
        if c + 1 < n_tiles:
            nxt = up(cols[c + 1])
        acc = acc + _dot(_silu(g) * u, wd_ref[cols[c], :])
    hnew = h + acc
    out_ref[...] = _rms(hnew, fw_ref[...]) if final else hnew


def _ffn(h, nw, wg, wu, wd, fw, tm, final):
    t = h.shape[0]
    row = pl.BlockSpec((tm, D_MODEL), lambda i: (i, 0))
    return pl.pallas_call(
        functools.partial(_ffn_kernel, final=final),
        grid=(t // tm,),
        in_specs=[row, _full((1, D_MODEL)), _full((D_MODEL, D_FF)), _full((D_MODEL, D_FF)),
                  _full((D_FF, D_MODEL)), _full((1, D_MODEL))],
        out_specs=row,
        out_shape=jax.ShapeDtypeStruct((t, D_MODEL), F32),
        compiler_params=_cparams("parallel"),
        name="ffn",
    )(h, nw, wg, wu, wd, fw)


def _run_skewed(streams, skew):
    results = [None] * len(streams)
    live, step = [], 0
    pending = list(enumerate(streams))
    while pending or live:
        while pending and pending[0][0] * skew <= step:
            live.append(pending.pop(0))
        still = []
        for idx, gen in live:
            try:
                next(gen)
                still.append((idx, gen))
            except StopIteration as done:
                results[idx] = done.value
        live = still
        step += 1
    return results


def _gla_stream(load, st, m, tt):
    each = lambda f, *xs: [f(*a) for a in zip(*xs)]
    blk = m["block16"]
    b_all = _dot_xr(_chunk_tri(tt), load(4, slice(0, tt)), 3)
    yield
    rows = [slice(c * CHUNK, (c + 1) * CHUNK) for c in range(tt // CHUNK)]
    q, k, v = ([load(j, r) for r in rows] for j in range(3))
    b = [b_all[r, :] for r in rows]
    b_last = each(lambda x: x[CHUNK - 1:CHUNK, :], b)
    scores = []
    for i0 in range(0, CHUNK, SUB):
        q_i = each(lambda qq, x: qq[i0:i0 + SUB, :] * jnp.exp(x[i0:i0 + SUB, :] - x[i0:i0 + 1, :]), q, b)
        k_i = each(lambda kk, x: kk * jnp.exp(jnp.minimum(x[i0:i0 + 1, :] - x, EXP_CAP)), k, b)
        yield
        scores.append(each(lambda a, c: _dot_nt(a, _blockdiag(c, blk)), q_i, k_i))
    kv = each(lambda vv, kk, x, xl: jnp.where(m["block"], _dot_tn(vv, kk * jnp.exp(xl - x)), 0.0), v, k, b, b_last)
    yield
    attn = each(lambda *r: jnp.where(m["causal"], jnp.concatenate(r, axis=0), 0.0), *scores)
    qe = each(lambda qq, x: qq * jnp.exp(x), q, b)
    yield
    o_loc = each(lambda a, vv: _dot(a, _blockdiag(vv, blk)), attn, v)
    yield
    outs = []
    for c in range(len(q)):
        outs.append(_dot_nt(qe[c], st) + o_loc[c])
        st = st * jnp.exp(b_last[c]) + kv[c]
        yield
    return outs, st


def _write_state(out_ref, i, st):
    r = _iota2((BR_WIDTH, BR_WIDTH), 0)
    c = _iota2((BR_WIDTH, BR_WIDTH), 1)
    eye = jnp.where(r == c, 1.0, 0.0).astype(BF16)
    nt_dot = lambda p: lax.dot_general(eye, p, (((1,), (1,)), ((), ())), preferred_element_type=F32)
    st_t = sum(nt_dot(p) for p in _split(st, 3))
    for h in range(N_HEADS):
        out_ref[i, h] = st_t[h * HEAD_DIM:(h + 1) * HEAD_DIM, h * HEAD_DIM:(h + 1) * HEAD_DIM]


def _gla_prep(pb, gkw_ref, gkb_ref):
    w = BR_WIDTH
    lf = _log_sigmoid(_dot(pb[:, 4 * w:4 * w + LANE], gkw_ref[...]) + gkb_ref[...]) * (1.0 / GLA_GATE_NORM)
    return pb[:, 0:w] * HEAD_DIM ** -0.5, pb[:, w:2 * w], pb[:, 2 * w:3 * w], _silu(pb[:, 3 * w:4 * w]), lf


def _hgrn_prep(pd, lb_ref):
    w = BR_WIDTH
    lb = lb_ref[...]
    xf = pd[:, w:2 * w]
    return (_silu(pd[:, 0:w]) * HEAD_DIM ** -0.5, (1.0 - lb) * _sigmoid(-xf), pd[:, 2 * w:3 * w],
            _silu(pd[:, 3 * w:4 * w]), _hgrn_log_forget(xf, lb))


def _linear_prompt_kernel(p_ref, o_ref, sout_ref, st_s, *, tt):
    @pl.when(pl.program_id(1) == 0)
    def _():
        st_s[...] = jnp.zeros(st_s.shape, F32)

    w = BR_WIDTH
    m = _head_masks()
    n_seq = p_ref.shape[0]
    load = lambda i: (lambda j, r: p_ref[i, r, j * w:(j + 1) * w])
    results = _run_skewed([_gla_stream(load(i), st_s[i], m, tt) for i in range(n_seq)], STREAM_SKEW)
    for i, (outs, st) in enumerate(results):
        st_s[i] = st
        o_ref[i] = jnp.concatenate(outs, axis=0)

    @pl.when(pl.program_id(1) == pl.num_programs(1) - 1)
    def _():
        for i in range(n_seq):
            _write_state(sout_ref, i, st_s[i])


def _hgrn_log_forget(xf, lb):
    ls_pos = _log_sigmoid(xf)
    pos = lb > 0
    lb_safe = jnp.where(pos, lb, 1.0)
    mixed = ls_pos + _softplus(jnp.log(lb_safe) - xf)
    return jnp.where(pos, mixed, ls_pos)


def _mixer_call(kernel, p, bsz, seq, tt, name):
    nt = seq // tt
    width = p.shape[1]
    n_seq = SEQ_PER_STEP if bsz % SEQ_PER_STEP == 0 else 1
    o, st = pl.pallas_call(
        functools.partial(kernel, tt=tt),
        grid=(bsz // n_seq, nt),
        in_specs=[pl.BlockSpec((n_seq, tt, width), lambda b, t: (b, t, 0))],
        out_specs=[pl.BlockSpec((n_seq, tt, BR_WIDTH), lambda b, t: (b, t, 0)),
                   pl.BlockSpec((n_seq, N_HEADS, HEAD_DIM, HEAD_DIM), lambda b, t: (b, 0, 0, 0))],
        out_shape=[jax.ShapeDtypeStruct((bsz, seq, BR_WIDTH), F32),
                   jax.ShapeDtypeStruct((bsz, N_HEADS, HEAD_DIM, HEAD_DIM), F32)],
        scratch_shapes=[pltpu.VMEM((n_seq, BR_WIDTH, BR_WIDTH), F32)],
        compiler_params=_cparams("parallel", "arbitrary"),
        name=name,
    )(p.reshape(bsz, seq, width))
    return o.reshape(bsz * seq, BR_WIDTH), st


def _mm_split(a, b, tile_mask):
    a_hi, a_lo = _split(a, 2)
    b_hi, b_lo = (jnp.concatenate([p] * (BR_WIDTH // b.shape[0]), axis=0) * tile_mask for p in _split(b, 2))
    dot = functools.partial(jnp.dot, preferred_element_type=F32)
    return dot(a_hi, b_hi) + (dot(a_lo, b_hi) + dot(a_hi, b_lo))


def _unit_lower_inverse(n, m):
    each = lambda f, *xs: [f(*a) for a in zip(*xs)]
    eye_c, sub, blk = m["eye_sub"], m["sub16"], m["block16"]
    nd = each(lambda a: jnp.where(m["same_sub"], a, 0.0), n)
    low = each(lambda a, d: a - d, n, nd)
    c = each(lambda d: d[0:SUB] + d[SUB:2 * SUB] + d[2 * SUB:3 * SUB] + d[3 * SUB:4 * SUB], nd)
    p = each(lambda a: _mm_split(a, a, sub), c)
    x = each(lambda a: eye_c - a, c)
    yield
    for _ in range(2):
        xp = each(lambda a, b: _mm_split(jnp.concatenate([a, b], axis=0), b, sub), x, p)
        x = each(lambda a, b: a + b[0:SUB], x, xp)
        p = each(lambda b: b[SUB:2 * SUB], xp)
        yield
    x = each(lambda a, b: a + _mm_split(a, b, sub), x, p)
    dinv = each(lambda a: jnp.where(m["same_sub"], jnp.concatenate([a] * (CHUNK // SUB), axis=0), 0.0), x)
    yield
    mm = each(lambda a, b: _mm_split(a, b, blk), dinv, low)
    yield
    m2 = each(lambda a: _mm_split(a, a, blk), mm)
    yield
    e = each(lambda d, a: d + _mm_split(a, d, blk), dinv, m2)
    yield
    return each(lambda a, b: b - _mm_split(a, b, blk), mm, e)


def _gdn_stream(load, st, m, tt):
    each = lambda f, *xs: [f(*a) for a in zip(*xs)]
    w = BR_WIDTH
    blk = m["block16"]
    gam_all = _dot_xr(_chunk_tri(tt), load(4, slice(0, tt)), 3)
    yield
    lane_j = _iota2((tt, w), 1) & (CHUNK - 1)
    row_j = _iota2((tt, w), 0) & (CHUNK - 1)
    gam_t = _dot_xr(_chunk_ones(tt), jnp.where(lane_j == row_j, gam_all, 0.0), 3)
    yield
    rows = [slice(c * CHUNK, (c + 1) * CHUNK) for c in range(tt // CHUNK)]
    q, k, v, bexp = ([load(j, r) for r in rows] for j in (0, 1, 2, 5))
    gam_i = [gam_all[r, :] for r in rows]
    gam_j = [gam_t[r, :] for r in rows]
    decay = each(lambda gi, gj: jnp.where(m["causal"], jnp.exp(jnp.minimum(gi - gj, 0.0)), 0.0), gam_i, gam_j)
    scores = each(lambda kk, qq: _dot_nt(jnp.concatenate([kk, qq], axis=0), _blockdiag(kk, blk)), k, q)
    yield
    attn = each(lambda s, d: s[CHUNK:2 * CHUNK] * d, scores, decay)
    n = each(lambda b, s, d: jnp.where(m["strict"], b * s[0:CHUNK] * d, 0.0), bexp, scores, decay)
    tinv = yield from _unit_lower_inverse(n, m)
    yield
    u_v = each(lambda t, b, vv: _dot(t, _blockdiag(b * vv, blk)), tinv, bexp, v)
    w_k = each(lambda t, b, gi, kk: _dot(t, _blockdiag(b * jnp.exp(gi) * kk, blk)), tinv, bexp, gam_i, k)
    yield
    outs = []
    for c in range(len(rows)):
        gam_last = gam_i[c][CHUNK - 1:CHUNK, :]
        ws = _dot_nt(jnp.concatenate([w_k[c], q[c]], axis=0), st)
        u = u_v[c] - ws[0:CHUNK]
        yield
        outs.append(jnp.exp(gam_i[c]) * ws[CHUNK:2 * CHUNK] + _dot(attn[c], _blockdiag(u, blk)))
        kd = k[c] * jnp.exp(gam_last - gam_i[c])
        st = st * jnp.exp(gam_last) + jnp.where(m["block"], _dot_tn(u, kd), 0.0)
        yield
    return outs, st


def _l2norm_heads(x, ones_blk):
    return x * lax.rsqrt(_dot_xl(x * x, ones_blk, 2) + EPS)


def _gdn_prep(a_alpha_beta, alog_ref, dtb_ref):
    r = _iota2((LANE, BR_WIDTH), 0)
    c = _iota2((LANE, BR_WIDTH), 1) >> 6
    sel_a = jnp.where(r == c, 1.0, 0.0).astype(BF16)
    sel_b = jnp.where(r == c + N_HEADS, 1.0, 0.0).astype(BF16)
    alpha = _dot_xl(a_alpha_beta, sel_a, 3)
    beta = _sigmoid(_dot_xl(a_alpha_beta, sel_b, 3))
    g = -jnp.exp(alog_ref[...]) * _softplus(alpha + dtb_ref[...])
    return g, beta


def _gdn_conv_qkv(raw, hist, cw_ref):
    w = BR_WIDTH
    rows = raw.shape[0]
    xp = jnp.concatenate([hist, raw], axis=0)
    y = raw * cw_ref[CONV_W - 1:CONV_W, :]
    for j in range(CONV_W - 1):
        start = 8 - (CONV_W - 1) + j
        y = y + xp[start:start + rows, :] * cw_ref[j:j + 1, :]
    qkv = _silu(y)
    ones_blk = _block_ones()
    return (_l2norm_heads(qkv[:, 0:w], ones_blk) * HEAD_DIM ** -0.5, _l2norm_heads(qkv[:, w:2 * w], ones_blk),
            qkv[:, 2 * w:3 * w])


def _gdn_prompt_kernel(p_ref, o_ref, sout_ref, st_s, *, tt):
    t = pl.program_id(1)

    @pl.when(t == 0)
    def _():
        st_s[...] = jnp.zeros(st_s.shape, F32)

    w = BR_WIDTH
    m = _head_masks()
    n_seq = p_ref.shape[0]
    load = lambda i: (lambda j, r: p_ref[i, r, j * w:(j + 1) * w])
    results = _run_skewed([_gdn_stream(load(i), st_s[i], m, tt) for i in range(n_seq)], STREAM_SKEW)
    for i, (outs, st) in enumerate(results):
        st_s[i] = st
        o_ref[i] = jnp.concatenate(outs, axis=0)

    @pl.when(t == pl.num_programs(1) - 1)
    def _():
        for i in range(n_seq):
            _write_state(sout_ref, i, st_s[i])


def _s5_kernel(u_ref, x0r_ref, x0i_ref, ar_ref, ai_ref, br_ref, bi_ref, cr_ref, ci_ref, d_ref, glu_ref,
               o_ref, xr_ref, xi_ref, sr_s, si_s, *, steps, rows):
    @pl.when(pl.program_id(0) == 0)
    def _():
        xr_ref[...] = x0r_ref[...]
        xi_ref[...] = x0i_ref[...]

    u = u_ref[...]
    ub = u.astype(BF16)
    sr_s[...] = _dot(ub, br_ref[...])
    si_s[...] = _dot(ub, bi_ref[...])
    a_re = jnp.broadcast_to(ar_ref[...], (rows, S5_STATE))
    a_im = jnp.broadcast_to(ai_ref[...], (rows, S5_STATE))

    def body(t, carry):
        xr, xi = carry
        r = pl.ds(pl.multiple_of(t * rows, rows), rows)
        nr = (sr_s[r, :] + a_re * xr) - a_im * xi
        ni = (si_s[r, :] + a_re * xi) + a_im * xr
        sr_s[r, :] = nr
        si_s[r, :] = ni
        return nr, ni

    xr, xi = lax.fori_loop(0, steps, body, (xr_ref[...], xi_ref[...]), unroll=min(steps, S5_UNROLL))
    xr_ref[...] = xr
    xi_ref[...] = xi
    y = _dot(sr_s[...].astype(BF16), cr_ref[...]) - _dot(si_s[...].astype(BF16), ci_ref[...]) + d_ref[...] * u
    hg = _dot(_gelu(y).astype(BF16), glu_ref[...])
    o_ref[...] = hg[:, 0:BR_WIDTH] * _sigmoid(hg[:, BR_WIDTH:2 * BR_WIDTH])


def _s5(u, x0r, x0i, params, steps_total, rows, steps):
    nt = steps_total // steps
    tile = steps * rows
    return pl.pallas_call(
        functools.partial(_s5_kernel, steps=steps, rows=rows),
        grid=(nt,),
        in_specs=[pl.BlockSpec((tile, BR_WIDTH), lambda t: (t, 0)), _full(x0r.shape), _full(x0i.shape)]
                 + [_full(a.shape) for a in params],
        out_specs=[pl.BlockSpec((tile, BR_WIDTH), lambda t: (t, 0)), _full(x0r.shape), _full(x0i.shape)],
        out_shape=[jax.ShapeDtypeStruct((steps_total * rows, BR_WIDTH), F32),
                   jax.ShapeDtypeStruct(x0r.shape, F32), jax.ShapeDtypeStruct(x0i.shape, F32)],
        scratch_shapes=[pltpu.VMEM((tile, S5_STATE), F32)] * 2,
        compiler_params=_cparams("arbitrary"),
        name="s5",
    )(u, x0r, x0i, *params)


def _s5_params(lam_re, lam_im, b_re, b_im, c_re, c_im, d, log_dt, glu_w):
    lr = jnp.minimum(lam_re, -1e-4)
    li = lam_im
    dt = jnp.exp(log_dt)[:, None]
    mag = jnp.exp(lr * dt)
    ab_re, ab_im = mag * jnp.cos(li * dt), mag * jnp.sin(li * dt)
    den = lr * lr + li * li
    z_re = ((ab_re - 1.0) * lr + ab_im * li) / den
    z_im = (ab_im * lr - (ab_re - 1.0) * li) / den
    bb_re = z_re[..., None] * b_re - z_im[..., None] * b_im
    bb_im = z_re[..., None] * b_im + z_im[..., None] * b_re
    eye = jnp.eye(S5_GROUPS, dtype=F32)

    def pack_in(bb):
        return jnp.einsum('gph,gk->ghkp', bb, eye).reshape(BR_WIDTH, S5_STATE).astype(BF16)

    def pack_out(c):
        return jnp.einsum('ghp,gk->gpkh', c, eye).reshape(S5_STATE, BR_WIDTH).astype(BF16)

    return (ab_re.reshape(1, S5_STATE), ab_im.reshape(1, S5_STATE), pack_in(bb_re), pack_in(bb_im),
            pack_out(c_re), pack_out(c_im), d.reshape(1, BR_WIDTH), glu_w.astype(BF16))


def _decode_views(r_ref, c_ref, s_ref):
    pairs = [(b, h) for b in range(s_ref.shape[0]) for h in range(N_HEADS)]
    each = lambda f: [f(b, h) for b, h in pairs]
    row = lambda j, b, h: r_ref[j, b:b + 1, h * HEAD_DIM:(h + 1) * HEAD_DIM]
    col = lambda j, b, h: c_ref[0, j, h * HEAD_DIM:(h + 1) * HEAD_DIM, b:b + 1]
    square = lambda x: jnp.broadcast_to(x, (HEAD_DIM, HEAD_DIM))
    return pairs, each, row, col, square


def _gla_decode_kernel(r_ref, c_ref, s_ref, o_ref, sn_ref):
    pairs, each, row, col, square = _decode_views(r_ref, c_ref, s_ref)
    s = each(lambda b, h: s_ref[b, h])
    qe, k, dec = (each(lambda b, h: square(col(j, b, h))) for j in range(3))
    v = each(lambda b, h: row(2, b, h))
    qk = each(lambda b, h: jnp.sum(row(0, b, h) * row(1, b, h), axis=1, keepdims=True))
    o = [jnp.sum(a * x, axis=0, keepdims=True) + c * y for a, x, c, y in zip(qe, s, qk, v)]
    sn = [d * x + a * y for d, x, a, y in zip(dec, s, k, v)]
    for (b, h), o_bh, sn_bh in zip(pairs, o, sn):
        o_ref[b:b + 1, h * HEAD_DIM:(h + 1) * HEAD_DIM] = o_bh
        sn_ref[b, h] = sn_bh


def _gdn_decode_kernel(r_ref, c_ref, s_ref, o_ref, sn_ref):
    pairs, each, row, col, square = _decode_views(r_ref, c_ref, s_ref)
    s = each(lambda b, h: s_ref[b, h])
    q, k = (each(lambda b, h: square(col(j, b, h))) for j in range(2))
    v = each(lambda b, h: row(2, b, h))
    eg, beta = (each(lambda b, h: r_ref[j, b:b + 1, h * HEAD_DIM:h * HEAD_DIM + 1]) for j in (3, 4))
    qk = each(lambda b, h: jnp.sum(row(0, b, h) * row(1, b, h), axis=1, keepdims=True))
    ks = [jnp.sum(a * x, axis=0, keepdims=True) for a, x in zip(k, s)]
    qs = [jnp.sum(a * x, axis=0, keepdims=True) for a, x in zip(q, s)]
    u = [bt * y - (bt * e) * z for bt, y, e, z in zip(beta, v, eg, ks)]
    o = [e * z + c * y for e, z, c, y in zip(eg, qs, qk, u)]
    sn = [e * x + a * y for e, x, a, y in zip(eg, s, k, u)]
    for (b, h), o_bh, sn_bh in zip(pairs, o, sn):
        o_ref[b:b + 1, h * HEAD_DIM:(h + 1) * HEAD_DIM] = o_bh
        sn_ref[b, h] = sn_bh


def _decode_call(kernel, row_pack, col_pack, state, name):
    rows = state.shape[0]
    blk = col_pack.shape[3]
    st_spec = pl.BlockSpec((blk, N_HEADS, HEAD_DIM, HEAD_DIM), lambda i: (i, 0, 0, 0))
    return pl.pallas_call(
        kernel,
        grid=(rows // blk,),
        in_specs=[pl.BlockSpec((row_pack.shape[0], blk, BR_WIDTH), lambda i: (0, i, 0)),
                  pl.BlockSpec((1,) + col_pack.shape[1:], lambda i: (i, 0, 0, 0)), st_spec],
        out_specs=[pl.BlockSpec((blk, BR_WIDTH), lambda i: (i, 0)), st_spec],
        out_shape=[jax.ShapeDtypeStruct((rows, BR_WIDTH), F32), jax.ShapeDtypeStruct(state.shape, F32)],
        compiler_params=_cparams("parallel"),
        name=name,
    )(row_pack, col_pack, state)


def _put_columns(ref, j, x, eye):
    nt_dot = lambda p: lax.dot_general(eye, p, (((1,), (1,)), ((), ())), preferred_element_type=F32)
    x_t = sum(nt_dot(p) for p in _split(x, 3))
    for i in range(ref.shape[0]):
        ref[i, j] = x_t[:, i * DECODE_ROWS:(i + 1) * DECODE_ROWS]


def _decode_prep_kernel(pa_ref, pb_ref, pd_ref, conv_ref, cw_ref, alog_ref, dtb_ref, gkw_ref, gkb_ref, lb_ref,
                        ar_ref, ac_ref, br_ref, bc_ref, dr_ref, dc_ref, convn_ref):
    w = BR_WIDTH
    cw = 3 * w
    ones_blk = _block_ones()
    eye = jnp.where(_iota2((w, w), 0) == _iota2((w, w), 1), 1.0, 0.0).astype(BF16)
    raw = pa_ref[:, 0:cw]
    y = raw * cw_ref[CONV_W - 1:CONV_W, :]
    for j in range(CONV_W - 1):
        y = y + conv_ref[:, j * cw:(j + 1) * cw] * cw_ref[j:j + 1, :]
    convn_ref[:, 0:(CONV_W - 2) * cw] = conv_ref[:, cw:(CONV_W - 1) * cw]
    convn_ref[:, (CONV_W - 2) * cw:(CONV_W - 1) * cw] = raw
    qkv = _silu(y)
    q = _l2norm_heads(qkv[:, 0:w], ones_blk) * HEAD_DIM ** -0.5
    k = _l2norm_heads(qkv[:, w:2 * w], ones_blk)
    g, beta = _gdn_prep(pa_ref[:, 4 * w:4 * w + LANE], alog_ref, dtb_ref)
    for j, val in enumerate((q, k, qkv[:, 2 * w:3 * w], jnp.exp(g), beta)):
        ar_ref[j] = val
    for j, val in enumerate((q, k)):
        _put_columns(ac_ref, j, val, eye)
    for rows_ref, cols_ref, (q, k, v, _, lf) in ((br_ref, bc_ref, _gla_prep(pb_ref[...], gkw_ref, gkb_ref)),
                                                 (dr_ref, dc_ref, _hgrn_prep(pd_ref[...], lb_ref))):
        dec = jnp.exp(lf)
        for j, val in enumerate((q, k, v)):
            rows_ref[j] = val
        for j, val in enumerate((q * dec, k, dec)):
            _put_columns(cols_ref, j, val, eye)


def _decode_prep(pa, pb, pd, conv, params):
    rows = pa.shape[0]
    ins = (pa, pb, pd, conv) + tuple(params)
    row_pack = lambda n: (n, rows, BR_WIDTH)
    col_pack = lambda n: (rows // DECODE_ROWS, n, BR_WIDTH, DECODE_ROWS)
    shapes = [row_pack(5), col_pack(2), row_pack(3), col_pack(3), row_pack(3), col_pack(3), conv.shape]
    return pl.pallas_call(
        _decode_prep_kernel,
        grid=(1,),
        in_specs=[_full(a.shape) for a in ins],
        out_specs=[_full(s) for s in shapes],
        out_shape=[jax.ShapeDtypeStruct(s, F32) for s in shapes],
        compiler_params=_cparams("arbitrary"),
        name="decode_prep",
    )(*ins)


def _decode_post_kernel(oa_ref, ob_ref, od_ref, pa_ref, pb_ref, pd_ref, nwa_ref, nwb_ref, nwd_ref,
                        a_ref, b_ref, d_ref):
    w = BR_WIDTH
    ones_blk = _block_ones()
    for out, o, p, nw in ((a_ref, oa_ref, pa_ref, nwa_ref), (b_ref, ob_ref, pb_ref, nwb_ref),
                          (d_ref, od_ref, pd_ref, nwd_ref)):
        out[...] = _head_norm_gate(o[...], _silu(p[:, 3 * w:4 * w]), nw[...], ones_blk)


def _decode_post(oa, ob, od, pa, pb, pd, nwa, nwb, nwd):
    ins = (oa, ob, od, pa, pb, pd, nwa, nwb, nwd)
    return pl.pallas_call(
        _decode_post_kernel,
        grid=(1,),
        in_specs=[_full(a.shape) for a in ins],
        out_specs=[_full(oa.shape)] * 3,
        out_shape=[jax.ShapeDtypeStruct(oa.shape, F32)] * 3,
        compiler_params=_cparams("arbitrary"),
        name="decode_post",
    )(*ins)


def _pack_w_in(w_in):
    sizes = (3 * BR_WIDTH, N_HEADS, N_HEADS, BR_WIDTH, BR_WIDTH, BR_WIDTH, BR_WIDTH, GLA_GATE_RANK, BR_WIDTH,
             BR_WIDTH, BR_WIDTH, BR_WIDTH, BR_WIDTH, BR_WIDTH, N_BRANCH * D_MODEL)
    w_in = w_in.astype(BF16)
    parts, off = [], 0
    for n in sizes:
        parts.append(w_in[..., off:off + n])
        off += n
    (a_qkv, a_alpha, a_beta, a_gate, b_q, b_k, b_v, b_gk, b_gate, c_u, d_q, d_f, d_i, d_gate, merge) = parts
    zpad = lambda n: jnp.zeros(w_in.shape[:-1] + (n,), BF16)
    mix = jnp.concatenate([a_qkv, a_gate, a_alpha, a_beta, zpad(LANE - 2 * N_HEADS),
                           b_q, b_k, b_v, b_gate, b_gk, zpad(LANE - GLA_GATE_RANK),
                           c_u, d_q, d_f, d_i, d_gate], axis=-1)
    return mix, merge


def _tile_sizes(seq):
    pick = lambda n: n if seq % n == 0 else seq
    return pick(512), pick(256), pick(128)


def _hgrn_lower_bounds(logits):
    p = jax.nn.softmax(logits, axis=0)
    return jnp.cumsum(p, axis=0) - p[0]


def _row(x):
    return x.reshape(1, -1)


def _rep_heads(x):
    return jnp.repeat(x, HEAD_DIM).reshape(1, BR_WIDTH)


def kernel(x_prompt, x_sample, state_gdn_conv, state_gdn, state_gla, state_s5_re, state_s5_im, state_hgrn, norm1_w, w_in, gdn_conv_w, gdn_a_log, gdn_dt_bias, gdn_norm_w, gla_gk_w, gla_gk_b, gla_norm_w, s5_lambda_re, s5_lambda_im, s5_b_re, s5_b_im, s5_c_re, s5_c_im, s5_d, s5_log_dt, s5_glu_w, hgrn_lb_logits, hgrn_norm_w, w_branch, w_out, norm2_w, ffn_w_gate, ffn_w_up, ffn_w_down, final_norm_w):
    bsz, seq, _ = x_prompt.shape
    dbs = x_sample.shape[0]
    tm_p, tt, s5_steps = _tile_sizes(seq)
    lb_all = _hgrn_lower_bounds(hgrn_lb_logits)
    w_mix_all, w_merge_all = _pack_w_in(w_in)
    gkw_all = jnp.concatenate([gla_gk_w, jnp.zeros((DEPTH, LANE - GLA_GATE_RANK, BR_WIDTH), F32)], axis=1)
    wb_all, wo_all = w_branch.astype(BF16), w_out.astype(BF16)
    wg_all, wu_all, wd_all = ffn_w_gate.astype(BF16), ffn_w_up.astype(BF16), ffn_w_down.astype(BF16)
    hp = x_prompt.reshape(bsz * seq, D_MODEL)
    hs = x_sample.reshape(dbs, D_MODEL)
    zeros_p = jnp.zeros((bsz, S5_STATE), F32)
    fw = _row(final_norm_w)
    outs_p = [[] for _ in range(6)]
    outs_s = [[] for _ in range(6)]
    for i in range(DEPTH):
        final = i == DEPTH - 1
        w_mix, w_merge, gkw = w_mix_all[i], w_merge_all[i], gkw_all[i]
        nw1, nw2 = _row(norm1_w[i]), _row(norm2_w[i])
        gdn_params = (gdn_conv_w[i], _rep_heads(gdn_a_log[i]), _rep_heads(gdn_dt_bias[i]), _row(gdn_norm_w[i]))
        gla_params = (gkw, _row(gla_gk_b[i]), _row(gla_norm_w[i]))
        hgrn_params = (_row(lb_all[i]), _row(hgrn_norm_w[i]))
        s5_params = _s5_params(s5_lambda_re[i], s5_lambda_im[i], s5_b_re[i], s5_b_im[i], s5_c_re[i], s5_c_im[i],
                               s5_d[i], s5_log_dt[i], s5_glu_w[i])
        prep_params = (gdn_conv_w[i], gdn_params[1], gdn_params[2], gkw, gla_params[1], hgrn_params[0])
        wb, wo, wg, wu, wd = wb_all[i], wo_all[i], wg_all[i], wu_all[i], wd_all[i]

        ga, gb, pc, gd, conv_p = _in_proj_prompt(hp, nw1, w_mix, prep_params, tm_p, seq)
        o_a, st_a = _mixer_call(_gdn_prompt_kernel, ga, bsz, seq, tt, "gdn_prompt")
        o_b, st_b = _mixer_call(_linear_prompt_kernel, gb, bsz, seq, tt, "gla_prompt")
        o_d, st_d = _mixer_call(_linear_prompt_kernel, gd, bsz, seq, tt, "hgrn_prompt")
        o_c, xr_p, xi_p = _s5(pc.reshape(seq * bsz, BR_WIDTH), zeros_p, zeros_p, s5_params, seq, bsz, s5_steps)
        hp = _merge(hp, o_a, o_b, o_c.reshape(seq, bsz * BR_WIDTH), o_d, nw1, w_merge, wb, wo, tm_p, seq,
                    prepared=(ga, gb, gd), gains=(gdn_params[3], gla_params[2], hgrn_params[1]))
        hp = _ffn(hp, nw2, wg, wu, wd, fw, tm_p, final)
        for lst, s in zip(outs_p, (conv_p, st_a, st_b, xr_p.reshape(bsz, S5_GROUPS, S5_P),
                                   xi_p.reshape(bsz, S5_GROUPS, S5_P), st_d)):
            lst.append(s)

        pa, pb, pc, pd = _in_proj(hs, nw1, w_mix, dbs, dbs)
        ar, ac, br, bc, dr, dc, conv_s = _decode_prep(
            pa, pb, pd, state_gdn_conv[i].reshape(dbs, (CONV_W - 1) * 3 * BR_WIDTH), prep_params)
        o_a, sa = _decode_call(_gdn_decode_kernel, ar, ac, state_gdn[i], "gdn_decode")
        o_b, sb = _decode_call(_gla_decode_kernel, br, bc, state_gla[i], "gla_decode")
        o_d, sd = _decode_call(_gla_decode_kernel, dr, dc, state_hgrn[i], "hgrn_decode")
        o_a, o_b, o_d = _decode_post(o_a, o_b, o_d, pa, pb, pd, gdn_params[3], gla_params[2], hgrn_params[1])
        o_c, xr_s, xi_s = _s5(pc, state_s5_re[i].reshape(dbs, S5_STATE), state_s5_im[i].reshape(dbs, S5_STATE),
                              s5_params, 1, dbs, 1)
        hs = _merge(hs, o_a, o_b, o_c, o_d, nw1, w_merge, wb, wo, dbs, dbs)
        hs = _ffn(hs, nw2, wg, wu, wd, fw, dbs, final)
        for lst, s in zip(outs_s, (conv_s.reshape(dbs, CONV_W - 1, 3 * BR_WIDTH), sa, sb,
                                   xr_s.reshape(dbs, S5_GROUPS, S5_P), xi_s.reshape(dbs, S5_GROUPS, S5_P), sd)):
            lst.append(s)

    return (hp.reshape(bsz, seq, D_MODEL), hs.reshape(dbs, 1, D_MODEL),
            *[jnp.stack(l) for l in outs_p], *[jnp.stack(l) for l in outs_s])
```

```python
import functools
import math

import jax
import jax.numpy as jnp
from jax import lax
from jax.experimental import pallas as pl
from jax.experimental.pallas import tpu as pltpu

F32 = jnp.float32
BF16 = jnp.bfloat16

D_MODEL = 1024
DEPTH = 4
N_BRANCH = 4
BR_WIDTH = D_MODEL // N_BRANCH
HEAD_DIM = 64
N_HEADS = BR_WIDTH // HEAD_DIM
CONV_W = 4
CHUNK = 64
SUB = 16
GLA_GATE_RANK = 16
GLA_GATE_NORM = 16.0
S5_GROUP = 16
S5_GROUPS = BR_WIDTH // S5_GROUP
S5_P = 64
S5_STATE = S5_GROUPS * S5_P
D_FF = -(-8 * D_MODEL // (3 * 256)) * 256
FF_TILE = 256
S5_UNROLL = 8
EPS = 1e-6
SEQ_PER_STEP = 4
STREAM_SKEW = 1
EXP_CAP = 60.0
LANE = 128
SEG_A = 3 * BR_WIDTH + BR_WIDTH + LANE
SEG_B = 4 * BR_WIDTH + LANE
SEG_C = BR_WIDTH
SEG_D = 4 * BR_WIDTH
SEG_OFF = (0, SEG_A, SEG_A + SEG_B, SEG_A + SEG_B + SEG_C, SEG_A + SEG_B + SEG_C + SEG_D)
N_MIX = SEG_OFF[-1]
PREP_A = 6 * BR_WIDTH
PREP_B = 5 * BR_WIDTH
VMEM_LIMIT = 56 * 1024 * 1024


def _cparams(*sem):
    return pltpu.CompilerParams(dimension_semantics=sem, vmem_limit_bytes=VMEM_LIMIT)


def _full(shape):
    n = len(shape)
    return pl.BlockSpec(shape, lambda *_: (0,) * n)


def _rms(x, w):
    return x * lax.rsqrt(jnp.mean(x * x, axis=-1, keepdims=True) + EPS) * w


def _sigmoid(x):
    return 1.0 / (1.0 + jnp.exp(-x))


def _silu(x):
    return x * _sigmoid(x)


def _softplus(x):
    return jnp.maximum(x, 0.0) + jnp.log(1.0 + jnp.exp(-jnp.abs(x)))


def _log_sigmoid(x):
    return -_softplus(-x)


def _gelu(x):
    return 0.5 * x * (1.0 + jnp.tanh(math.sqrt(2.0 / math.pi) * (x + 0.044715 * x * x * x)))


def _dot(a, b):
    return jnp.dot(a.astype(BF16), b.astype(BF16), preferred_element_type=F32)


def _dot_nt(a, b):
    return lax.dot_general(a.astype(BF16), b.astype(BF16), (((1,), (1,)), ((), ())), preferred_element_type=F32)


def _dot_tn(a, b):
    return lax.dot_general(a.astype(BF16), b.astype(BF16), (((0,), (0,)), ((), ())), preferred_element_type=F32)


def _split(x, terms):
    parts, rest = [], x
    for t in range(terms):
        p = rest.astype(BF16)
        parts.append(p)
        if t + 1 < terms:
            rest = rest - p.astype(F32)
    return parts


def _dot_xl(x, c, terms=3):
    return sum(jnp.dot(p, c, preferred_element_type=F32) for p in _split(x, terms))


def _dot_xr(c, x, terms=3):
    return sum(jnp.dot(c, p, preferred_element_type=F32) for p in _split(x, terms))


def _dot_nt_xl(x, c, terms=3):
    return sum(lax.dot_general(p, c, (((1,), (1,)), ((), ())), preferred_element_type=F32) for p in _split(x, terms))


def _iota2(shape, dim):
    return lax.broadcasted_iota(jnp.int32, shape, dim)


def _head_masks():
    r = _iota2((CHUNK, N_HEADS * CHUNK), 0)
    c = _iota2((CHUNK, N_HEADS * CHUNK), 1) & (CHUNK - 1)
    rr = _iota2((N_HEADS * CHUNK, N_HEADS * HEAD_DIM), 0)
    cc = _iota2((N_HEADS * CHUNK, N_HEADS * HEAD_DIM), 1)
    block = (rr >> 6) == (cc >> 6)
    one = lambda mask: jnp.where(mask, 1.0, 0.0)
    return dict(causal=r >= c, strict=r > c, eye=r == c, same_sub=(r >> 4) == (c >> 4), block=block,
                block16=one(block).astype(BF16), sub16=one((rr >> 4) == (cc >> 4)).astype(BF16),
                eye_sub=one(_iota2((SUB, BR_WIDTH), 0) == (_iota2((SUB, BR_WIDTH), 1) & (SUB - 1))))


def _chunk_tri(tt):
    r = _iota2((tt, tt), 0)
    c = _iota2((tt, tt), 1)
    return jnp.where((r >= c) & ((r >> 6) == (c >> 6)), 1.0, 0.0).astype(BF16)


def _chunk_ones(tt):
    r = _iota2((tt, tt), 0) >> 6
    c = _iota2((tt, tt), 1) >> 6
    return jnp.where(r == c, 1.0, 0.0).astype(BF16)


def _block_ones():
    return _chunk_ones(BR_WIDTH)


def _blockdiag(x, block16):
    return jnp.concatenate([x.astype(BF16)] * N_HEADS, axis=0) * block16


def _head_norm_gate(o, silu_gate, w, ones_blk):
    ms = _dot_xl(o * o, ones_blk, 2) * (1.0 / HEAD_DIM)
    return o * lax.rsqrt(ms + EPS) * w * silu_gate


def _in_proj_kernel(x_ref, nw_ref, w_ref, pa_ref, pb_ref, pc_ref, pd_ref):
    xn = _rms(x_ref[...], nw_ref[...]).astype(BF16)
    for i, ref in enumerate((pa_ref, pb_ref, pc_ref, pd_ref)):
        ref[...] = _dot(xn, w_ref[:, SEG_OFF[i]:SEG_OFF[i + 1]])


def _time_major_spec(tm, seq):
    nt = seq // tm
    return pl.BlockSpec((tm, BR_WIDTH), lambda i: (i % nt, i // nt))


def _in_proj_prompt_kernel(x_ref, halo_ref, nw_ref, w_ref, cw_ref, alog_ref, dtb_ref, gkw_ref, gkb_ref, lb_ref,
                           ga_ref, gb_ref, pc_ref, gd_ref, conv_ref, *, nt):
    w = BR_WIDTH
    seg = lambda i: w_ref[:, SEG_OFF[i]:SEG_OFF[i + 1]]
    xn = _rms(x_ref[...], nw_ref[...]).astype(BF16)
    pd = _dot(xn, seg(3))
    pa = _dot(xn, seg(0))
    hist = _dot(_rms(halo_ref[...], nw_ref[...]), w_ref[:, 0:3 * w])
    for j, val in enumerate(_hgrn_prep(pd, lb_ref)):
        gd_ref[:, j * w:(j + 1) * w] = val
    pb = _dot(xn, seg(1))
    raw = pa[:, 0:3 * w]
    hist = jnp.where((pl.program_id(0) % nt) == 0, 0.0, hist)
    rows = raw.shape[0]
    conv_ref[0] = raw[rows - (CONV_W - 1):rows, :]
    q, k, v = _gdn_conv_qkv(raw, hist, cw_ref)
    g, beta = _gdn_prep(pa[:, 4 * w:4 * w + LANE], alog_ref, dtb_ref)
    for j, val in enumerate((q, k, v, _silu(pa[:, 3 * w:4 * w]), g, beta)):
        ga_ref[:, j * w:(j + 1) * w] = val
    pc_ref[...] = _dot(xn, seg(2))
    for j, val in enumerate(_gla_prep(pb, gkw_ref, gkb_ref)):
        gb_ref[:, j * w:(j + 1) * w] = val


def _in_proj_prompt(x, nw, w, params, tm, seq):
    t = x.shape[0]
    nt = seq // tm
    bsz = t // seq
    row = lambda n: pl.BlockSpec((tm, n), lambda i: (i, 0))
    halo = pl.BlockSpec((8, D_MODEL), lambda i: (jnp.maximum(i * (tm // 8) - 1, 0), 0))
    return pl.pallas_call(
        functools.partial(_in_proj_prompt_kernel, nt=nt),
        grid=(t // tm,),
        in_specs=[row(D_MODEL), halo, _full((1, D_MODEL)), _full((D_MODEL, N_MIX))] + [_full(a.shape) for a in params],
        out_specs=[row(PREP_A), row(PREP_B), _time_major_spec(tm, seq), row(PREP_B),
                   pl.BlockSpec((1, CONV_W - 1, 3 * BR_WIDTH), lambda i: (i // nt, 0, 0))],
        out_shape=[jax.ShapeDtypeStruct((t, PREP_A), F32), jax.ShapeDtypeStruct((t, PREP_B), F32),
                   jax.ShapeDtypeStruct((seq, bsz * BR_WIDTH), F32), jax.ShapeDtypeStruct((t, PREP_B), F32),
                   jax.ShapeDtypeStruct((bsz, CONV_W - 1, 3 * BR_WIDTH), F32)],
        compiler_params=_cparams("arbitrary"),
        name="in_proj_prompt",
    )(x, x, nw, w, *params)


def _in_proj(x, nw, w, tm, seq):
    t = x.shape[0]
    widths = (SEG_A, SEG_B, SEG_C, SEG_D)
    out_specs = [pl.BlockSpec((tm, n), lambda i: (i, 0)) for n in widths]
    out_shape = [jax.ShapeDtypeStruct((t, n), F32) for n in widths]
    out_specs[2] = _time_major_spec(tm, seq)
    out_shape[2] = jax.ShapeDtypeStruct((seq, (t // seq) * BR_WIDTH), F32)
    return pl.pallas_call(
        _in_proj_kernel,
        grid=(t // tm,),
        in_specs=[pl.BlockSpec((tm, D_MODEL), lambda i: (i, 0)), _full((1, D_MODEL)), _full((D_MODEL, N_MIX))],
        out_specs=out_specs,
        out_shape=out_shape,
        compiler_params=_cparams("parallel"),
        name="in_proj",
    )(x, nw, w)


def _merge_kernel(h_ref, oa_ref, ob_ref, oc_ref, od_ref, *rest, head_norm):
    if head_norm:
        gates, gains, rest = rest[0:3], rest[3:6], rest[6:]
    nw_ref, wm_ref, wb_ref, wo_ref, out_ref = rest
    h = h_ref[...]
    xn = _rms(h, nw_ref[...]).astype(BF16)
    o_vals = [oa_ref[...], ob_ref[...], oc_ref[...], od_ref[...]]
    if head_norm:
        ones_blk = _block_ones()
        for k, g_ref, w_ref in zip((0, 1, 3), gates, gains):
            o_vals[k] = _head_norm_gate(o_vals[k], g_ref[...], w_ref[...], ones_blk)
    proj = lambda k: (_dot(xn, wm_ref[:, k * D_MODEL:(k + 1) * D_MODEL]), _dot(o_vals[k], wb_ref[k]))
    acc = jnp.zeros(h.shape, F32)
    nxt = proj(0)
    for k in range(N_BRANCH):
        gate, br = nxt
        if k + 1 < N_BRANCH:
            nxt = proj(k + 1)
        acc = acc + _sigmoid(gate) * br
    out_ref[...] = h + _dot(acc, wo_ref[...])


def _merge(h, oa, ob, oc, od, nw, wm, wb, wo, tm, seq, prepared=None, gains=None):
    t = h.shape[0]
    row = lambda n: pl.BlockSpec((tm, n), lambda i: (i, 0))
    head_norm = prepared is not None
    extra, extra_specs = (), []
    if head_norm:
        extra = tuple(prepared) + tuple(gains)
        extra_specs = [pl.BlockSpec((tm, BR_WIDTH), lambda i: (i, 3))] * 3 + [_full((1, BR_WIDTH))] * 3
    return pl.pallas_call(
        functools.partial(_merge_kernel, head_norm=head_norm),
        grid=(t // tm,),
        in_specs=[row(D_MODEL), row(BR_WIDTH), row(BR_WIDTH), _time_major_spec(tm, seq), row(BR_WIDTH)]
                 + extra_specs + [_full((1, D_MODEL)), _full((D_MODEL, N_BRANCH * D_MODEL)),
                                  _full((N_BRANCH, BR_WIDTH, D_MODEL)), _full((D_MODEL, D_MODEL))],
        out_specs=row(D_MODEL),
        out_shape=jax.ShapeDtypeStruct((t, D_MODEL), F32),
        compiler_params=_cparams("parallel"),
        name="merge",
    )(h, oa, ob, oc, od, *extra, nw, wm, wb, wo)


def _ffn_kernel(h_ref, nw_ref, wg_ref, wu_ref, wd_ref, fw_ref, out_ref, *, final):
    h = h_ref[...]
    hn = _rms(h, nw_ref[...]).astype(BF16)
    n_tiles = D_FF // FF_TILE
    cols = [slice(c * FF_TILE, (c + 1) * FF_TILE) for c in range(n_tiles)]
    up = lambda sl: (_dot(hn, wg_ref[:, sl]), _dot(hn, wu_ref[:, sl]))
    acc = jnp.zeros(h.shape, F32)
    nxt = up(cols[0])
    for c in range(n_tiles):
        g, u = nxt
        if c + 1 < n_tiles:
            nxt = up(cols[c + 1])
        acc = acc + _dot(_silu(g) * u, wd_ref[cols[c], :])
    hnew = h + acc
    out_ref[...] = _rms(hnew, fw_ref[...]) if final else hnew


def _ffn(h, nw, wg, wu, wd, fw, tm, final):
    t = h.shape[0]
    row = pl.BlockSpec((tm, D_MODEL), lambda i: (i, 0))
    return pl.pallas_call(
        functools.partial(_ffn_kernel, final=final),
        grid=(t // tm,),
        in_specs=[row, _full((1, D_MODEL)), _full((D_MODEL, D_FF)), _full((D_MODEL, D_FF)),
                  _full((D_FF, D_MODEL)), _full((1, D_MODEL))],
        out_specs=row,
        out_shape=jax.ShapeDtypeStruct((t, D_MODEL), F32),
        compiler_params=_cparams("parallel"),
        name="ffn",
    )(h, nw, wg, wu, wd, fw)


def _run_skewed(streams, skew):
    results = [None] * len(streams)
    live, step = [], 0
    pending = list(enumerate(streams))
    while pending or live:
        while pending and pending[0][0] * skew <= step:
            live.append(pending.pop(0))
        still = []
        for idx, gen in live:
            try:
                next(gen)
                still.append((idx, gen))
            except StopIteration as done:
                results[idx] = done.value
        live = still
        step += 1
    return results


def _gla_stream(load, st, m, tt):
    each = lambda f, *xs: [f(*a) for a in zip(*xs)]
    blk = m["block16"]
    b_all = _dot_xr(_chunk_tri(tt), load(4, slice(0, tt)), 3)
    yield
    rows = [slice(c * CHUNK, (c + 1) * CHUNK) for c in range(tt // CHUNK)]
    q, k, v = ([load(j, r) for r in rows] for j in range(3))
    b = [b_all[r, :] for r in rows]
    b_last = each(lambda x: x[CHUNK - 1:CHUNK, :], b)
    scores = []
    for i0 in range(0, CHUNK, SUB):
        q_i = each(lambda qq, x: qq[i0:i0 + SUB, :] * jnp.exp(x[i0:i0 + SUB, :] - x[i0:i0 + 1, :]), q, b)
        k_i = each(lambda kk, x: kk * jnp.exp(jnp.minimum(x[i0:i0 + 1, :] - x, EXP_CAP)), k, b)
        yield
        scores.append(each(lambda a, c: _dot_nt(a, _blockdiag(c, blk)), q_i, k_i))
    kv = each(lambda vv, kk, x, xl: jnp.where(m["block"], _dot_tn(vv, kk * jnp.exp(xl - x)), 0.0), v, k, b, b_last)
    yield
    attn = each(lambda *r: jnp.where(m["causal"], jnp.concatenate(r, axis=0), 0.0), *scores)
    qe = each(lambda qq, x: qq * jnp.exp(x), q, b)
    yield
    o_loc = each(lambda a, vv: _dot(a, _blockdiag(vv, blk)), attn, v)
    yield
    outs = []
    for c in range(len(q)):
        outs.append(_dot_nt(qe[c], st) + o_loc[c])
        st = st * jnp.exp(b_last[c]) + kv[c]
        yield
    return outs, st


def _write_state(out_ref, i, st):
    r = _iota2((BR_WIDTH, BR_WIDTH), 0)
    c = _iota2((BR_WIDTH, BR_WIDTH), 1)
    eye = jnp.where(r == c, 1.0, 0.0).astype(BF16)
    nt_dot = lambda p: lax.dot_general(eye, p, (((1,), (1,)), ((), ())), preferred_element_type=F32)
    st_t = sum(nt_dot(p) for p in _split(st, 3))
    for h in range(N_HEADS):
        out_ref[i, h] = st_t[h * HEAD_DIM:(h + 1) * HEAD_DIM, h * HEAD_DIM:(h + 1) * HEAD_DIM]


def _gla_prep(pb, gkw_ref, gkb_ref):
    w = BR_WIDTH
    lf = _log_sigmoid(_dot(pb[:, 4 * w:4 * w + LANE], gkw_ref[...]) + gkb_ref[...]) * (1.0 / GLA_GATE_NORM)
    return pb[:, 0:w] * HEAD_DIM ** -0.5, pb[:, w:2 * w], pb[:, 2 * w:3 * w], _silu(pb[:, 3 * w:4 * w]), lf


def _hgrn_prep(pd, lb_ref):
    w = BR_WIDTH
    lb = lb_ref[...]
    xf = pd[:, w:2 * w]
    return (_silu(pd[:, 0:w]) * HEAD_DIM ** -0.5, (1.0 - lb) * _sigmoid(-xf), pd[:, 2 * w:3 * w],
            _silu(pd[:, 3 * w:4 * w]), _hgrn_log_forget(xf, lb))


def _linear_prompt_kernel(p_ref, o_ref, sout_ref, st_s, *, tt):
    @pl.when(pl.program_id(1) == 0)
    def _():
        st_s[...] = jnp.zeros(st_s.shape, F32)

    w = BR_WIDTH
    m = _head_masks()
    n_seq = p_ref.shape[0]
    load = lambda i: (lambda j, r: p_ref[i, r, j * w:(j + 1) * w])
    results = _run_skewed([_gla_stream(load(i), st_s[i], m, tt) for i in range(n_seq)], STREAM_SKEW)
    for i, (outs, st) in enumerate(results):
        st_s[i] = st
        o_ref[i] = jnp.concatenate(outs, axis=0)

    @pl.when(pl.program_id(1) == pl.num_programs(1) - 1)
    def _():
        for i in range(n_seq):
            _write_state(sout_ref, i, st_s[i])


def _hgrn_log_forget(xf, lb):
    ls_pos = _log_sigmoid(xf)
    pos = lb > 0
    lb_safe = jnp.where(pos, lb, 1.0)
    mixed = ls_pos + _softplus(jnp.log(lb_safe) - xf)
    return jnp.where(pos, mixed, ls_pos)


def _mixer_call(kernel, p, bsz, seq, tt, name):
    nt = seq // tt
    width = p.shape[1]
    n_seq = SEQ_PER_STEP if bsz % SEQ_PER_STEP == 0 else 1
    o, st = pl.pallas_call(
        functools.partial(kernel, tt=tt),
        grid=(bsz // n_seq, nt),
        in_specs=[pl.BlockSpec((n_seq, tt, width), lambda b, t: (b, t, 0))],
        out_specs=[pl.BlockSpec((n_seq, tt, BR_WIDTH), lambda b, t: (b, t, 0)),
                   pl.BlockSpec((n_seq, N_HEADS, HEAD_DIM, HEAD_DIM), lambda b, t: (b, 0, 0, 0))],
        out_shape=[jax.ShapeDtypeStruct((bsz, seq, BR_WIDTH), F32),
                   jax.ShapeDtypeStruct((bsz, N_HEADS, HEAD_DIM, HEAD_DIM), F32)],
        scratch_shapes=[pltpu.VMEM((n_seq, BR_WIDTH, BR_WIDTH), F32)],
        compiler_params=_cparams("parallel", "arbitrary"),
        name=name,
    )(p.reshape(bsz, seq, width))
    return o.reshape(bsz * seq, BR_WIDTH), st


def _mm_split(a, b, tile_mask):
    a_hi, a_lo = _split(a, 2)
    b_hi, b_lo = (jnp.concatenate([p] * (BR_WIDTH // b.shape[0]), axis=0) * tile_mask for p in _split(b, 2))
    dot = functools.partial(jnp.dot, preferred_element_type=F32)
    return dot(a_hi, b_hi) + (dot(a_lo, b_hi) + dot(a_hi, b_lo))


def _unit_lower_inverse(n, m):
    each = lambda f, *xs: [f(*a) for a in zip(*xs)]
    eye_c, sub, blk = m["eye_sub"], m["sub16"], m["block16"]
    nd = each(lambda a: jnp.where(m["same_sub"], a, 0.0), n)
    low = each(lambda a, d: a - d, n, nd)
    c = each(lambda d: d[0:SUB] + d[SUB:2 * SUB] + d[2 * SUB:3 * SUB] + d[3 * SUB:4 * SUB], nd)
    p = each(lambda a: _mm_split(a, a, sub), c)
    x = each(lambda a: eye_c - a, c)
    yield
    for _ in range(2):
        xp = each(lambda a, b: _mm_split(jnp.concatenate([a, b], axis=0), b, sub), x, p)
        x = each(lambda a, b: a + b[0:SUB], x, xp)
        p = each(lambda b: b[SUB:2 * SUB], xp)
        yield
    x = each(lambda a, b: a + _mm_split(a, b, sub), x, p)
    dinv = each(lambda a: jnp.where(m["same_sub"], jnp.concatenate([a] * (CHUNK // SUB), axis=0), 0.0), x)
    yield
    mm = each(lambda a, b: _mm_split(a, b, blk), dinv, low)
    yield
    m2 = each(lambda a: _mm_split(a, a, blk), mm)
    yield
    e = each(lambda d, a: d + _mm_split(a, d, blk), dinv, m2)
    yield
    return each(lambda a, b: b - _mm_split(a, b, blk), mm, e)


def _gdn_stream(load, st, m, tt):
    each = lambda f, *xs: [f(*a) for a in zip(*xs)]
    w = BR_WIDTH
    blk = m["block16"]
    gam_all = _dot_xr(_chunk_tri(tt), load(4, slice(0, tt)), 3)
    yield
    lane_j = _iota2((tt, w), 1) & (CHUNK - 1)
    row_j = _iota2((tt, w), 0) & (CHUNK - 1)
    gam_t = _dot_xr(_chunk_ones(tt), jnp.where(lane_j == row_j, gam_all, 0.0), 3)
    yield
    rows = [slice(c * CHUNK, (c + 1) * CHUNK) for c in range(tt // CHUNK)]
    q, k, v, bexp = ([load(j, r) for r in rows] for j in (0, 1, 2, 5))
    gam_i = [gam_all[r, :] for r in rows]
    gam_j = [gam_t[r, :] for r in rows]
    decay = each(lambda gi, gj: jnp.where(m["causal"], jnp.exp(jnp.minimum(gi - gj, 0.0)), 0.0), gam_i, gam_j)
    scores = each(lambda kk, qq: _dot_nt(jnp.concatenate([kk, qq], axis=0), _blockdiag(kk, blk)), k, q)
    yield
    attn = each(lambda s, d: s[CHUNK:2 * CHUNK] * d, scores, decay)
    n = each(lambda b, s, d: jnp.where(m["strict"], b * s[0:CHUNK] * d, 0.0), bexp, scores, decay)
    tinv = yield from _unit_lower_inverse(n, m)
    yield
    u_v = each(lambda t, b, vv: _dot(t, _blockdiag(b * vv, blk)), tinv, bexp, v)
    w_k = each(lambda t, b, gi, kk: _dot(t, _blockdiag(b * jnp.exp(gi) * kk, blk)), tinv, bexp, gam_i, k)
    yield
    outs = []
    for c in range(len(rows)):
        gam_last = gam_i[c][CHUNK - 1:CHUNK, :]
        ws = _dot_nt(jnp.concatenate([w_k[c], q[c]], axis=0), st)
        u = u_v[c] - ws[0:CHUNK]
        yield
        outs.append(jnp.exp(gam_i[c]) * ws[CHUNK:2 * CHUNK] + _dot(attn[c], _blockdiag(u, blk)))
        kd = k[c] * jnp.exp(gam_last - gam_i[c])
        st = st * jnp.exp(gam_last) + jnp.where(m["block"], _dot_tn(u, kd), 0.0)
        yield
    return outs, st


def _l2norm_heads(x, ones_blk):
    return x * lax.rsqrt(_dot_xl(x * x, ones_blk, 2) + EPS)


def _gdn_prep(a_alpha_beta, alog_ref, dtb_ref):
    r = _iota2((LANE, BR_WIDTH), 0)
    c = _iota2((LANE, BR_WIDTH), 1) >> 6
    sel_a = jnp.where(r == c, 1.0, 0.0).astype(BF16)
    sel_b = jnp.where(r == c + N_HEADS, 1.0, 0.0).astype(BF16)
    alpha = _dot_xl(a_alpha_beta, sel_a, 3)
    beta = _sigmoid(_dot_xl(a_alpha_beta, sel_b, 3))
    g = -jnp.exp(alog_ref[...]) * _softplus(alpha + dtb_ref[...])
    return g, beta


def _gdn_conv_qkv(raw, hist, cw_ref):
    w = BR_WIDTH
    rows = raw.shape[0]
    xp = jnp.concatenate([hist, raw], axis=0)
    y = raw * cw_ref[CONV_W - 1:CONV_W, :]
    for j in range(CONV_W - 1):
        start = 8 - (CONV_W - 1) + j
        y = y + xp[start:start + rows, :] * cw_ref[j:j + 1, :]
    qkv = _silu(y)
    ones_blk = _block_ones()
    return (_l2norm_heads(qkv[:, 0:w], ones_blk) * HEAD_DIM ** -0.5, _l2norm_heads(qkv[:, w:2 * w], ones_blk),
            qkv[:, 2 * w:3 * w])


def _gdn_prompt_kernel(p_ref, o_ref, sout_ref, st_s, *, tt):
    t = pl.program_id(1)

    @pl.when(t == 0)
    def _():
        st_s[...] = jnp.zeros(st_s.shape, F32)

    w = BR_WIDTH
    m = _head_masks()
    n_seq = p_ref.shape[0]
    load = lambda i: (lambda j, r: p_ref[i, r, j * w:(j + 1) * w])
    results = _run_skewed([_gdn_stream(load(i), st_s[i], m, tt) for i in range(n_seq)], STREAM_SKEW)
    for i, (outs, st) in enumerate(results):
        st_s[i] = st
        o_ref[i] = jnp.concatenate(outs, axis=0)

    @pl.when(t == pl.num_programs(1) - 1)
    def _():
        for i in range(n_seq):
            _write_state(sout_ref, i, st_s[i])


def _s5_kernel(u_ref, x0r_ref, x0i_ref, ar_ref, ai_ref, br_ref, bi_ref, cr_ref, ci_ref, d_ref, glu_ref,
               o_ref, xr_ref, xi_ref, sr_s, si_s, *, steps, rows):
    @pl.when(pl.program_id(0) == 0)
    def _():
        xr_ref[...] = x0r_ref[...]
        xi_ref[...] = x0i_ref[...]

    u = u_ref[...]
    ub = u.astype(BF16)
    sr_s[...] = _dot(ub, br_ref[...])
    si_s[...] = _dot(ub, bi_ref[...])
    a_re = jnp.broadcast_to(ar_ref[...], (rows, S5_STATE))
    a_im = jnp.broadcast_to(ai_ref[...], (rows, S5_STATE))

    def body(t, carry):
        xr, xi = carry
        r = pl.ds(pl.multiple_of(t * rows, rows), rows)
        nr = (sr_s[r, :] + a_re * xr) - a_im * xi
        ni = (si_s[r, :] + a_re * xi) + a_im * xr
        sr_s[r, :] = nr
        si_s[r, :] = ni
        return nr, ni

    xr, xi = lax.fori_loop(0, steps, body, (xr_ref[...], xi_ref[...]), unroll=min(steps, S5_UNROLL))
    xr_ref[...] = xr
    xi_ref[...] = xi
    y = _dot(sr_s[...].astype(BF16), cr_ref[...]) - _dot(si_s[...].astype(BF16), ci_ref[...]) + d_ref[...] * u
    hg = _dot(_gelu(y).astype(BF16), glu_ref[...])
    o_ref[...] = hg[:, 0:BR_WIDTH] * _sigmoid(hg[:, BR_WIDTH:2 * BR_WIDTH])


def _s5(u, x0r, x0i, params, steps_total, rows, steps):
    nt = steps_total // steps
    tile = steps * rows
    return pl.pallas_call(
        functools.partial(_s5_kernel, steps=steps, rows=rows),
        grid=(nt,),
        in_specs=[pl.BlockSpec((tile, BR_WIDTH), lambda t: (t, 0)), _full(x0r.shape), _full(x0i.shape)]
                 + [_full(a.shape) for a in params],
        out_specs=[pl.BlockSpec((tile, BR_WIDTH), lambda t: (t, 0)), _full(x0r.shape), _full(x0i.shape)],
        out_shape=[jax.ShapeDtypeStruct((steps_total * rows, BR_WIDTH), F32),
                   jax.ShapeDtypeStruct(x0r.shape, F32), jax.ShapeDtypeStruct(x0i.shape, F32)],
        scratch_shapes=[pltpu.VMEM((tile, S5_STATE), F32)] * 2,
        compiler_params=_cparams("arbitrary"),
        name="s5",
    )(u, x0r, x0i, *params)


def _s5_params(lam_re, lam_im, b_re, b_im, c_re, c_im, d, log_dt, glu_w):
    lr = jnp.minimum(lam_re, -1e-4)
    li = lam_im
    dt = jnp.exp(log_dt)[:, None]
    mag = jnp.exp(lr * dt)
    ab_re, ab_im = mag * jnp.cos(li * dt), mag * jnp.sin(li * dt)
    den = lr * lr + li * li
    z_re = ((ab_re - 1.0) * lr + ab_im * li) / den
    z_im = (ab_im * lr - (ab_re - 1.0) * li) / den
    bb_re = z_re[..., None] * b_re - z_im[..., None] * b_im
    bb_im = z_re[..., None] * b_im + z_im[..., None] * b_re
    eye = jnp.eye(S5_GROUPS, dtype=F32)

    def pack_in(bb):
        return jnp.einsum('gph,gk->ghkp', bb, eye).reshape(BR_WIDTH, S5_STATE).astype(BF16)

    def pack_out(c):
        return jnp.einsum('ghp,gk->gpkh', c, eye).reshape(S5_STATE, BR_WIDTH).astype(BF16)

    return (ab_re.reshape(1, S5_STATE), ab_im.reshape(1, S5_STATE), pack_in(bb_re), pack_in(bb_im),
            pack_out(c_re), pack_out(c_im), d.reshape(1, BR_WIDTH), glu_w.astype(BF16))


def _expand_mats():
    r = _iota2((HEAD_DIM, HEAD_DIM * HEAD_DIM), 0)
    c = _iota2((HEAD_DIM, HEAD_DIM * HEAD_DIM), 1)
    rep_k = jnp.where(r == (c >> 6), 1.0, 0.0).astype(BF16)
    rep_v = jnp.where(r == (c & (HEAD_DIM - 1)), 1.0, 0.0).astype(BF16)
    return rep_k, rep_v


def _gla_decode_kernel(q_ref, k_ref, v_ref, lf_ref, s_ref, o_ref, sn_ref):
    rep_k, rep_v = _expand_mats()
    q, k, v, lf = q_ref[0, 0], k_ref[0, 0], v_ref[0, 0], lf_ref[0, 0]
    s = s_ref[...]
    dec = jnp.exp(lf)
    qe_x = _dot_xl(q * dec, rep_k, 1)
    o = _dot_nt_xl(qe_x * s, rep_v, 2) + jnp.sum(q * k, axis=-1, keepdims=True) * v
    sn_ref[...] = _dot_xl(dec, rep_k, 3) * s + _dot_xl(k, rep_k, 1) * _dot_xl(v, rep_v, 1)
    o_ref[0] = o


def _gdn_decode_kernel(q_ref, k_ref, v_ref, g_ref, b_ref, s_ref, o_ref, sn_ref):
    rep_k, rep_v = _expand_mats()
    q, k, v = q_ref[0, 0], k_ref[0, 0], v_ref[0, 0]
    eg = jnp.exp(g_ref[0, 0])
    beta = b_ref[0, 0]
    s = s_ref[...]
    k_x = _dot_xl(k, rep_k, 1)
    ks = _dot_nt_xl(k_x * s, rep_v, 2)
    qs = _dot_nt_xl(_dot_xl(q, rep_k, 1) * s, rep_v, 2)
    u = beta * v - (beta * eg) * ks
    o_ref[0] = eg * qs + jnp.sum(q * k, axis=-1, keepdims=True) * u
    sn_ref[...] = eg * s + k_x * _dot_xl(u, rep_v, 1)


def _decode_call(kernel, packs, state, name):
    rows = state.shape[0]
    hw = HEAD_DIM * HEAD_DIM
    vec_specs, vecs = [], []
    for a in packs:
        for j in range(a.shape[0]):
            vec_specs.append(pl.BlockSpec((1, 1, rows, a.shape[3]), lambda h, j=j: (j, h, 0, 0)))
            vecs.append(a)
    return pl.pallas_call(
        kernel,
        grid=(N_HEADS,),
        in_specs=vec_specs + [pl.BlockSpec((rows, hw), lambda h: (0, h))],
        out_specs=[pl.BlockSpec((1, rows, HEAD_DIM), lambda h: (h, 0, 0)), pl.BlockSpec((rows, hw), lambda h: (0, h))],
        out_shape=[jax.ShapeDtypeStruct((N_HEADS, rows, HEAD_DIM), F32), jax.ShapeDtypeStruct(state.shape, F32)],
        compiler_params=_cparams("parallel"),
        name=name,
    )(*vecs, state)


def _put_heads(ref, j, x):
    for h in range(N_HEADS):
        ref[j, h] = x[:, h * HEAD_DIM:(h + 1) * HEAD_DIM]


def _decode_prep_kernel(pa_ref, pb_ref, pd_ref, conv_ref, cw_ref, alog_ref, dtb_ref, gkw_ref, gkb_ref, lb_ref,
                        a_ref, ag_ref, b_ref, d_ref, convn_ref):
    w = BR_WIDTH
    cw = 3 * w
    ones_blk = _block_ones()
    raw = pa_ref[:, 0:cw]
    y = raw * cw_ref[CONV_W - 1:CONV_W, :]
    for j in range(CONV_W - 1):
        y = y + conv_ref[:, j * cw:(j + 1) * cw] * cw_ref[j:j + 1, :]
    convn_ref[:, 0:(CONV_W - 2) * cw] = conv_ref[:, cw:(CONV_W - 1) * cw]
    convn_ref[:, (CONV_W - 2) * cw:(CONV_W - 1) * cw] = raw
    qkv = _silu(y)
    _put_heads(a_ref, 0, _l2norm_heads(qkv[:, 0:w], ones_blk) * HEAD_DIM ** -0.5)
    _put_heads(a_ref, 1, _l2norm_heads(qkv[:, w:2 * w], ones_blk))
    _put_heads(a_ref, 2, qkv[:, 2 * w:3 * w])
    g, beta = _gdn_prep(pa_ref[:, 4 * w:4 * w + LANE], alog_ref, dtb_ref)
    for h in range(N_HEADS):
        ag_ref[0, h] = g[:, h * HEAD_DIM:h * HEAD_DIM + 1]
        ag_ref[1, h] = beta[:, h * HEAD_DIM:h * HEAD_DIM + 1]
    for ref, (q, k, v, _, lf) in ((b_ref, _gla_prep(pb_ref[...], gkw_ref, gkb_ref)),
                                  (d_ref, _hgrn_prep(pd_ref[...], lb_ref))):
        for j, val in enumerate((q, k, v, lf)):
            _put_heads(ref, j, val)


def _decode_prep(pa, pb, pd, conv, params):
    rows = pa.shape[0]
    ins = (pa, pb, pd, conv) + tuple(params)
    per_head = lambda n, width: (n, N_HEADS, rows, width)
    shapes = [per_head(3, HEAD_DIM), per_head(2, 1), per_head(4, HEAD_DIM), per_head(4, HEAD_DIM), conv.shape]
    return pl.pallas_call(
        _decode_prep_kernel,
        grid=(1,),
        in_specs=[_full(a.shape) for a in ins],
        out_specs=[_full(s) for s in shapes],
        out_shape=[jax.ShapeDtypeStruct(s, F32) for s in shapes],
        compiler_params=_cparams("arbitrary"),
        name="decode_prep",
    )(*ins)


def _decode_post_kernel(oa_ref, ob_ref, od_ref, pa_ref, pb_ref, pd_ref, nwa_ref, nwb_ref, nwd_ref,
                        a_ref, b_ref, d_ref):
    w = BR_WIDTH
    ones_blk = _block_ones()
    heads = lambda ref: jnp.concatenate([ref[h] for h in range(N_HEADS)], axis=1)
    for out, o, p, nw in ((a_ref, oa_ref, pa_ref, nwa_ref), (b_ref, ob_ref, pb_ref, nwb_ref),
                          (d_ref, od_ref, pd_ref, nwd_ref)):
        out[...] = _head_norm_gate(heads(o), _silu(p[:, 3 * w:4 * w]), nw[...], ones_blk)


def _decode_post(oa, ob, od, pa, pb, pd, nwa, nwb, nwd):
    ins = (oa, ob, od, pa, pb, pd, nwa, nwb, nwd)
    shp = (oa.shape[1], BR_WIDTH)
    return pl.pallas_call(
        _decode_post_kernel,
        grid=(1,),
        in_specs=[_full(a.shape) for a in ins],
        out_specs=[_full(shp)] * 3,
        out_shape=[jax.ShapeDtypeStruct(shp, F32)] * 3,
        compiler_params=_cparams("arbitrary"),
        name="decode_post",
    )(*ins)


def _pack_w_in(w_in):
    sizes = (3 * BR_WIDTH, N_HEADS, N_HEADS, BR_WIDTH, BR_WIDTH, BR_WIDTH, BR_WIDTH, GLA_GATE_RANK, BR_WIDTH,
             BR_WIDTH, BR_WIDTH, BR_WIDTH, BR_WIDTH, BR_WIDTH, N_BRANCH * D_MODEL)
    w_in = w_in.astype(BF16)
    parts, off = [], 0
    for n in sizes:
        parts.append(w_in[..., off:off + n])
        off += n
    (a_qkv, a_alpha, a_beta, a_gate, b_q, b_k, b_v, b_gk, b_gate, c_u, d_q, d_f, d_i, d_gate, merge) = parts
    zpad = lambda n: jnp.zeros(w_in.shape[:-1] + (n,), BF16)
    mix = jnp.concatenate([a_qkv, a_gate, a_alpha, a_beta, zpad(LANE - 2 * N_HEADS),
                           b_q, b_k, b_v, b_gate, b_gk, zpad(LANE - GLA_GATE_RANK),
                           c_u, d_q, d_f, d_i, d_gate], axis=-1)
    return mix, merge


def _tile_sizes(seq):
    pick = lambda n: n if seq % n == 0 else seq
    return pick(512), pick(256), pick(128)


def _hgrn_lower_bounds(logits):
    p = jax.nn.softmax(logits, axis=0)
    return jnp.cumsum(p, axis=0) - p[0]


def _row(x):
    return x.reshape(1, -1)


def _rep_heads(x):
    return jnp.repeat(x, HEAD_DIM).reshape(1, BR_WIDTH)


def kernel(x_prompt, x_sample, state_gdn_conv, state_gdn, state_gla, state_s5_re, state_s5_im, state_hgrn, norm1_w, w_in, gdn_conv_w, gdn_a_log, gdn_dt_bias, gdn_norm_w, gla_gk_w, gla_gk_b, gla_norm_w, s5_lambda_re, s5_lambda_im, s5_b_re, s5_b_im, s5_c_re, s5_c_im, s5_d, s5_log_dt, s5_glu_w, hgrn_lb_logits, hgrn_norm_w, w_branch, w_out, norm2_w, ffn_w_gate, ffn_w_up, ffn_w_down, final_norm_w):
    bsz, seq, _ = x_prompt.shape
    dbs = x_sample.shape[0]
    tm_p, tt, s5_steps = _tile_sizes(seq)
    lb_all = _hgrn_lower_bounds(hgrn_lb_logits)
    w_mix_all, w_merge_all = _pack_w_in(w_in)
    gkw_all = jnp.concatenate([gla_gk_w, jnp.zeros((DEPTH, LANE - GLA_GATE_RANK, BR_WIDTH), F32)], axis=1)
    wb_all, wo_all = w_branch.astype(BF16), w_out.astype(BF16)
    wg_all, wu_all, wd_all = ffn_w_gate.astype(BF16), ffn_w_up.astype(BF16), ffn_w_down.astype(BF16)
    hp = x_prompt.reshape(bsz * seq, D_MODEL)
    hs = x_sample.reshape(dbs, D_MODEL)
    zeros_p = jnp.zeros((bsz, S5_STATE), F32)
    fw = _row(final_norm_w)
    outs_p = [[] for _ in range(6)]
    outs_s = [[] for _ in range(6)]
    for i in range(DEPTH):
        final = i == DEPTH - 1
        w_mix, w_merge, gkw = w_mix_all[i], w_merge_all[i], gkw_all[i]
        nw1, nw2 = _row(norm1_w[i]), _row(norm2_w[i])
        gdn_params = (gdn_conv_w[i], _rep_heads(gdn_a_log[i]), _rep_heads(gdn_dt_bias[i]), _row(gdn_norm_w[i]))
        gla_params = (gkw, _row(gla_gk_b[i]), _row(gla_norm_w[i]))
        hgrn_params = (_row(lb_all[i]), _row(hgrn_norm_w[i]))
        s5_params = _s5_params(s5_lambda_re[i], s5_lambda_im[i], s5_b_re[i], s5_b_im[i], s5_c_re[i], s5_c_im[i],
                               s5_d[i], s5_log_dt[i], s5_glu_w[i])
        prep_params = (gdn_conv_w[i], gdn_params[1], gdn_params[2], gkw, gla_params[1], hgrn_params[0])
        wb, wo, wg, wu, wd = wb_all[i], wo_all[i], wg_all[i], wu_all[i], wd_all[i]

        ga, gb, pc, gd, conv_p = _in_proj_prompt(hp, nw1, w_mix, prep_params, tm_p, seq)
        o_a, st_a = _mixer_call(_gdn_prompt_kernel, ga, bsz, seq, tt, "gdn_prompt")
        o_b, st_b = _mixer_call(_linear_prompt_kernel, gb, bsz, seq, tt, "gla_prompt")
        o_d, st_d = _mixer_call(_linear_prompt_kernel, gd, bsz, seq, tt, "hgrn_prompt")
        o_c, xr_p, xi_p = _s5(pc.reshape(seq * bsz, BR_WIDTH), zeros_p, zeros_p, s5_params, seq, bsz, s5_steps)
        hp = _merge(hp, o_a, o_b, o_c.reshape(seq, bsz * BR_WIDTH), o_d, nw1, w_merge, wb, wo, tm_p, seq,
                    prepared=(ga, gb, gd), gains=(gdn_params[3], gla_params[2], hgrn_params[1]))
        hp = _ffn(hp, nw2, wg, wu, wd, fw, tm_p, final)
        for lst, s in zip(outs_p, (conv_p, st_a, st_b, xr_p.reshape(bsz, S5_GROUPS, S5_P),
                                   xi_p.reshape(bsz, S5_GROUPS, S5_P), st_d)):
            lst.append(s)

        pa, pb, pc, pd = _in_proj(hs, nw1, w_mix, dbs, dbs)
        va, vg, vb, vd, conv_s = _decode_prep(
            pa, pb, pd, state_gdn_conv[i].reshape(dbs, (CONV_W - 1) * 3 * BR_WIDTH), prep_params)
        flat = lambda s: s.reshape(dbs, N_HEADS * HEAD_DIM * HEAD_DIM)
        oa_h, sa = _decode_call(_gdn_decode_kernel, [va, vg], flat(state_gdn[i]), "gdn_decode")
        ob_h, sb = _decode_call(_gla_decode_kernel, [vb], flat(state_gla[i]), "gla_decode")
        od_h, sd = _decode_call(_gla_decode_kernel, [vd], flat(state_hgrn[i]), "hgrn_decode")
        o_a, o_b, o_d = _decode_post(oa_h, ob_h, od_h, pa, pb, pd, gdn_params[3], gla_params[2], hgrn_params[1])
        o_c, xr_s, xi_s = _s5(pc, state_s5_re[i].reshape(dbs, S5_STATE), state_s5_im[i].reshape(dbs, S5_STATE),
                              s5_params, 1, dbs, 1)
        hs = _merge(hs, o_a, o_b, o_c, o_d, nw1, w_merge, wb, wo, dbs, dbs)
        hs = _ffn(hs, nw2, wg, wu, wd, fw, dbs, final)
        st5 = lambda s: s.reshape(dbs, N_HEADS, HEAD_DIM, HEAD_DIM)
        for lst, s in zip(outs_s, (conv_s.reshape(dbs, CONV_W - 1, 3 * BR_WIDTH), st5(sa), st5(sb),
                                   xr_s.reshape(dbs, S5_GROUPS, S5_P), xi_s.reshape(dbs, S5_GROUPS, S5_P), st5(sd))):
            lst.append(s)

    return (hp.reshape(bsz, seq, D_MODEL), hs.reshape(dbs, 1, D_MODEL),
            *[jnp.stack(l) for l in outs_p], *[jnp.stack(l) for l in outs_s])
```

```python
import functools
import math

import jax
import jax.numpy as jnp
from jax import lax
from jax.experimental import pallas as pl
from jax.experimental.pallas import tpu as pltpu

F32 = jnp.float32
BF16 = jnp.bfloat16

D_MODEL = 1024
DEPTH = 4
N_BRANCH = 4
BR_WIDTH = D_MODEL // N_BRANCH
HEAD_DIM = 64
N_HEADS = BR_WIDTH // HEAD_DIM
CONV_W = 4
CHUNK = 64
SUB = 16
GLA_GATE_RANK = 16
GLA_GATE_NORM = 16.0
S5_GROUP = 16
S5_GROUPS = BR_WIDTH // S5_GROUP
S5_P = 64
S5_STATE = S5_GROUPS * S5_P
D_FF = -(-8 * D_MODEL // (3 * 256)) * 256
FF_TILE = 256
S5_UNROLL = 8
EPS = 1e-6
SEQ_PER_STEP = 8
STREAM_SKEW = 1
EXP_CAP = 60.0
LANE = 128
SEG_A = 3 * BR_WIDTH + BR_WIDTH + LANE
SEG_B = 4 * BR_WIDTH + LANE
SEG_C = BR_WIDTH
SEG_D = 4 * BR_WIDTH
SEG_OFF = (0, SEG_A, SEG_A + SEG_B, SEG_A + SEG_B + SEG_C, SEG_A + SEG_B + SEG_C + SEG_D)
N_MIX = SEG_OFF[-1]
PREP_A = 6 * BR_WIDTH
PREP_B = 5 * BR_WIDTH
VMEM_LIMIT = 56 * 1024 * 1024


def _cparams(*sem):
    return pltpu.CompilerParams(dimension_semantics=sem, vmem_limit_bytes=VMEM_LIMIT)


def _full(shape):
    n = len(shape)
    return pl.BlockSpec(shape, lambda *_: (0,) * n)


def _rms(x, w):
    return x * lax.rsqrt(jnp.mean(x * x, axis=-1, keepdims=True) + EPS) * w


def _sigmoid(x):
    return 1.0 / (1.0 + jnp.exp(-x))


def _silu(x):
    return x * _sigmoid(x)


def _softplus(x):
    return jnp.maximum(x, 0.0) + jnp.log(1.0 + jnp.exp(-jnp.abs(x)))


def _log_sigmoid(x):
    return -_softplus(-x)


def _gelu(x):
    return 0.5 * x * (1.0 + jnp.tanh(math.sqrt(2.0 / math.pi) * (x + 0.044715 * x * x * x)))


def _dot(a, b):
    return jnp.dot(a.astype(BF16), b.astype(BF16), preferred_element_type=F32)


def _dot_nt(a, b):
    return lax.dot_general(a.astype(BF16), b.astype(BF16), (((1,), (1,)), ((), ())), preferred_element_type=F32)


def _dot_tn(a, b):
    return lax.dot_general(a.astype(BF16), b.astype(BF16), (((0,), (0,)), ((), ())), preferred_element_type=F32)


def _split(x, terms):
    parts, rest = [], x
    for t in range(terms):
        p = rest.astype(BF16)
        parts.append(p)
        if t + 1 < terms:
            rest = rest - p.astype(F32)
    return parts


def _dot_xl(x, c, terms=3):
    return sum(jnp.dot(p, c, preferred_element_type=F32) for p in _split(x, terms))


def _dot_xr(c, x, terms=3):
    return sum(jnp.dot(c, p, preferred_element_type=F32) for p in _split(x, terms))


def _dot_nt_xl(x, c, terms=3):
    return sum(lax.dot_general(p, c, (((1,), (1,)), ((), ())), preferred_element_type=F32) for p in _split(x, terms))


def _iota2(shape, dim):
    return lax.broadcasted_iota(jnp.int32, shape, dim)


def _head_masks():
    r = _iota2((CHUNK, N_HEADS * CHUNK), 0)
    c = _iota2((CHUNK, N_HEADS * CHUNK), 1) & (CHUNK - 1)
    rr = _iota2((N_HEADS * CHUNK, N_HEADS * HEAD_DIM), 0)
    cc = _iota2((N_HEADS * CHUNK, N_HEADS * HEAD_DIM), 1)
    block = (rr >> 6) == (cc >> 6)
    one = lambda mask: jnp.where(mask, 1.0, 0.0)
    return dict(causal=r >= c, strict=r > c, eye=r == c, same_sub=(r >> 4) == (c >> 4), block=block,
                block16=one(block).astype(BF16), sub16=one((rr >> 4) == (cc >> 4)).astype(BF16),
                eye_sub=one(_iota2((SUB, BR_WIDTH), 0) == (_iota2((SUB, BR_WIDTH), 1) & (SUB - 1))))


def _chunk_tri(tt):
    r = _iota2((tt, tt), 0)
    c = _iota2((tt, tt), 1)
    return jnp.where((r >= c) & ((r >> 6) == (c >> 6)), 1.0, 0.0).astype(BF16)


def _chunk_ones(tt):
    r = _iota2((tt, tt), 0) >> 6
    c = _iota2((tt, tt), 1) >> 6
    return jnp.where(r == c, 1.0, 0.0).astype(BF16)


def _block_ones():
    return _chunk_ones(BR_WIDTH)


def _blockdiag(x, block16):
    return jnp.concatenate([x.astype(BF16)] * N_HEADS, axis=0) * block16


def _head_norm_gate(o, silu_gate, w, ones_blk):
    ms = _dot_xl(o * o, ones_blk, 2) * (1.0 / HEAD_DIM)
    return o * lax.rsqrt(ms + EPS) * w * silu_gate


def _in_proj_kernel(x_ref, nw_ref, w_ref, pa_ref, pb_ref, pc_ref, pd_ref):
    xn = _rms(x_ref[...], nw_ref[...]).astype(BF16)
    for i, ref in enumerate((pa_ref, pb_ref, pc_ref, pd_ref)):
        ref[...] = _dot(xn, w_ref[:, SEG_OFF[i]:SEG_OFF[i + 1]])


def _time_major_spec(tm, seq):
    nt = seq // tm
    return pl.BlockSpec((tm, BR_WIDTH), lambda i: (i % nt, i // nt))


def _in_proj_prompt_kernel(x_ref, halo_ref, nw_ref, w_ref, cw_ref, alog_ref, dtb_ref, gkw_ref, gkb_ref, lb_ref,
                           ga_ref, gb_ref, pc_ref, gd_ref, conv_ref, *, nt):
    w = BR_WIDTH
    seg = lambda i: w_ref[:, SEG_OFF[i]:SEG_OFF[i + 1]]
    xn = _rms(x_ref[...], nw_ref[...]).astype(BF16)
    pd = _dot(xn, seg(3))
    pa = _dot(xn, seg(0))
    hist = _dot(_rms(halo_ref[...], nw_ref[...]), w_ref[:, 0:3 * w])
    for j, val in enumerate(_hgrn_prep(pd, lb_ref)):
        gd_ref[:, j * w:(j + 1) * w] = val
    pb = _dot(xn, seg(1))
    raw = pa[:, 0:3 * w]
    hist = jnp.where((pl.program_id(0) % nt) == 0, 0.0, hist)
    rows = raw.shape[0]
    conv_ref[0] = raw[rows - (CONV_W - 1):rows, :]
    q, k, v = _gdn_conv_qkv(raw, hist, cw_ref)
    g, beta = _gdn_prep(pa[:, 4 * w:4 * w + LANE], alog_ref, dtb_ref)
    for j, val in enumerate((q, k, v, _silu(pa[:, 3 * w:4 * w]), g, beta)):
        ga_ref[:, j * w:(j + 1) * w] = val
    pc_ref[...] = _dot(xn, seg(2))
    for j, val in enumerate(_gla_prep(pb, gkw_ref, gkb_ref)):
        gb_ref[:, j * w:(j + 1) * w] = val


def _in_proj_prompt(x, nw, w, params, tm, seq):
    t = x.shape[0]
    nt = seq // tm
    bsz = t // seq
    row = lambda n: pl.BlockSpec((tm, n), lambda i: (i, 0))
    halo = pl.BlockSpec((8, D_MODEL), lambda i: (jnp.maximum(i * (tm // 8) - 1, 0), 0))
    return pl.pallas_call(
        functools.partial(_in_proj_prompt_kernel, nt=nt),
        grid=(t // tm,),
        in_specs=[row(D_MODEL), halo, _full((1, D_MODEL)), _full((D_MODEL, N_MIX))] + [_full(a.shape) for a in params],
        out_specs=[row(PREP_A), row(PREP_B), _time_major_spec(tm, seq), row(PREP_B),
                   pl.BlockSpec((1, CONV_W - 1, 3 * BR_WIDTH), lambda i: (i // nt, 0, 0))],
        out_shape=[jax.ShapeDtypeStruct((t, PREP_A), F32), jax.ShapeDtypeStruct((t, PREP_B), F32),
                   jax.ShapeDtypeStruct((seq, bsz * BR_WIDTH), F32), jax.ShapeDtypeStruct((t, PREP_B), F32),
                   jax.ShapeDtypeStruct((bsz, CONV_W - 1, 3 * BR_WIDTH), F32)],
        compiler_params=_cparams("arbitrary"),
        name="in_proj_prompt",
    )(x, x, nw, w, *params)


def _in_proj(x, nw, w, tm, seq):
    t = x.shape[0]
    widths = (SEG_A, SEG_B, SEG_C, SEG_D)
    out_specs = [pl.BlockSpec((tm, n), lambda i: (i, 0)) for n in widths]
    out_shape = [jax.ShapeDtypeStruct((t, n), F32) for n in widths]
    out_specs[2] = _time_major_spec(tm, seq)
    out_shape[2] = jax.ShapeDtypeStruct((seq, (t // seq) * BR_WIDTH), F32)
    return pl.pallas_call(
        _in_proj_kernel,
        grid=(t // tm,),
        in_specs=[pl.BlockSpec((tm, D_MODEL), lambda i: (i, 0)), _full((1, D_MODEL)), _full((D_MODEL, N_MIX))],
        out_specs=out_specs,
        out_shape=out_shape,
        compiler_params=_cparams("parallel"),
        name="in_proj",
    )(x, nw, w)


def _merge_kernel(h_ref, oa_ref, ob_ref, oc_ref, od_ref, *rest, head_norm):
    if head_norm:
        gates, gains, rest = rest[0:3], rest[3:6], rest[6:]
    nw_ref, wm_ref, wb_ref, wo_ref, out_ref = rest
    h = h_ref[...]
    xn = _rms(h, nw_ref[...]).astype(BF16)
    o_vals = [oa_ref[...], ob_ref[...], oc_ref[...], od_ref[...]]
    if head_norm:
        ones_blk = _block_ones()
        for k, g_ref, w_ref in zip((0, 1, 3), gates, gains):
            o_vals[k] = _head_norm_gate(o_vals[k], g_ref[...], w_ref[...], ones_blk)
    proj = lambda k: (_dot(xn, wm_ref[:, k * D_MODEL:(k + 1) * D_MODEL]), _dot(o_vals[k], wb_ref[k]))
    acc = jnp.zeros(h.shape, F32)
    nxt = proj(0)
    for k in range(N_BRANCH):
        gate, br = nxt
        if k + 1 < N_BRANCH:
            nxt = proj(k + 1)
        acc = acc + _sigmoid(gate) * br
    out_ref[...] = h + _dot(acc, wo_ref[...])


def _merge(h, oa, ob, oc, od, nw, wm, wb, wo, tm, seq, prepared=None, gains=None):
    t = h.shape[0]
    row = lambda n: pl.BlockSpec((tm, n), lambda i: (i, 0))
    head_norm = prepared is not None
    extra, extra_specs = (), []
    if head_norm:
        extra = tuple(prepared) + tuple(gains)
        extra_specs = [pl.BlockSpec((tm, BR_WIDTH), lambda i: (i, 3))] * 3 + [_full((1, BR_WIDTH))] * 3
    return pl.pallas_call(
        functools.partial(_merge_kernel, head_norm=head_norm),
        grid=(t // tm,),
        in_specs=[row(D_MODEL), row(BR_WIDTH), row(BR_WIDTH), _time_major_spec(tm, seq), row(BR_WIDTH)]
                 + extra_specs + [_full((1, D_MODEL)), _full((D_MODEL, N_BRANCH * D_MODEL)),
                                  _full((N_BRANCH, BR_WIDTH, D_MODEL)), _full((D_MODEL, D_MODEL))],
        out_specs=row(D_MODEL),
        out_shape=jax.ShapeDtypeStruct((t, D_MODEL), F32),
        compiler_params=_cparams("parallel"),
        name="merge",
    )(h, oa, ob, oc, od, *extra, nw, wm, wb, wo)


def _ffn_kernel(h_ref, nw_ref, wg_ref, wu_ref, wd_ref, fw_ref, out_ref, *, final):
    h = h_ref[...]
    hn = _rms(h, nw_ref[...]).astype(BF16)
    n_tiles = D_FF // FF_TILE
    cols = [slice(c * FF_TILE, (c + 1) * FF_TILE) for c in range(n_tiles)]
    up = lambda sl: (_dot(hn, wg_ref[:, sl]), _dot(hn, wu_ref[:, sl]))
    acc = jnp.zeros(h.shape, F32)
    nxt = up(cols[0])
    for c in range(n_tiles):
        g, u = nxt
        if c + 1 < n_tiles:
            nxt = up(cols[c + 1])
        acc = acc + _dot(_silu(g) * u, wd_ref[cols[c], :])
    hnew = h + acc
    out_ref[...] = _rms(hnew, fw_ref[...]) if final else hnew


def _ffn(h, nw, wg, wu, wd, fw, tm, final):
    t = h.shape[0]
    row = pl.BlockSpec((tm, D_MODEL), lambda i: (i, 0))
    return pl.pallas_call(
        functools.partial(_ffn_kernel, final=final),
        grid=(t // tm,),
        in_specs=[row, _full((1, D_MODEL)), _full((D_MODEL, D_FF)), _full((D_MODEL, D_FF)),
                  _full((D_FF, D_MODEL)), _full((1, D_MODEL))],
        out_specs=row,
        out_shape=jax.ShapeDtypeStruct((t, D_MODEL), F32),
        compiler_params=_cparams("parallel"),
        name="ffn",
    )(h, nw, wg, wu, wd, fw)


def _run_skewed(streams, skew):
    results = [None] * len(streams)
    live, step = [], 0
    pending = list(enumerate(streams))
    while pending or live:
        while pending and pending[0][0] * skew <= step:
            live.append(pending.pop(0))
        still = []
        for idx, gen in live:
            try:
                next(gen)
                still.append((idx, gen))
            except StopIteration as done:
                results[idx] = done.value
        live = still
        step += 1
    return results


def _gla_stream(load, st, m, tt):
    each = lambda f, *xs: [f(*a) for a in zip(*xs)]
    blk = m["block16"]
    b_all = _dot_xr(_chunk_tri(tt), load(4, slice(0, tt)), 3)
    yield
    rows = [slice(c * CHUNK, (c + 1) * CHUNK) for c in range(tt // CHUNK)]
    q, k, v = ([load(j, r) for r in rows] for j in range(3))
    b = [b_all[r, :] for r in rows]
    b_last = each(lambda x: x[CHUNK - 1:CHUNK, :], b)
    scores = []
    for i0 in range(0, CHUNK, SUB):
        q_i = each(lambda qq, x: qq[i0:i0 + SUB, :] * jnp.exp(x[i0:i0 + SUB, :] - x[i0:i0 + 1, :]), q, b)
        k_i = each(lambda kk, x: kk * jnp.exp(jnp.minimum(x[i0:i0 + 1, :] - x, EXP_CAP)), k, b)
        yield
        scores.append(each(lambda a, c: _dot_nt(a, _blockdiag(c, blk)), q_i, k_i))
    kv = each(lambda vv, kk, x, xl: jnp.where(m["block"], _dot_tn(vv, kk * jnp.exp(xl - x)), 0.0), v, k, b, b_last)
    yield
    attn = each(lambda *r: jnp.where(m["causal"], jnp.concatenate(r, axis=0), 0.0), *scores)
    qe = each(lambda qq, x: qq * jnp.exp(x), q, b)
    yield
    o_loc = each(lambda a, vv: _dot(a, _blockdiag(vv, blk)), attn, v)
    yield
    outs = []
    for c in range(len(q)):
        outs.append(_dot_nt(qe[c], st) + o_loc[c])
        st = st * jnp.exp(b_last[c]) + kv[c]
        yield
    return outs, st


def _write_state(out_ref, i, st):
    r = _iota2((BR_WIDTH, BR_WIDTH), 0)
    c = _iota2((BR_WIDTH, BR_WIDTH), 1)
    eye = jnp.where(r == c, 1.0, 0.0).astype(BF16)
    nt_dot = lambda p: lax.dot_general(eye, p, (((1,), (1,)), ((), ())), preferred_element_type=F32)
    st_t = sum(nt_dot(p) for p in _split(st, 3))
    for h in range(N_HEADS):
        out_ref[i, h] = st_t[h * HEAD_DIM:(h + 1) * HEAD_DIM, h * HEAD_DIM:(h + 1) * HEAD_DIM]


def _gla_prep(pb, gkw_ref, gkb_ref):
    w = BR_WIDTH
    lf = _log_sigmoid(_dot(pb[:, 4 * w:4 * w + LANE], gkw_ref[...]) + gkb_ref[...]) * (1.0 / GLA_GATE_NORM)
    return pb[:, 0:w] * HEAD_DIM ** -0.5, pb[:, w:2 * w], pb[:, 2 * w:3 * w], _silu(pb[:, 3 * w:4 * w]), lf


def _hgrn_prep(pd, lb_ref):
    w = BR_WIDTH
    lb = lb_ref[...]
    xf = pd[:, w:2 * w]
    return (_silu(pd[:, 0:w]) * HEAD_DIM ** -0.5, (1.0 - lb) * _sigmoid(-xf), pd[:, 2 * w:3 * w],
            _silu(pd[:, 3 * w:4 * w]), _hgrn_log_forget(xf, lb))


def _linear_prompt_kernel(p_ref, o_ref, sout_ref, st_s, *, tt):
    @pl.when(pl.program_id(1) == 0)
    def _():
        st_s[...] = jnp.zeros(st_s.shape, F32)

    w = BR_WIDTH
    m = _head_masks()
    n_seq = p_ref.shape[0]
    load = lambda i: (lambda j, r: p_ref[i, r, j * w:(j + 1) * w])
    results = _run_skewed([_gla_stream(load(i), st_s[i], m, tt) for i in range(n_seq)], STREAM_SKEW)
    for i, (outs, st) in enumerate(results):
        st_s[i] = st
        o_ref[i] = jnp.concatenate(outs, axis=0)

    @pl.when(pl.program_id(1) == pl.num_programs(1) - 1)
    def _():
        for i in range(n_seq):
            _write_state(sout_ref, i, st_s[i])


def _hgrn_log_forget(xf, lb):
    ls_pos = _log_sigmoid(xf)
    pos = lb > 0
    lb_safe = jnp.where(pos, lb, 1.0)
    mixed = ls_pos + _softplus(jnp.log(lb_safe) - xf)
    return jnp.where(pos, mixed, ls_pos)


def _mixer_call(kernel, p, bsz, seq, tt, name):
    nt = seq // tt
    width = p.shape[1]
    n_seq = SEQ_PER_STEP if bsz % SEQ_PER_STEP == 0 else 1
    o, st = pl.pallas_call(
        functools.partial(kernel, tt=tt),
        grid=(bsz // n_seq, nt),
        in_specs=[pl.BlockSpec((n_seq, tt, width), lambda b, t: (b, t, 0))],
        out_specs=[pl.BlockSpec((n_seq, tt, BR_WIDTH), lambda b, t: (b, t, 0)),
                   pl.BlockSpec((n_seq, N_HEADS, HEAD_DIM, HEAD_DIM), lambda b, t: (b, 0, 0, 0))],
        out_shape=[jax.ShapeDtypeStruct((bsz, seq, BR_WIDTH), F32),
                   jax.ShapeDtypeStruct((bsz, N_HEADS, HEAD_DIM, HEAD_DIM), F32)],
        scratch_shapes=[pltpu.VMEM((n_seq, BR_WIDTH, BR_WIDTH), F32)],
        compiler_params=_cparams("parallel", "arbitrary"),
        name=name,
    )(p.reshape(bsz, seq, width))
    return o.reshape(bsz * seq, BR_WIDTH), st


def _mm_split(a, b, tile_mask):
    a_hi, a_lo = _split(a, 2)
    b_hi, b_lo = (jnp.concatenate([p] * (BR_WIDTH // b.shape[0]), axis=0) * tile_mask for p in _split(b, 2))
    dot = functools.partial(jnp.dot, preferred_element_type=F32)
    return dot(a_hi, b_hi) + (dot(a_lo, b_hi) + dot(a_hi, b_lo))


def _unit_lower_inverse(n, m):
    each = lambda f, *xs: [f(*a) for a in zip(*xs)]
    eye_c, sub, blk = m["eye_sub"], m["sub16"], m["block16"]
    nd = each(lambda a: jnp.where(m["same_sub"], a, 0.0), n)
    low = each(lambda a, d: a - d, n, nd)
    c = each(lambda d: d[0:SUB] + d[SUB:2 * SUB] + d[2 * SUB:3 * SUB] + d[3 * SUB:4 * SUB], nd)
    p = each(lambda a: _mm_split(a, a, sub), c)
    x = each(lambda a: eye_c - a, c)
    yield
    for _ in range(2):
        xp = each(lambda a, b: _mm_split(jnp.concatenate([a, b], axis=0), b, sub), x, p)
        x = each(lambda a, b: a + b[0:SUB], x, xp)
        p = each(lambda b: b[SUB:2 * SUB], xp)
        yield
    x = each(lambda a, b: a + _mm_split(a, b, sub), x, p)
    dinv = each(lambda a: jnp.where(m["same_sub"], jnp.concatenate([a] * (CHUNK // SUB), axis=0), 0.0), x)
    yield
    mm = each(lambda a, b: _mm_split(a, b, blk), dinv, low)
    yield
    m2 = each(lambda a: _mm_split(a, a, blk), mm)
    yield
    e = each(lambda d, a: d + _mm_split(a, d, blk), dinv, m2)
    yield
    return each(lambda a, b: b - _mm_split(a, b, blk), mm, e)


def _gdn_stream(load, st, m, tt):
    each = lambda f, *xs: [f(*a) for a in zip(*xs)]
    w = BR_WIDTH
    blk = m["block16"]
    gam_all = _dot_xr(_chunk_tri(tt), load(4, slice(0, tt)), 3)
    yield
    lane_j = _iota2((tt, w), 1) & (CHUNK - 1)
    row_j = _iota2((tt, w), 0) & (CHUNK - 1)
    gam_t = _dot_xr(_chunk_ones(tt), jnp.where(lane_j == row_j, gam_all, 0.0), 3)
    yield
    rows = [slice(c * CHUNK, (c + 1) * CHUNK) for c in range(tt // CHUNK)]
    q, k, v, bexp = ([load(j, r) for r in rows] for j in (0, 1, 2, 5))
    gam_i = [gam_all[r, :] for r in rows]
    gam_j = [gam_t[r, :] for r in rows]
    decay = each(lambda gi, gj: jnp.where(m["causal"], jnp.exp(jnp.minimum(gi - gj, 0.0)), 0.0), gam_i, gam_j)
    scores = each(lambda kk, qq: _dot_nt(jnp.concatenate([kk, qq], axis=0), _blockdiag(kk, blk)), k, q)
    yield
    attn = each(lambda s, d: s[CHUNK:2 * CHUNK] * d, scores, decay)
    n = each(lambda b, s, d: jnp.where(m["strict"], b * s[0:CHUNK] * d, 0.0), bexp, scores, decay)
    tinv = yield from _unit_lower_inverse(n, m)
    yield
    u_v = each(lambda t, b, vv: _dot(t, _blockdiag(b * vv, blk)), tinv, bexp, v)
    w_k = each(lambda t, b, gi, kk: _dot(t, _blockdiag(b * jnp.exp(gi) * kk, blk)), tinv, bexp, gam_i, k)
    yield
    outs = []
    for c in range(len(rows)):
        gam_last = gam_i[c][CHUNK - 1:CHUNK, :]
        ws = _dot_nt(jnp.concatenate([w_k[c], q[c]], axis=0), st)
        u = u_v[c] - ws[0:CHUNK]
        yield
        outs.append(jnp.exp(gam_i[c]) * ws[CHUNK:2 * CHUNK] + _dot(attn[c], _blockdiag(u, blk)))
        kd = k[c] * jnp.exp(gam_last - gam_i[c])
        st = st * jnp.exp(gam_last) + jnp.where(m["block"], _dot_tn(u, kd), 0.0)
        yield
    return outs, st


def _l2norm_heads(x, ones_blk):
    return x * lax.rsqrt(_dot_xl(x * x, ones_blk, 2) + EPS)


def _gdn_prep(a_alpha_beta, alog_ref, dtb_ref):
    r = _iota2((LANE, BR_WIDTH), 0)
    c = _iota2((LANE, BR_WIDTH), 1) >> 6
    sel_a = jnp.where(r == c, 1.0, 0.0).astype(BF16)
    sel_b = jnp.where(r == c + N_HEADS, 1.0, 0.0).astype(BF16)
    alpha = _dot_xl(a_alpha_beta, sel_a, 3)
    beta = _sigmoid(_dot_xl(a_alpha_beta, sel_b, 3))
    g = -jnp.exp(alog_ref[...]) * _softplus(alpha + dtb_ref[...])
    return g, beta


def _gdn_conv_qkv(raw, hist, cw_ref):
    w = BR_WIDTH
    rows = raw.shape[0]
    xp = jnp.concatenate([hist, raw], axis=0)
    y = raw * cw_ref[CONV_W - 1:CONV_W, :]
    for j in range(CONV_W - 1):
        start = 8 - (CONV_W - 1) + j
        y = y + xp[start:start + rows, :] * cw_ref[j:j + 1, :]
    qkv = _silu(y)
    ones_blk = _block_ones()
    return (_l2norm_heads(qkv[:, 0:w], ones_blk) * HEAD_DIM ** -0.5, _l2norm_heads(qkv[:, w:2 * w], ones_blk),
            qkv[:, 2 * w:3 * w])


def _gdn_prompt_kernel(p_ref, o_ref, sout_ref, st_s, *, tt):
    t = pl.program_id(1)

    @pl.when(t == 0)
    def _():
        st_s[...] = jnp.zeros(st_s.shape, F32)

    w = BR_WIDTH
    m = _head_masks()
    n_seq = p_ref.shape[0]
    load = lambda i: (lambda j, r: p_ref[i, r, j * w:(j + 1) * w])
    results = _run_skewed([_gdn_stream(load(i), st_s[i], m, tt) for i in range(n_seq)], STREAM_SKEW)
    for i, (outs, st) in enumerate(results):
        st_s[i] = st
        o_ref[i] = jnp.concatenate(outs, axis=0)

    @pl.when(t == pl.num_programs(1) - 1)
    def _():
        for i in range(n_seq):
            _write_state(sout_ref, i, st_s[i])


def _s5_kernel(u_ref, x0r_ref, x0i_ref, ar_ref, ai_ref, br_ref, bi_ref, cr_ref, ci_ref, d_ref, glu_ref,
               o_ref, xr_ref, xi_ref, sr_s, si_s, *, steps, rows):
    @pl.when(pl.program_id(0) == 0)
    def _():
        xr_ref[...] = x0r_ref[...]
        xi_ref[...] = x0i_ref[...]

    u = u_ref[...]
    ub = u.astype(BF16)
    sr_s[...] = _dot(ub, br_ref[...])
    si_s[...] = _dot(ub, bi_ref[...])
    a_re = jnp.broadcast_to(ar_ref[...], (rows, S5_STATE))
    a_im = jnp.broadcast_to(ai_ref[...], (rows, S5_STATE))

    def body(t, carry):
        xr, xi = carry
        r = pl.ds(pl.multiple_of(t * rows, rows), rows)
        nr = (sr_s[r, :] + a_re * xr) - a_im * xi
        ni = (si_s[r, :] + a_re * xi) + a_im * xr
        sr_s[r, :] = nr
        si_s[r, :] = ni
        return nr, ni

    xr, xi = lax.fori_loop(0, steps, body, (xr_ref[...], xi_ref[...]), unroll=min(steps, S5_UNROLL))
    xr_ref[...] = xr
    xi_ref[...] = xi
    y = _dot(sr_s[...].astype(BF16), cr_ref[...]) - _dot(si_s[...].astype(BF16), ci_ref[...]) + d_ref[...] * u
    hg = _dot(_gelu(y).astype(BF16), glu_ref[...])
    o_ref[...] = hg[:, 0:BR_WIDTH] * _sigmoid(hg[:, BR_WIDTH:2 * BR_WIDTH])


def _s5(u, x0r, x0i, params, steps_total, rows, steps):
    nt = steps_total // steps
    tile = steps * rows
    return pl.pallas_call(
        functools.partial(_s5_kernel, steps=steps, rows=rows),
        grid=(nt,),
        in_specs=[pl.BlockSpec((tile, BR_WIDTH), lambda t: (t, 0)), _full(x0r.shape), _full(x0i.shape)]
                 + [_full(a.shape) for a in params],
        out_specs=[pl.BlockSpec((tile, BR_WIDTH), lambda t: (t, 0)), _full(x0r.shape), _full(x0i.shape)],
        out_shape=[jax.ShapeDtypeStruct((steps_total * rows, BR_WIDTH), F32),
                   jax.ShapeDtypeStruct(x0r.shape, F32), jax.ShapeDtypeStruct(x0i.shape, F32)],
        scratch_shapes=[pltpu.VMEM((tile, S5_STATE), F32)] * 2,
        compiler_params=_cparams("arbitrary"),
        name="s5",
    )(u, x0r, x0i, *params)


def _s5_params(lam_re, lam_im, b_re, b_im, c_re, c_im, d, log_dt, glu_w):
    lr = jnp.minimum(lam_re, -1e-4)
    li = lam_im
    dt = jnp.exp(log_dt)[:, None]
    mag = jnp.exp(lr * dt)
    ab_re, ab_im = mag * jnp.cos(li * dt), mag * jnp.sin(li * dt)
    den = lr * lr + li * li
    z_re = ((ab_re - 1.0) * lr + ab_im * li) / den
    z_im = (ab_im * lr - (ab_re - 1.0) * li) / den
    bb_re = z_re[..., None] * b_re - z_im[..., None] * b_im
    bb_im = z_re[..., None] * b_im + z_im[..., None] * b_re
    eye = jnp.eye(S5_GROUPS, dtype=F32)

    def pack_in(bb):
        return jnp.einsum('gph,gk->ghkp', bb, eye).reshape(BR_WIDTH, S5_STATE).astype(BF16)

    def pack_out(c):
        return jnp.einsum('ghp,gk->gpkh', c, eye).reshape(S5_STATE, BR_WIDTH).astype(BF16)

    return (ab_re.reshape(1, S5_STATE), ab_im.reshape(1, S5_STATE), pack_in(bb_re), pack_in(bb_im),
            pack_out(c_re), pack_out(c_im), d.reshape(1, BR_WIDTH), glu_w.astype(BF16))


def _expand_mats():
    r = _iota2((HEAD_DIM, HEAD_DIM * HEAD_DIM), 0)
    c = _iota2((HEAD_DIM, HEAD_DIM * HEAD_DIM), 1)
    rep_k = jnp.where(r == (c >> 6), 1.0, 0.0).astype(BF16)
    rep_v = jnp.where(r == (c & (HEAD_DIM - 1)), 1.0, 0.0).astype(BF16)
    return rep_k, rep_v


def _gla_decode_kernel(q_ref, k_ref, v_ref, lf_ref, s_ref, o_ref, sn_ref):
    rep_k, rep_v = _expand_mats()
    q, k, v, lf = q_ref[0, 0], k_ref[0, 0], v_ref[0, 0], lf_ref[0, 0]
    s = s_ref[...]
    dec = jnp.exp(lf)
    qe_x = _dot_xl(q * dec, rep_k, 1)
    o = _dot_nt_xl(qe_x * s, rep_v, 2) + jnp.sum(q * k, axis=-1, keepdims=True) * v
    sn_ref[...] = _dot_xl(dec, rep_k, 3) * s + _dot_xl(k, rep_k, 1) * _dot_xl(v, rep_v, 1)
    o_ref[0] = o


def _gdn_decode_kernel(q_ref, k_ref, v_ref, g_ref, b_ref, s_ref, o_ref, sn_ref):
    rep_k, rep_v = _expand_mats()
    q, k, v = q_ref[0, 0], k_ref[0, 0], v_ref[0, 0]
    eg = jnp.exp(g_ref[0, 0])
    beta = b_ref[0, 0]
    s = s_ref[...]
    k_x = _dot_xl(k, rep_k, 1)
    ks = _dot_nt_xl(k_x * s, rep_v, 2)
    qs = _dot_nt_xl(_dot_xl(q, rep_k, 1) * s, rep_v, 2)
    u = beta * v - (beta * eg) * ks
    o_ref[0] = eg * qs + jnp.sum(q * k, axis=-1, keepdims=True) * u
    sn_ref[...] = eg * s + k_x * _dot_xl(u, rep_v, 1)


def _decode_call(kernel, packs, state, name):
    rows = state.shape[0]
    hw = HEAD_DIM * HEAD_DIM
    vec_specs, vecs = [], []
    for a in packs:
        for j in range(a.shape[0]):
            vec_specs.append(pl.BlockSpec((1, 1, rows, a.shape[3]), lambda h, j=j: (j, h, 0, 0)))
            vecs.append(a)
    return pl.pallas_call(
        kernel,
        grid=(N_HEADS,),
        in_specs=vec_specs + [pl.BlockSpec((rows, hw), lambda h: (0, h))],
        out_specs=[pl.BlockSpec((1, rows, HEAD_DIM), lambda h: (h, 0, 0)), pl.BlockSpec((rows, hw), lambda h: (0, h))],
        out_shape=[jax.ShapeDtypeStruct((N_HEADS, rows, HEAD_DIM), F32), jax.ShapeDtypeStruct(state.shape, F32)],
        compiler_params=_cparams("parallel"),
        name=name,
    )(*vecs, state)


def _put_heads(ref, j, x):
    for h in range(N_HEADS):
        ref[j, h] = x[:, h * HEAD_DIM:(h + 1) * HEAD_DIM]


def _decode_prep_kernel(pa_ref, pb_ref, pd_ref, conv_ref, cw_ref, alog_ref, dtb_ref, gkw_ref, gkb_ref, lb_ref,
                        a_ref, ag_ref, b_ref, d_ref, convn_ref):
    w = BR_WIDTH
    cw = 3 * w
    ones_blk = _block_ones()
    raw = pa_ref[:, 0:cw]
    y = raw * cw_ref[CONV_W - 1:CONV_W, :]
    for j in range(CONV_W - 1):
        y = y + conv_ref[:, j * cw:(j + 1) * cw] * cw_ref[j:j + 1, :]
    convn_ref[:, 0:(CONV_W - 2) * cw] = conv_ref[:, cw:(CONV_W - 1) * cw]
    convn_ref[:, (CONV_W - 2) * cw:(CONV_W - 1) * cw] = raw
    qkv = _silu(y)
    _put_heads(a_ref, 0, _l2norm_heads(qkv[:, 0:w], ones_blk) * HEAD_DIM ** -0.5)
    _put_heads(a_ref, 1, _l2norm_heads(qkv[:, w:2 * w], ones_blk))
    _put_heads(a_ref, 2, qkv[:, 2 * w:3 * w])
    g, beta = _gdn_prep(pa_ref[:, 4 * w:4 * w + LANE], alog_ref, dtb_ref)
    for h in range(N_HEADS):
        ag_ref[0, h] = g[:, h * HEAD_DIM:h * HEAD_DIM + 1]
        ag_ref[1, h] = beta[:, h * HEAD_DIM:h * HEAD_DIM + 1]
    for ref, (q, k, v, _, lf) in ((b_ref, _gla_prep(pb_ref[...], gkw_ref, gkb_ref)),
                                  (d_ref, _hgrn_prep(pd_ref[...], lb_ref))):
        for j, val in enumerate((q, k, v, lf)):
            _put_heads(ref, j, val)


def _decode_prep(pa, pb, pd, conv, params):
    rows = pa.shape[0]
    ins = (pa, pb, pd, conv) + tuple(params)
    per_head = lambda n, width: (n, N_HEADS, rows, width)
    shapes = [per_head(3, HEAD_DIM), per_head(2, 1), per_head(4, HEAD_DIM), per_head(4, HEAD_DIM), conv.shape]
    return pl.pallas_call(
        _decode_prep_kernel,
        grid=(1,),
        in_specs=[_full(a.shape) for a in ins],
        out_specs=[_full(s) for s in shapes],
        out_shape=[jax.ShapeDtypeStruct(s, F32) for s in shapes],
        compiler_params=_cparams("arbitrary"),
        name="decode_prep",
    )(*ins)


def _decode_post_kernel(oa_ref, ob_ref, od_ref, pa_ref, pb_ref, pd_ref, nwa_ref, nwb_ref, nwd_ref,
                        a_ref, b_ref, d_ref):
    w = BR_WIDTH
    ones_blk = _block_ones()
    heads = lambda ref: jnp.concatenate([ref[h] for h in range(N_HEADS)], axis=1)
    for out, o, p, nw in ((a_ref, oa_ref, pa_ref, nwa_ref), (b_ref, ob_ref, pb_ref, nwb_ref),
                          (d_ref, od_ref, pd_ref, nwd_ref)):
        out[...] = _head_norm_gate(heads(o), _silu(p[:, 3 * w:4 * w]), nw[...], ones_blk)


def _decode_post(oa, ob, od, pa, pb, pd, nwa, nwb, nwd):
    ins = (oa, ob, od, pa, pb, pd, nwa, nwb, nwd)
    shp = (oa.shape[1], BR_WIDTH)
    return pl.pallas_call(
        _decode_post_kernel,
        grid=(1,),
        in_specs=[_full(a.shape) for a in ins],
        out_specs=[_full(shp)] * 3,
        out_shape=[jax.ShapeDtypeStruct(shp, F32)] * 3,
        compiler_params=_cparams("arbitrary"),
        name="decode_post",
    )(*ins)


def _pack_w_in(w_in):
    sizes = (3 * BR_WIDTH, N_HEADS, N_HEADS, BR_WIDTH, BR_WIDTH, BR_WIDTH, BR_WIDTH, GLA_GATE_RANK, BR_WIDTH,
             BR_WIDTH, BR_WIDTH, BR_WIDTH, BR_WIDTH, BR_WIDTH, N_BRANCH * D_MODEL)
    w_in = w_in.astype(BF16)
    parts, off = [], 0
    for n in sizes:
        parts.append(w_in[..., off:off + n])
        off += n
    (a_qkv, a_alpha, a_beta, a_gate, b_q, b_k, b_v, b_gk, b_gate, c_u, d_q, d_f, d_i, d_gate, merge) = parts
    zpad = lambda n: jnp.zeros(w_in.shape[:-1] + (n,), BF16)
    mix = jnp.concatenate([a_qkv, a_gate, a_alpha, a_beta, zpad(LANE - 2 * N_HEADS),
                           b_q, b_k, b_v, b_gate, b_gk, zpad(LANE - GLA_GATE_RANK),
                           c_u, d_q, d_f, d_i, d_gate], axis=-1)
    return mix, merge


def _tile_sizes(seq):
    pick = lambda n: n if seq % n == 0 else seq
    return pick(512), pick(256), pick(128)


def _hgrn_lower_bounds(logits):
    p = jax.nn.softmax(logits, axis=0)
    return jnp.cumsum(p, axis=0) - p[0]


def _row(x):
    return x.reshape(1, -1)


def _rep_heads(x):
    return jnp.repeat(x, HEAD_DIM).reshape(1, BR_WIDTH)


def kernel(x_prompt, x_sample, state_gdn_conv, state_gdn, state_gla, state_s5_re, state_s5_im, state_hgrn, norm1_w, w_in, gdn_conv_w, gdn_a_log, gdn_dt_bias, gdn_norm_w, gla_gk_w, gla_gk_b, gla_norm_w, s5_lambda_re, s5_lambda_im, s5_b_re, s5_b_im, s5_c_re, s5_c_im, s5_d, s5_log_dt, s5_glu_w, hgrn_lb_logits, hgrn_norm_w, w_branch, w_out, norm2_w, ffn_w_gate, ffn_w_up, ffn_w_down, final_norm_w):
    bsz, seq, _ = x_prompt.shape
    dbs = x_sample.shape[0]
    tm_p, tt, s5_steps = _tile_sizes(seq)
    lb_all = _hgrn_lower_bounds(hgrn_lb_logits)
    w_mix_all, w_merge_all = _pack_w_in(w_in)
    gkw_all = jnp.concatenate([gla_gk_w, jnp.zeros((DEPTH, LANE - GLA_GATE_RANK, BR_WIDTH), F32)], axis=1)
    wb_all, wo_all = w_branch.astype(BF16), w_out.astype(BF16)
    wg_all, wu_all, wd_all = ffn_w_gate.astype(BF16), ffn_w_up.astype(BF16), ffn_w_down.astype(BF16)
    hp = x_prompt.reshape(bsz * seq, D_MODEL)
    hs = x_sample.reshape(dbs, D_MODEL)
    zeros_p = jnp.zeros((bsz, S5_STATE), F32)
    fw = _row(final_norm_w)
    outs_p = [[] for _ in range(6)]
    outs_s = [[] for _ in range(6)]
    for i in range(DEPTH):
        final = i == DEPTH - 1
        w_mix, w_merge, gkw = w_mix_all[i], w_merge_all[i], gkw_all[i]
        nw1, nw2 = _row(norm1_w[i]), _row(norm2_w[i])
        gdn_params = (gdn_conv_w[i], _rep_heads(gdn_a_log[i]), _rep_heads(gdn_dt_bias[i]), _row(gdn_norm_w[i]))
        gla_params = (gkw, _row(gla_gk_b[i]), _row(gla_norm_w[i]))
        hgrn_params = (_row(lb_all[i]), _row(hgrn_norm_w[i]))
        s5_params = _s5_params(s5_lambda_re[i], s5_lambda_im[i], s5_b_re[i], s5_b_im[i], s5_c_re[i], s5_c_im[i],
                               s5_d[i], s5_log_dt[i], s5_glu_w[i])
        prep_params = (gdn_conv_w[i], gdn_params[1], gdn_params[2], gkw, gla_params[1], hgrn_params[0])
        wb, wo, wg, wu, wd = wb_all[i], wo_all[i], wg_all[i], wu_all[i], wd_all[i]

        ga, gb, pc, gd, conv_p = _in_proj_prompt(hp, nw1, w_mix, prep_params, tm_p, seq)
        o_a, st_a = _mixer_call(_gdn_prompt_kernel, ga, bsz, seq, tt, "gdn_prompt")
        o_b, st_b = _mixer_call(_linear_prompt_kernel, gb, bsz, seq, tt, "gla_prompt")
        o_d, st_d = _mixer_call(_linear_prompt_kernel, gd, bsz, seq, tt, "hgrn_prompt")
        o_c, xr_p, xi_p = _s5(pc.reshape(seq * bsz, BR_WIDTH), zeros_p, zeros_p, s5_params, seq, bsz, s5_steps)
        hp = _merge(hp, o_a, o_b, o_c.reshape(seq, bsz * BR_WIDTH), o_d, nw1, w_merge, wb, wo, tm_p, seq,
                    prepared=(ga, gb, gd), gains=(gdn_params[3], gla_params[2], hgrn_params[1]))
        hp = _ffn(hp, nw2, wg, wu, wd, fw, tm_p, final)
        for lst, s in zip(outs_p, (conv_p, st_a, st_b, xr_p.reshape(bsz, S5_GROUPS, S5_P),
                                   xi_p.reshape(bsz, S5_GROUPS, S5_P), st_d)):
            lst.append(s)

        pa, pb, pc, pd = _in_proj(hs, nw1, w_mix, dbs, dbs)
        va, vg, vb, vd, conv_s = _decode_prep(
            pa, pb, pd, state_gdn_conv[i].reshape(dbs, (CONV_W - 1) * 3 * BR_WIDTH), prep_params)
        flat = lambda s: s.reshape(dbs, N_HEADS * HEAD_DIM * HEAD_DIM)
        oa_h, sa = _decode_call(_gdn_decode_kernel, [va, vg], flat(state_gdn[i]), "gdn_decode")
        ob_h, sb = _decode_call(_gla_decode_kernel, [vb], flat(state_gla[i]), "gla_decode")
        od_h, sd = _decode_call(_gla_decode_kernel, [vd], flat(state_hgrn[i]), "hgrn_decode")
        o_a, o_b, o_d = _decode_post(oa_h, ob_h, od_h, pa, pb, pd, gdn_params[3], gla_params[2], hgrn_params[1])
        o_c, xr_s, xi_s = _s5(pc, state_s5_re[i].reshape(dbs, S5_STATE), state_s5_im[i].reshape(dbs, S5_STATE),
                              s5_params, 1, dbs, 1)
        hs = _merge(hs, o_a, o_b, o_c, o_d, nw1, w_merge, wb, wo, dbs, dbs)
        hs = _ffn(hs, nw2, wg, wu, wd, fw, dbs, final)
        st5 = lambda s: s.reshape(dbs, N_HEADS, HEAD_DIM, HEAD_DIM)
        for lst, s in zip(outs_s, (conv_s.reshape(dbs, CONV_W - 1, 3 * BR_WIDTH), st5(sa), st5(sb),
                                   xr_s.reshape(dbs, S5_GROUPS, S5_P), xi_s.reshape(dbs, S5_GROUPS, S5_P), st5(sd))):
            lst.append(s)

    return (hp.reshape(bsz, seq, D_MODEL), hs.reshape(dbs, 1, D_MODEL),
            *[jnp.stack(l) for l in outs_p], *[jnp.stack(l) for l in outs_s])
```

```python
import functools
import math

import jax
import jax.numpy as jnp
from jax import lax
from jax.experimental import pallas as pl
from jax.experimental.pallas import tpu as pltpu

F32 = jnp.float32
BF16 = jnp.bfloat16

D_MODEL = 1024
DEPTH = 4
N_BRANCH = 4
BR_WIDTH = D_MODEL // N_BRANCH
HEAD_DIM = 64
N_HEADS = BR_WIDTH // HEAD_DIM
CONV_W = 4
CHUNK = 64
SUB = 16
GLA_GATE_RANK = 16
GLA_GATE_NORM = 16.0
S5_GROUP = 16
S5_GROUPS = BR_WIDTH // S5_GROUP
S5_P = 64
S5_STATE = S5_GROUPS * S5_P
D_FF = -(-8 * D_MODEL // (3 * 256)) * 256
FF_TILE = 256
S5_UNROLL = 8
EPS = 1e-6
SEQ_PER_STEP = 8
STREAM_SKEW = 1
EXP_CAP = 60.0
LANE = 128
SUBLANE = 8
HEAD_SHIFT = HEAD_DIM.bit_length() - 1
CHUNK_SHIFT = CHUNK.bit_length() - 1
SUB_SHIFT = SUB.bit_length() - 1
SEG_A = 3 * BR_WIDTH + BR_WIDTH + LANE
SEG_B = 4 * BR_WIDTH + LANE
SEG_C = BR_WIDTH
SEG_D = 4 * BR_WIDTH
SEG_OFF = (0, SEG_A, SEG_A + SEG_B, SEG_A + SEG_B + SEG_C, SEG_A + SEG_B + SEG_C + SEG_D)
N_MIX = SEG_OFF[-1]
PREP_A = 6 * BR_WIDTH
PREP_B = 5 * BR_WIDTH
VMEM_LIMIT = 56 * 1024 * 1024


def _cparams(*sem):
    return pltpu.CompilerParams(dimension_semantics=sem, vmem_limit_bytes=VMEM_LIMIT)


def _full(shape):
    n = len(shape)
    return pl.BlockSpec(shape, lambda *_: (0,) * n)


def _rms(x, w):
    return x * lax.rsqrt(jnp.mean(x * x, axis=-1, keepdims=True) + EPS) * w


def _sigmoid(x):
    return 1.0 / (1.0 + jnp.exp(-x))


def _silu(x):
    return x * _sigmoid(x)


def _softplus(x):
    return jnp.maximum(x, 0.0) + jnp.log(1.0 + jnp.exp(-jnp.abs(x)))


def _log_sigmoid(x):
    return -_softplus(-x)


def _gelu(x):
    return 0.5 * x * (1.0 + jnp.tanh(math.sqrt(2.0 / math.pi) * (x + 0.044715 * x * x * x)))


def _dot(a, b):
    return jnp.dot(a.astype(BF16), b.astype(BF16), preferred_element_type=F32)


def _dot_nt(a, b):
    return lax.dot_general(a.astype(BF16), b.astype(BF16), (((1,), (1,)), ((), ())), preferred_element_type=F32)


def _dot_tn(a, b):
    return lax.dot_general(a.astype(BF16), b.astype(BF16), (((0,), (0,)), ((), ())), preferred_element_type=F32)


def _split(x, terms):
    parts, rest = [], x
    for t in range(terms):
        p = rest.astype(BF16)
        parts.append(p)
        if t + 1 < terms:
            rest = rest - p.astype(F32)
    return parts


def _dot_xl(x, c, terms=3):
    return sum(jnp.dot(p, c, preferred_element_type=F32) for p in _split(x, terms))


def _dot_xr(c, x, terms=3):
    return sum(jnp.dot(c, p, preferred_element_type=F32) for p in _split(x, terms))


def _dot_nt_xl(x, c, terms=3):
    return sum(lax.dot_general(p, c, (((1,), (1,)), ((), ())), preferred_element_type=F32) for p in _split(x, terms))


def _iota2(shape, dim):
    return lax.broadcasted_iota(jnp.int32, shape, dim)


def _head_masks():
    r = _iota2((CHUNK, N_HEADS * CHUNK), 0)
    c = _iota2((CHUNK, N_HEADS * CHUNK), 1) & (CHUNK - 1)
    rr = _iota2((N_HEADS * CHUNK, N_HEADS * HEAD_DIM), 0)
    cc = _iota2((N_HEADS * CHUNK, N_HEADS * HEAD_DIM), 1)
    block = (rr >> CHUNK_SHIFT) == (cc >> HEAD_SHIFT)
    one = lambda mask: jnp.where(mask, 1.0, 0.0)
    return dict(causal=r >= c, strict=r > c, eye=r == c, same_sub=(r >> SUB_SHIFT) == (c >> SUB_SHIFT), block=block,
                block16=one(block).astype(BF16), sub16=one((rr >> SUB_SHIFT) == (cc >> SUB_SHIFT)).astype(BF16),
                eye_sub=one(_iota2((SUB, BR_WIDTH), 0) == (_iota2((SUB, BR_WIDTH), 1) & (SUB - 1))))


def _chunk_tri(tt):
    r = _iota2((tt, tt), 0)
    c = _iota2((tt, tt), 1)
    return jnp.where((r >= c) & ((r >> CHUNK_SHIFT) == (c >> CHUNK_SHIFT)), 1.0, 0.0).astype(BF16)


def _chunk_ones(tt):
    r = _iota2((tt, tt), 0) >> CHUNK_SHIFT
    c = _iota2((tt, tt), 1) >> CHUNK_SHIFT
    return jnp.where(r == c, 1.0, 0.0).astype(BF16)


def _block_ones():
    return _chunk_ones(BR_WIDTH)


def _blockdiag(x, block16):
    return jnp.concatenate([x.astype(BF16)] * N_HEADS, axis=0) * block16


def _head_norm_gate(o, silu_gate, w, ones_blk):
    ms = _dot_xl(o * o, ones_blk, 2) * (1.0 / HEAD_DIM)
    return o * lax.rsqrt(ms + EPS) * w * silu_gate


def _in_proj_kernel(x_ref, nw_ref, w_ref, pa_ref, pb_ref, pc_ref, pd_ref):
    xn = _rms(x_ref[...], nw_ref[...]).astype(BF16)
    for i, ref in enumerate((pa_ref, pb_ref, pc_ref, pd_ref)):
        ref[...] = _dot(xn, w_ref[:, SEG_OFF[i]:SEG_OFF[i + 1]])


def _time_major_spec(tm, seq):
    nt = seq // tm
    return pl.BlockSpec((tm, BR_WIDTH), lambda i: (i % nt, i // nt))


def _in_proj_prompt_kernel(x_ref, halo_ref, nw_ref, w_ref, cw_ref, alog_ref, dtb_ref, gkw_ref, gkb_ref, lb_ref,
                           ga_ref, gb_ref, pc_ref, gd_ref, conv_ref, *, nt):
    w = BR_WIDTH
    seg = lambda i: w_ref[:, SEG_OFF[i]:SEG_OFF[i + 1]]
    xn = _rms(x_ref[...], nw_ref[...]).astype(BF16)
    pd = _dot(xn, seg(3))
    pa = _dot(xn, seg(0))
    hist = _dot(_rms(halo_ref[...], nw_ref[...]), w_ref[:, 0:3 * w])
    for j, val in enumerate(_hgrn_prep(pd, lb_ref)):
        gd_ref[:, j * w:(j + 1) * w] = val
    pb = _dot(xn, seg(1))
    raw = pa[:, 0:3 * w]
    hist = jnp.where((pl.program_id(0) % nt) == 0, 0.0, hist)
    rows = raw.shape[0]
    conv_ref[0] = raw[rows - (CONV_W - 1):rows, :]
    q, k, v = _gdn_conv_qkv(raw, hist, cw_ref)
    g, beta = _gdn_prep(pa[:, 4 * w:4 * w + LANE], alog_ref, dtb_ref)
    for j, val in enumerate((q, k, v, _silu(pa[:, 3 * w:4 * w]), g, beta)):
        ga_ref[:, j * w:(j + 1) * w] = val
    pc_ref[...] = _dot(xn, seg(2))
    for j, val in enumerate(_gla_prep(pb, gkw_ref, gkb_ref)):
        gb_ref[:, j * w:(j + 1) * w] = val


def _in_proj_prompt(x, nw, w, params, tm, seq):
    t = x.shape[0]
    nt = seq // tm
    bsz = t // seq
    row = lambda n: pl.BlockSpec((tm, n), lambda i: (i, 0))
    halo = pl.BlockSpec((SUBLANE, D_MODEL), lambda i: (jnp.maximum(i * (tm // SUBLANE) - 1, 0), 0))
    return pl.pallas_call(
        functools.partial(_in_proj_prompt_kernel, nt=nt),
        grid=(t // tm,),
        in_specs=[row(D_MODEL), halo, _full((1, D_MODEL)), _full((D_MODEL, N_MIX))] + [_full(a.shape) for a in params],
        out_specs=[row(PREP_A), row(PREP_B), _time_major_spec(tm, seq), row(PREP_B),
                   pl.BlockSpec((1, CONV_W - 1, 3 * BR_WIDTH), lambda i: (i // nt, 0, 0))],
        out_shape=[jax.ShapeDtypeStruct((t, PREP_A), F32), jax.ShapeDtypeStruct((t, PREP_B), F32),
                   jax.ShapeDtypeStruct((seq, bsz * BR_WIDTH), F32), jax.ShapeDtypeStruct((t, PREP_B), F32),
                   jax.ShapeDtypeStruct((bsz, CONV_W - 1, 3 * BR_WIDTH), F32)],
        compiler_params=_cparams("arbitrary"),
        name="in_proj_prompt",
    )(x, x, nw, w, *params)


def _in_proj(x, nw, w, tm, seq):
    t = x.shape[0]
    widths = (SEG_A, SEG_B, SEG_C, SEG_D)
    out_specs = [pl.BlockSpec((tm, n), lambda i: (i, 0)) for n in widths]
    out_shape = [jax.ShapeDtypeStruct((t, n), F32) for n in widths]
    out_specs[2] = _time_major_spec(tm, seq)
    out_shape[2] = jax.ShapeDtypeStruct((seq, (t // seq) * BR_WIDTH), F32)
    return pl.pallas_call(
        _in_proj_kernel,
        grid=(t // tm,),
        in_specs=[pl.BlockSpec((tm, D_MODEL), lambda i: (i, 0)), _full((1, D_MODEL)), _full((D_MODEL, N_MIX))],
        out_specs=out_specs,
        out_shape=out_shape,
        compiler_params=_cparams("parallel"),
        name="in_proj",
    )(x, nw, w)


def _merge_kernel(h_ref, oa_ref, ob_ref, oc_ref, od_ref, *rest, head_norm):
    if head_norm:
        gates, gains, rest = rest[0:3], rest[3:6], rest[6:]
    nw_ref, wm_ref, wb_ref, wo_ref, out_ref = rest
    h = h_ref[...]
    xn = _rms(h, nw_ref[...]).astype(BF16)
    o_vals = [oa_ref[...], ob_ref[...], oc_ref[...], od_ref[...]]
    if head_norm:
        ones_blk = _block_ones()
        for k, g_ref, w_ref in zip((0, 1, 3), gates, gains):
            o_vals[k] = _head_norm_gate(o_vals[k], g_ref[...], w_ref[...], ones_blk)
    proj = lambda k: (_dot(xn, wm_ref[:, k * D_MODEL:(k + 1) * D_MODEL]), _dot(o_vals[k], wb_ref[k]))
    acc = jnp.zeros(h.shape, F32)
    nxt = proj(0)
    for k in range(N_BRANCH):
        gate, br = nxt
        if k + 1 < N_BRANCH:
            nxt = proj(k + 1)
        acc = acc + _sigmoid(gate) * br
    out_ref[...] = h + _dot(acc, wo_ref[...])


def _merge(h, oa, ob, oc, od, nw, wm, wb, wo, tm, seq, prepared=None, gains=None):
    t = h.shape[0]
    row = lambda n: pl.BlockSpec((tm, n), lambda i: (i, 0))
    head_norm = prepared is not None
    extra, extra_specs = (), []
    if head_norm:
        extra = tuple(prepared) + tuple(gains)
        extra_specs = [pl.BlockSpec((tm, BR_WIDTH), lambda i: (i, 3))] * 3 + [_full((1, BR_WIDTH))] * 3
    return pl.pallas_call(
        functools.partial(_merge_kernel, head_norm=head_norm),
        grid=(t // tm,),
        in_specs=[row(D_MODEL), row(BR_WIDTH), row(BR_WIDTH), _time_major_spec(tm, seq), row(BR_WIDTH)]
                 + extra_specs + [_full((1, D_MODEL)), _full((D_MODEL, N_BRANCH * D_MODEL)),
                                  _full((N_BRANCH, BR_WIDTH, D_MODEL)), _full((D_MODEL, D_MODEL))],
        out_specs=row(D_MODEL),
        out_shape=jax.ShapeDtypeStruct((t, D_MODEL), F32),
        compiler_params=_cparams("parallel"),
        name="merge",
    )(h, oa, ob, oc, od, *extra, nw, wm, wb, wo)


def _ffn_kernel(h_ref, nw_ref, wg_ref, wu_ref, wd_ref, fw_ref, out_ref, *, final):
    h = h_ref[...]
    hn = _rms(h, nw_ref[...]).astype(BF16)
    n_tiles = D_FF // FF_TILE
    cols = [slice(c * FF_TILE, (c + 1) * FF_TILE) for c in range(n_tiles)]
    up = lambda sl: (_dot(hn, wg_ref[:, sl]), _dot(hn, wu_ref[:, sl]))
    acc = jnp.zeros(h.shape, F32)
    nxt = up(cols[0])
    for c in range(n_tiles):
        g, u = nxt
        if c + 1 < n_tiles:
            nxt = up(cols[c + 1])
        acc = acc + _dot(_silu(g) * u, wd_ref[cols[c], :])
    hnew = h + acc
    out_ref[...] = _rms(hnew, fw_ref[...]) if final else hnew


def _ffn(h, nw, wg, wu, wd, fw, tm, final):
    t = h.shape[0]
    row = pl.BlockSpec((tm, D_MODEL), lambda i: (i, 0))
    return pl.pallas_call(
        functools.partial(_ffn_kernel, final=final),
        grid=(t // tm,),
        in_specs=[row, _full((1, D_MODEL)), _full((D_MODEL, D_FF)), _full((D_MODEL, D_FF)),
                  _full((D_FF, D_MODEL)), _full((1, D_MODEL))],
        out_specs=row,
        out_shape=jax.ShapeDtypeStruct((t, D_MODEL), F32),
        compiler_params=_cparams("parallel"),
        name="ffn",
    )(h, nw, wg, wu, wd, fw)


def _run_skewed(streams, skew):
    results = [None] * len(streams)
    live, step = [], 0
    pending = list(enumerate(streams))
    while pending or live:
        while pending and pending[0][0] * skew <= step:
            live.append(pending.pop(0))
        still = []
        for idx, gen in live:
            try:
                next(gen)
                still.append((idx, gen))
            except StopIteration as done:
                results[idx] = done.value
        live = still
        step += 1
    return results


def _gla_stream(load, st, m, tt):
    each = lambda f, *xs: [f(*a) for a in zip(*xs)]
    blk = m["block16"]
    b_all = _dot_xr(_chunk_tri(tt), load(4, slice(0, tt)), 3)
    yield
    rows = [slice(c * CHUNK, (c + 1) * CHUNK) for c in range(tt // CHUNK)]
    q, k, v = ([load(j, r) for r in rows] for j in range(3))
    b = [b_all[r, :] for r in rows]
    b_last = each(lambda x: x[CHUNK - 1:CHUNK, :], b)
    scores = []
    for i0 in range(0, CHUNK, SUB):
        q_i = each(lambda qq, x: qq[i0:i0 + SUB, :] * jnp.exp(x[i0:i0 + SUB, :] - x[i0:i0 + 1, :]), q, b)
        k_i = each(lambda kk, x: kk * jnp.exp(jnp.minimum(x[i0:i0 + 1, :] - x, EXP_CAP)), k, b)
        yield
        scores.append(each(lambda a, c: _dot_nt(a, _blockdiag(c, blk)), q_i, k_i))
    kv = each(lambda vv, kk, x, xl: jnp.where(m["block"], _dot_tn(vv, kk * jnp.exp(xl - x)), 0.0), v, k, b, b_last)
    yield
    attn = each(lambda *r: jnp.where(m["causal"], jnp.concatenate(r, axis=0), 0.0), *scores)
    qe = each(lambda qq, x: qq * jnp.exp(x), q, b)
    yield
    o_loc = each(lambda a, vv: _dot(a, _blockdiag(vv, blk)), attn, v)
    yield
    outs = []
    for c in range(len(q)):
        outs.append(_dot_nt(qe[c], st) + o_loc[c])
        st = st * jnp.exp(b_last[c]) + kv[c]
        yield
    return outs, st


def _write_state(out_ref, i, st):
    r = _iota2((BR_WIDTH, BR_WIDTH), 0)
    c = _iota2((BR_WIDTH, BR_WIDTH), 1)
    eye = jnp.where(r == c, 1.0, 0.0).astype(BF16)
    nt_dot = lambda p: lax.dot_general(eye, p, (((1,), (1,)), ((), ())), preferred_element_type=F32)
    st_t = sum(nt_dot(p) for p in _split(st, 3))
    for h in range(N_HEADS):
        out_ref[i, h] = st_t[h * HEAD_DIM:(h + 1) * HEAD_DIM, h * HEAD_DIM:(h + 1) * HEAD_DIM]


def _gla_prep(pb, gkw_ref, gkb_ref):
    w = BR_WIDTH
    lf = _log_sigmoid(_dot(pb[:, 4 * w:4 * w + LANE], gkw_ref[...]) + gkb_ref[...]) * (1.0 / GLA_GATE_NORM)
    return pb[:, 0:w] * HEAD_DIM ** -0.5, pb[:, w:2 * w], pb[:, 2 * w:3 * w], _silu(pb[:, 3 * w:4 * w]), lf


def _hgrn_prep(pd, lb_ref):
    w = BR_WIDTH
    lb = lb_ref[...]
    xf = pd[:, w:2 * w]
    return (_silu(pd[:, 0:w]) * HEAD_DIM ** -0.5, (1.0 - lb) * _sigmoid(-xf), pd[:, 2 * w:3 * w],
            _silu(pd[:, 3 * w:4 * w]), _hgrn_log_forget(xf, lb))


def _linear_prompt_kernel(p_ref, o_ref, sout_ref, st_s, *, tt):
    @pl.when(pl.program_id(1) == 0)
    def _():
        st_s[...] = jnp.zeros(st_s.shape, F32)

    w = BR_WIDTH
    m = _head_masks()
    n_seq = p_ref.shape[0]
    load = lambda i: (lambda j, r: p_ref[i, r, j * w:(j + 1) * w])
    results = _run_skewed([_gla_stream(load(i), st_s[i], m, tt) for i in range(n_seq)], STREAM_SKEW)
    for i, (outs, st) in enumerate(results):
        st_s[i] = st
        o_ref[i] = jnp.concatenate(outs, axis=0)

    @pl.when(pl.program_id(1) == pl.num_programs(1) - 1)
    def _():
        for i in range(n_seq):
            _write_state(sout_ref, i, st_s[i])


def _hgrn_log_forget(xf, lb):
    ls_pos = _log_sigmoid(xf)
    pos = lb > 0
    lb_safe = jnp.where(pos, lb, 1.0)
    mixed = ls_pos + _softplus(jnp.log(lb_safe) - xf)
    return jnp.where(pos, mixed, ls_pos)


def _mixer_call(kernel, p, bsz, seq, tt, name):
    nt = seq // tt
    width = p.shape[1]
    n_seq = SEQ_PER_STEP if bsz % SEQ_PER_STEP == 0 else 1
    o, st = pl.pallas_call(
        functools.partial(kernel, tt=tt),
        grid=(bsz // n_seq, nt),
        in_specs=[pl.BlockSpec((n_seq, tt, width), lambda b, t: (b, t, 0))],
        out_specs=[pl.BlockSpec((n_seq, tt, BR_WIDTH), lambda b, t: (b, t, 0)),
                   pl.BlockSpec((n_seq, N_HEADS, HEAD_DIM, HEAD_DIM), lambda b, t: (b, 0, 0, 0))],
        out_shape=[jax.ShapeDtypeStruct((bsz, seq, BR_WIDTH), F32),
                   jax.ShapeDtypeStruct((bsz, N_HEADS, HEAD_DIM, HEAD_DIM), F32)],
        scratch_shapes=[pltpu.VMEM((n_seq, BR_WIDTH, BR_WIDTH), F32)],
        compiler_params=_cparams("parallel", "arbitrary"),
        name=name,
    )(p.reshape(bsz, seq, width))
    return o.reshape(bsz * seq, BR_WIDTH), st


def _mm_split(a, b, tile_mask):
    a_hi, a_lo = _split(a, 2)
    b_hi, b_lo = (jnp.concatenate([p] * (BR_WIDTH // b.shape[0]), axis=0) * tile_mask for p in _split(b, 2))
    dot = functools.partial(jnp.dot, preferred_element_type=F32)
    return dot(a_hi, b_hi) + (dot(a_lo, b_hi) + dot(a_hi, b_lo))


def _unit_lower_inverse(n, m):
    each = lambda f, *xs: [f(*a) for a in zip(*xs)]
    eye_c, sub, blk = m["eye_sub"], m["sub16"], m["block16"]
    nd = each(lambda a: jnp.where(m["same_sub"], a, 0.0), n)
    low = each(lambda a, d: a - d, n, nd)
    c = each(lambda d: d[0:SUB] + d[SUB:2 * SUB] + d[2 * SUB:3 * SUB] + d[3 * SUB:4 * SUB], nd)
    p = each(lambda a: _mm_split(a, a, sub), c)
    x = each(lambda a: eye_c - a, c)
    yield
    for _ in range(2):
        xp = each(lambda a, b: _mm_split(jnp.concatenate([a, b], axis=0), b, sub), x, p)
        x = each(lambda a, b: a + b[0:SUB], x, xp)
        p = each(lambda b: b[SUB:2 * SUB], xp)
        yield
    x = each(lambda a, b: a + _mm_split(a, b, sub), x, p)
    dinv = each(lambda a: jnp.where(m["same_sub"], jnp.concatenate([a] * (CHUNK // SUB), axis=0), 0.0), x)
    yield
    mm = each(lambda a, b: _mm_split(a, b, blk), dinv, low)
    yield
    m2 = each(lambda a: _mm_split(a, a, blk), mm)
    yield
    e = each(lambda d, a: d + _mm_split(a, d, blk), dinv, m2)
    yield
    return each(lambda a, b: b - _mm_split(a, b, blk), mm, e)


def _gdn_stream(load, st, m, tt):
    each = lambda f, *xs: [f(*a) for a in zip(*xs)]
    w = BR_WIDTH
    blk = m["block16"]
    gam_all = _dot_xr(_chunk_tri(tt), load(4, slice(0, tt)), 3)
    yield
    lane_j = _iota2((tt, w), 1) & (CHUNK - 1)
    row_j = _iota2((tt, w), 0) & (CHUNK - 1)
    gam_t = _dot_xr(_chunk_ones(tt), jnp.where(lane_j == row_j, gam_all, 0.0), 3)
    yield
    rows = [slice(c * CHUNK, (c + 1) * CHUNK) for c in range(tt // CHUNK)]
    q, k, v, bexp = ([load(j, r) for r in rows] for j in (0, 1, 2, 5))
    gam_i = [gam_all[r, :] for r in rows]
    gam_j = [gam_t[r, :] for r in rows]
    decay = each(lambda gi, gj: jnp.where(m["causal"], jnp.exp(jnp.minimum(gi - gj, 0.0)), 0.0), gam_i, gam_j)
    scores = each(lambda kk, qq: _dot_nt(jnp.concatenate([kk, qq], axis=0), _blockdiag(kk, blk)), k, q)
    yield
    attn = each(lambda s, d: s[CHUNK:2 * CHUNK] * d, scores, decay)
    n = each(lambda b, s, d: jnp.where(m["strict"], b * s[0:CHUNK] * d, 0.0), bexp, scores, decay)
    tinv = yield from _unit_lower_inverse(n, m)
    yield
    u_v = each(lambda t, b, vv: _dot(t, _blockdiag(b * vv, blk)), tinv, bexp, v)
    w_k = each(lambda t, b, gi, kk: _dot(t, _blockdiag(b * jnp.exp(gi) * kk, blk)), tinv, bexp, gam_i, k)
    yield
    outs = []
    for c in range(len(rows)):
        gam_last = gam_i[c][CHUNK - 1:CHUNK, :]
        ws = _dot_nt(jnp.concatenate([w_k[c], q[c]], axis=0), st)
        u = u_v[c] - ws[0:CHUNK]
        yield
        outs.append(jnp.exp(gam_i[c]) * ws[CHUNK:2 * CHUNK] + _dot(attn[c], _blockdiag(u, blk)))
        kd = k[c] * jnp.exp(gam_last - gam_i[c])
        st = st * jnp.exp(gam_last) + jnp.where(m["block"], _dot_tn(u, kd), 0.0)
        yield
    return outs, st


def _l2norm_heads(x, ones_blk):
    return x * lax.rsqrt(_dot_xl(x * x, ones_blk, 2) + EPS)


def _gdn_prep(a_alpha_beta, alog_ref, dtb_ref):
    r = _iota2((LANE, BR_WIDTH), 0)
    c = _iota2((LANE, BR_WIDTH), 1) >> HEAD_SHIFT
    sel_a = jnp.where(r == c, 1.0, 0.0).astype(BF16)
    sel_b = jnp.where(r == c + N_HEADS, 1.0, 0.0).astype(BF16)
    alpha = _dot_xl(a_alpha_beta, sel_a, 3)
    beta = _sigmoid(_dot_xl(a_alpha_beta, sel_b, 3))
    g = -jnp.exp(alog_ref[...]) * _softplus(alpha + dtb_ref[...])
    return g, beta


def _gdn_conv_qkv(raw, hist, cw_ref):
    w = BR_WIDTH
    rows = raw.shape[0]
    xp = jnp.concatenate([hist, raw], axis=0)
    y = raw * cw_ref[CONV_W - 1:CONV_W, :]
    for j in range(CONV_W - 1):
        start = SUBLANE - (CONV_W - 1) + j
        y = y + xp[start:start + rows, :] * cw_ref[j:j + 1, :]
    qkv = _silu(y)
    ones_blk = _block_ones()
    return (_l2norm_heads(qkv[:, 0:w], ones_blk) * HEAD_DIM ** -0.5, _l2norm_heads(qkv[:, w:2 * w], ones_blk),
            qkv[:, 2 * w:3 * w])


def _gdn_prompt_kernel(p_ref, o_ref, sout_ref, st_s, *, tt):
    t = pl.program_id(1)

    @pl.when(t == 0)
    def _():
        st_s[...] = jnp.zeros(st_s.shape, F32)

    w = BR_WIDTH
    m = _head_masks()
    n_seq = p_ref.shape[0]
    load = lambda i: (lambda j, r: p_ref[i, r, j * w:(j + 1) * w])
    results = _run_skewed([_gdn_stream(load(i), st_s[i], m, tt) for i in range(n_seq)], STREAM_SKEW)
    for i, (outs, st) in enumerate(results):
        st_s[i] = st
        o_ref[i] = jnp.concatenate(outs, axis=0)

    @pl.when(t == pl.num_programs(1) - 1)
    def _():
        for i in range(n_seq):
            _write_state(sout_ref, i, st_s[i])


def _s5_kernel(u_ref, x0r_ref, x0i_ref, ar_ref, ai_ref, br_ref, bi_ref, cr_ref, ci_ref, d_ref, glu_ref,
               o_ref, xr_ref, xi_ref, sr_s, si_s, *, steps, rows):
    @pl.when(pl.program_id(0) == 0)
    def _():
        xr_ref[...] = x0r_ref[...]
        xi_ref[...] = x0i_ref[...]

    u = u_ref[...]
    ub = u.astype(BF16)
    sr_s[...] = _dot(ub, br_ref[...])
    si_s[...] = _dot(ub, bi_ref[...])
    a_re = jnp.broadcast_to(ar_ref[...], (rows, S5_STATE))
    a_im = jnp.broadcast_to(ai_ref[...], (rows, S5_STATE))

    def body(t, carry):
        xr, xi = carry
        r = pl.ds(pl.multiple_of(t * rows, rows), rows)
        nr = (sr_s[r, :] + a_re * xr) - a_im * xi
        ni = (si_s[r, :] + a_re * xi) + a_im * xr
        sr_s[r, :] = nr
        si_s[r, :] = ni
        return nr, ni

    xr, xi = lax.fori_loop(0, steps, body, (xr_ref[...], xi_ref[...]), unroll=min(steps, S5_UNROLL))
    xr_ref[...] = xr
    xi_ref[...] = xi
    y = _dot(sr_s[...].astype(BF16), cr_ref[...]) - _dot(si_s[...].astype(BF16), ci_ref[...]) + d_ref[...] * u
    hg = _dot(_gelu(y).astype(BF16), glu_ref[...])
    o_ref[...] = hg[:, 0:BR_WIDTH] * _sigmoid(hg[:, BR_WIDTH:2 * BR_WIDTH])


def _s5(u, x0r, x0i, params, steps_total, rows, steps):
    nt = steps_total // steps
    tile = steps * rows
    return pl.pallas_call(
        functools.partial(_s5_kernel, steps=steps, rows=rows),
        grid=(nt,),
        in_specs=[pl.BlockSpec((tile, BR_WIDTH), lambda t: (t, 0)), _full(x0r.shape), _full(x0i.shape)]
                 + [_full(a.shape) for a in params],
        out_specs=[pl.BlockSpec((tile, BR_WIDTH), lambda t: (t, 0)), _full(x0r.shape), _full(x0i.shape)],
        out_shape=[jax.ShapeDtypeStruct((steps_total * rows, BR_WIDTH), F32),
                   jax.ShapeDtypeStruct(x0r.shape, F32), jax.ShapeDtypeStruct(x0i.shape, F32)],
        scratch_shapes=[pltpu.VMEM((tile, S5_STATE), F32)] * 2,
        compiler_params=_cparams("arbitrary"),
        name="s5",
    )(u, x0r, x0i, *params)


def _s5_params(lam_re, lam_im, b_re, b_im, c_re, c_im, d, log_dt, glu_w):
    lr = jnp.minimum(lam_re, -1e-4)
    li = lam_im
    dt = jnp.exp(log_dt)[:, None]
    mag = jnp.exp(lr * dt)
    ab_re, ab_im = mag * jnp.cos(li * dt), mag * jnp.sin(li * dt)
    den = lr * lr + li * li
    z_re = ((ab_re - 1.0) * lr + ab_im * li) / den
    z_im = (ab_im * lr - (ab_re - 1.0) * li) / den
    bb_re = z_re[..., None] * b_re - z_im[..., None] * b_im
    bb_im = z_re[..., None] * b_im + z_im[..., None] * b_re
    eye = jnp.eye(S5_GROUPS, dtype=F32)

    def pack_in(bb):
        return jnp.einsum('gph,gk->ghkp', bb, eye).reshape(BR_WIDTH, S5_STATE).astype(BF16)

    def pack_out(c):
        return jnp.einsum('ghp,gk->gpkh', c, eye).reshape(S5_STATE, BR_WIDTH).astype(BF16)

    return (ab_re.reshape(1, S5_STATE), ab_im.reshape(1, S5_STATE), pack_in(bb_re), pack_in(bb_im),
            pack_out(c_re), pack_out(c_im), d.reshape(1, BR_WIDTH), glu_w.astype(BF16))


def _expand_mats():
    r = _iota2((HEAD_DIM, HEAD_DIM * HEAD_DIM), 0)
    c = _iota2((HEAD_DIM, HEAD_DIM * HEAD_DIM), 1)
    rep_k = jnp.where(r == (c >> HEAD_SHIFT), 1.0, 0.0).astype(BF16)
    rep_v = jnp.where(r == (c & (HEAD_DIM - 1)), 1.0, 0.0).astype(BF16)
    return rep_k, rep_v


def _gla_decode_kernel(q_ref, k_ref, v_ref, lf_ref, s_ref, o_ref, sn_ref):
    rep_k, rep_v = _expand_mats()
    q, k, v, lf = q_ref[0, 0], k_ref[0, 0], v_ref[0, 0], lf_ref[0, 0]
    s = s_ref[...]
    dec = jnp.exp(lf)
    qe_x = _dot_xl(q * dec, rep_k, 1)
    o = _dot_nt_xl(qe_x * s, rep_v, 2) + jnp.sum(q * k, axis=-1, keepdims=True) * v
    sn_ref[...] = _dot_xl(dec, rep_k, 3) * s + _dot_xl(k, rep_k, 1) * _dot_xl(v, rep_v, 1)
    o_ref[0] = o


def _gdn_decode_kernel(q_ref, k_ref, v_ref, g_ref, b_ref, s_ref, o_ref, sn_ref):
    rep_k, rep_v = _expand_mats()
    q, k, v = q_ref[0, 0], k_ref[0, 0], v_ref[0, 0]
    eg = jnp.exp(g_ref[0, 0])
    beta = b_ref[0, 0]
    s = s_ref[...]
    k_x = _dot_xl(k, rep_k, 1)
    ks = _dot_nt_xl(k_x * s, rep_v, 2)
    qs = _dot_nt_xl(_dot_xl(q, rep_k, 1) * s, rep_v, 2)
    u = beta * v - (beta * eg) * ks
    o_ref[0] = eg * qs + jnp.sum(q * k, axis=-1, keepdims=True) * u
    sn_ref[...] = eg * s + k_x * _dot_xl(u, rep_v, 1)


def _decode_call(kernel, packs, state, name):
    rows = state.shape[0]
    hw = HEAD_DIM * HEAD_DIM
    vec_specs, vecs = [], []
    for a in packs:
        for j in range(a.shape[0]):
            vec_specs.append(pl.BlockSpec((1, 1, rows, a.shape[3]), lambda h, j=j: (j, h, 0, 0)))
            vecs.append(a)
    return pl.pallas_call(
        kernel,
        grid=(N_HEADS,),
        in_specs=vec_specs + [pl.BlockSpec((rows, hw), lambda h: (0, h))],
        out_specs=[pl.BlockSpec((1, rows, HEAD_DIM), lambda h: (h, 0, 0)), pl.BlockSpec((rows, hw), lambda h: (0, h))],
        out_shape=[jax.ShapeDtypeStruct((N_HEADS, rows, HEAD_DIM), F32), jax.ShapeDtypeStruct(state.shape, F32)],
        compiler_params=_cparams("parallel"),
        name=name,
    )(*vecs, state)


def _put_heads(ref, j, x):
    for h in range(N_HEADS):
        ref[j, h] = x[:, h * HEAD_DIM:(h + 1) * HEAD_DIM]


def _decode_prep_kernel(pa_ref, pb_ref, pd_ref, conv_ref, cw_ref, alog_ref, dtb_ref, gkw_ref, gkb_ref, lb_ref,
                        a_ref, ag_ref, b_ref, d_ref, convn_ref):
    w = BR_WIDTH
    cw = 3 * w
    ones_blk = _block_ones()
    raw = pa_ref[:, 0:cw]
    y = raw * cw_ref[CONV_W - 1:CONV_W, :]
    for j in range(CONV_W - 1):
        y = y + conv_ref[:, j * cw:(j + 1) * cw] * cw_ref[j:j + 1, :]
    convn_ref[:, 0:(CONV_W - 2) * cw] = conv_ref[:, cw:(CONV_W - 1) * cw]
    convn_ref[:, (CONV_W - 2) * cw:(CONV_W - 1) * cw] = raw
    qkv = _silu(y)
    _put_heads(a_ref, 0, _l2norm_heads(qkv[:, 0:w], ones_blk) * HEAD_DIM ** -0.5)
    _put_heads(a_ref, 1, _l2norm_heads(qkv[:, w:2 * w], ones_blk))
    _put_heads(a_ref, 2, qkv[:, 2 * w:3 * w])
    g, beta = _gdn_prep(pa_ref[:, 4 * w:4 * w + LANE], alog_ref, dtb_ref)
    for h in range(N_HEADS):
        ag_ref[0, h] = g[:, h * HEAD_DIM:h * HEAD_DIM + 1]
        ag_ref[1, h] = beta[:, h * HEAD_DIM:h * HEAD_DIM + 1]
    for ref, (q, k, v, _, lf) in ((b_ref, _gla_prep(pb_ref[...], gkw_ref, gkb_ref)),
                                  (d_ref, _hgrn_prep(pd_ref[...], lb_ref))):
        for j, val in enumerate((q, k, v, lf)):
            _put_heads(ref, j, val)


def _decode_prep(pa, pb, pd, conv, params):
    rows = pa.shape[0]
    ins = (pa, pb, pd, conv) + tuple(params)
    per_head = lambda n, width: (n, N_HEADS, rows, width)
    shapes = [per_head(3, HEAD_DIM), per_head(2, 1), per_head(4, HEAD_DIM), per_head(4, HEAD_DIM), conv.shape]
    return pl.pallas_call(
        _decode_prep_kernel,
        grid=(1,),
        in_specs=[_full(a.shape) for a in ins],
        out_specs=[_full(s) for s in shapes],
        out_shape=[jax.ShapeDtypeStruct(s, F32) for s in shapes],
        compiler_params=_cparams("arbitrary"),
        name="decode_prep",
    )(*ins)


def _decode_post_kernel(oa_ref, ob_ref, od_ref, pa_ref, pb_ref, pd_ref, nwa_ref, nwb_ref, nwd_ref,
                        a_ref, b_ref, d_ref):
    w = BR_WIDTH
    ones_blk = _block_ones()
    heads = lambda ref: jnp.concatenate([ref[h] for h in range(N_HEADS)], axis=1)
    for out, o, p, nw in ((a_ref, oa_ref, pa_ref, nwa_ref), (b_ref, ob_ref, pb_ref, nwb_ref),
                          (d_ref, od_ref, pd_ref, nwd_ref)):
        out[...] = _head_norm_gate(heads(o), _silu(p[:, 3 * w:4 * w]), nw[...], ones_blk)


def _decode_post(oa, ob, od, pa, pb, pd, nwa, nwb, nwd):
    ins = (oa, ob, od, pa, pb, pd, nwa, nwb, nwd)
    shp = (oa.shape[1], BR_WIDTH)
    return pl.pallas_call(
        _decode_post_kernel,
        grid=(1,),
        in_specs=[_full(a.shape) for a in ins],
        out_specs=[_full(shp)] * 3,
        out_shape=[jax.ShapeDtypeStruct(shp, F32)] * 3,
        compiler_params=_cparams("arbitrary"),
        name="decode_post",
    )(*ins)


def _pack_w_in(w_in):
    sizes = (3 * BR_WIDTH, N_HEADS, N_HEADS, BR_WIDTH, BR_WIDTH, BR_WIDTH, BR_WIDTH, GLA_GATE_RANK, BR_WIDTH,
             BR_WIDTH, BR_WIDTH, BR_WIDTH, BR_WIDTH, BR_WIDTH, N_BRANCH * D_MODEL)
    w_in = w_in.astype(BF16)
    parts, off = [], 0
    for n in sizes:
        parts.append(w_in[..., off:off + n])
        off += n
    (a_qkv, a_alpha, a_beta, a_gate, b_q, b_k, b_v, b_gk, b_gate, c_u, d_q, d_f, d_i, d_gate, merge) = parts
    zpad = lambda n: jnp.zeros(w_in.shape[:-1] + (n,), BF16)
    mix = jnp.concatenate([a_qkv, a_gate, a_alpha, a_beta, zpad(LANE - 2 * N_HEADS),
                           b_q, b_k, b_v, b_gate, b_gk, zpad(LANE - GLA_GATE_RANK),
                           c_u, d_q, d_f, d_i, d_gate], axis=-1)
    return mix, merge


def _tile_sizes(seq):
    pick = lambda n: n if seq % n == 0 else seq
    return pick(512), pick(256), pick(128)


def _hgrn_lower_bounds(logits):
    p = jax.nn.softmax(logits, axis=0)
    return jnp.cumsum(p, axis=0) - p[0]


def _row(x):
    return x.reshape(1, -1)


def _rep_heads(x):
    return jnp.repeat(x, HEAD_DIM).reshape(1, BR_WIDTH)


def kernel(x_prompt, x_sample, state_gdn_conv, state_gdn, state_gla, state_s5_re, state_s5_im, state_hgrn, norm1_w, w_in, gdn_conv_w, gdn_a_log, gdn_dt_bias, gdn_norm_w, gla_gk_w, gla_gk_b, gla_norm_w, s5_lambda_re, s5_lambda_im, s5_b_re, s5_b_im, s5_c_re, s5_c_im, s5_d, s5_log_dt, s5_glu_w, hgrn_lb_logits, hgrn_norm_w, w_branch, w_out, norm2_w, ffn_w_gate, ffn_w_up, ffn_w_down, final_norm_w):
    bsz, seq, _ = x_prompt.shape
    dbs = x_sample.shape[0]
    tm_p, tt, s5_steps = _tile_sizes(seq)
    lb_all = _hgrn_lower_bounds(hgrn_lb_logits)
    w_mix_all, w_merge_all = _pack_w_in(w_in)
    gkw_all = jnp.concatenate([gla_gk_w, jnp.zeros((DEPTH, LANE - GLA_GATE_RANK, BR_WIDTH), F32)], axis=1)
    wb_all, wo_all = w_branch.astype(BF16), w_out.astype(BF16)
    wg_all, wu_all, wd_all = ffn_w_gate.astype(BF16), ffn_w_up.astype(BF16), ffn_w_down.astype(BF16)
    hp = x_prompt.reshape(bsz * seq, D_MODEL)
    hs = x_sample.reshape(dbs, D_MODEL)
    zeros_p = jnp.zeros((bsz, S5_STATE), F32)
    fw = _row(final_norm_w)
    outs_p = [[] for _ in range(6)]
    outs_s = [[] for _ in range(6)]
    for i in range(DEPTH):
        final = i == DEPTH - 1
        w_mix, w_merge, gkw = w_mix_all[i], w_merge_all[i], gkw_all[i]
        nw1, nw2 = _row(norm1_w[i]), _row(norm2_w[i])
        gdn_params = (gdn_conv_w[i], _rep_heads(gdn_a_log[i]), _rep_heads(gdn_dt_bias[i]), _row(gdn_norm_w[i]))
        gla_params = (gkw, _row(gla_gk_b[i]), _row(gla_norm_w[i]))
        hgrn_params = (_row(lb_all[i]), _row(hgrn_norm_w[i]))
        s5_params = _s5_params(s5_lambda_re[i], s5_lambda_im[i], s5_b_re[i], s5_b_im[i], s5_c_re[i], s5_c_im[i],
                               s5_d[i], s5_log_dt[i], s5_glu_w[i])
        prep_params = (gdn_conv_w[i], gdn_params[1], gdn_params[2], gkw, gla_params[1], hgrn_params[0])
        wb, wo, wg, wu, wd = wb_all[i], wo_all[i], wg_all[i], wu_all[i], wd_all[i]

        ga, gb, pc, gd, conv_p = _in_proj_prompt(hp, nw1, w_mix, prep_params, tm_p, seq)
        o_a, st_a = _mixer_call(_gdn_prompt_kernel, ga, bsz, seq, tt, "gdn_prompt")
        o_b, st_b = _mixer_call(_linear_prompt_kernel, gb, bsz, seq, tt, "gla_prompt")
        o_d, st_d = _mixer_call(_linear_prompt_kernel, gd, bsz, seq, tt, "hgrn_prompt")
        o_c, xr_p, xi_p = _s5(pc.reshape(seq * bsz, BR_WIDTH), zeros_p, zeros_p, s5_params, seq, bsz, s5_steps)
        hp = _merge(hp, o_a, o_b, o_c.reshape(seq, bsz * BR_WIDTH), o_d, nw1, w_merge, wb, wo, tm_p, seq,
                    prepared=(ga, gb, gd), gains=(gdn_params[3], gla_params[2], hgrn_params[1]))
        hp = _ffn(hp, nw2, wg, wu, wd, fw, tm_p, final)
        for lst, s in zip(outs_p, (conv_p, st_a, st_b, xr_p.reshape(bsz, S5_GROUPS, S5_P),
                                   xi_p.reshape(bsz, S5_GROUPS, S5_P), st_d)):
            lst.append(s)

        pa, pb, pc, pd = _in_proj(hs, nw1, w_mix, dbs, dbs)
        va, vg, vb, vd, conv_s = _decode_prep(
            pa, pb, pd, state_gdn_conv[i].reshape(dbs, (CONV_W - 1) * 3 * BR_WIDTH), prep_params)
        flat = lambda s: s.reshape(dbs, N_HEADS * HEAD_DIM * HEAD_DIM)
        oa_h, sa = _decode_call(_gdn_decode_kernel, [va, vg], flat(state_gdn[i]), "gdn_decode")
        ob_h, sb = _decode_call(_gla_decode_kernel, [vb], flat(state_gla[i]), "gla_decode")
        od_h, sd = _decode_call(_gla_decode_kernel, [vd], flat(state_hgrn[i]), "hgrn_decode")
        o_a, o_b, o_d = _decode_post(oa_h, ob_h, od_h, pa, pb, pd, gdn_params[3], gla_params[2], hgrn_params[1])
        o_c, xr_s, xi_s = _s5(pc, state_s5_re[i].reshape(dbs, S5_STATE), state_s5_im[i].reshape(dbs, S5_STATE),
                              s5_params, 1, dbs, 1)
        hs = _merge(hs, o_a, o_b, o_c, o_d, nw1, w_merge, wb, wo, dbs, dbs)
        hs = _ffn(hs, nw2, wg, wu, wd, fw, dbs, final)
        st5 = lambda s: s.reshape(dbs, N_HEADS, HEAD_DIM, HEAD_DIM)
        for lst, s in zip(outs_s, (conv_s.reshape(dbs, CONV_W - 1, 3 * BR_WIDTH), st5(sa), st5(sb),
                                   xr_s.reshape(dbs, S5_GROUPS, S5_P), xi_s.reshape(dbs, S5_GROUPS, S5_P), st5(sd))):
            lst.append(s)

    return (hp.reshape(bsz, seq, D_MODEL), hs.reshape(dbs, 1, D_MODEL),
            *[jnp.stack(l) for l in outs_p], *[jnp.stack(l) for l in outs_s])
```

```python
import functools
import math

import jax
import jax.numpy as jnp
from jax import lax
from jax.experimental import pallas as pl
from jax.experimental.pallas import tpu as pltpu

F32 = jnp.float32
BF16 = jnp.bfloat16

D_MODEL = 1024
DEPTH = 4
N_BRANCH = 4
BR_WIDTH = D_MODEL // N_BRANCH
HEAD_DIM = 64
N_HEADS = BR_WIDTH // HEAD_DIM
CONV_W = 4
CHUNK = 64
SUB = 16
GLA_GATE_RANK = 16
GLA_GATE_NORM = 16.0
S5_GROUP = 16
S5_GROUPS = BR_WIDTH // S5_GROUP
S5_P = 64
S5_STATE = S5_GROUPS * S5_P
D_FF = -(-8 * D_MODEL // (3 * 256)) * 256
FF_TILE = 256
S5_UNROLL = 8
EPS = 1e-6
SEQ_PER_STEP = 8
STREAM_SKEW = 1
EXP_CAP = 60.0
LANE = 128
SUBLANE = 8
HEAD_SHIFT = HEAD_DIM.bit_length() - 1
CHUNK_SHIFT = CHUNK.bit_length() - 1
SUB_SHIFT = SUB.bit_length() - 1
SEG_A = 3 * BR_WIDTH + BR_WIDTH + LANE
SEG_B = 4 * BR_WIDTH + LANE
SEG_C = BR_WIDTH
SEG_D = 4 * BR_WIDTH
SEG_OFF = (0, SEG_A, SEG_A + SEG_B, SEG_A + SEG_B + SEG_C, SEG_A + SEG_B + SEG_C + SEG_D)
N_MIX = SEG_OFF[-1]
PREP_A = 6 * BR_WIDTH
PREP_B = 5 * BR_WIDTH
VMEM_LIMIT = 56 * 1024 * 1024


def _cparams(*sem):
    return pltpu.CompilerParams(dimension_semantics=sem, vmem_limit_bytes=VMEM_LIMIT)


def _full(shape):
    n = len(shape)
    return pl.BlockSpec(shape, lambda *_: (0,) * n)


def _rms(x, w):
    return x * lax.rsqrt(jnp.mean(x * x, axis=-1, keepdims=True) + EPS) * w


def _sigmoid(x):
    return 0.5 * (1.0 + jnp.tanh(0.5 * x))


def _silu(x):
    return x * _sigmoid(x)


def _softplus(x):
    return jnp.maximum(x, 0.0) + jnp.log(1.0 + jnp.exp(-jnp.abs(x)))


def _log_sigmoid(x):
    return -_softplus(-x)


def _gelu(x):
    return 0.5 * x * (1.0 + jnp.tanh(math.sqrt(2.0 / math.pi) * (x + 0.044715 * x * x * x)))


def _dot(a, b):
    return jnp.dot(a.astype(BF16), b.astype(BF16), preferred_element_type=F32)


def _dot_nt(a, b):
    return lax.dot_general(a.astype(BF16), b.astype(BF16), (((1,), (1,)), ((), ())), preferred_element_type=F32)


def _dot_tn(a, b):
    return lax.dot_general(a.astype(BF16), b.astype(BF16), (((0,), (0,)), ((), ())), preferred_element_type=F32)


def _split(x, terms):
    parts, rest = [], x
    for t in range(terms):
        p = rest.astype(BF16)
        parts.append(p)
        if t + 1 < terms:
            rest = rest - p.astype(F32)
    return parts


def _dot_xl(x, c, terms=3):
    return sum(jnp.dot(p, c, preferred_element_type=F32) for p in _split(x, terms))


def _dot_xr(c, x, terms=3):
    return sum(jnp.dot(c, p, preferred_element_type=F32) for p in _split(x, terms))


def _dot_nt_xl(x, c, terms=3):
    return sum(lax.dot_general(p, c, (((1,), (1,)), ((), ())), preferred_element_type=F32) for p in _split(x, terms))


def _iota2(shape, dim):
    return lax.broadcasted_iota(jnp.int32, shape, dim)


def _head_masks():
    r = _iota2((CHUNK, N_HEADS * CHUNK), 0)
    c = _iota2((CHUNK, N_HEADS * CHUNK), 1) & (CHUNK - 1)
    rr = _iota2((N_HEADS * CHUNK, N_HEADS * HEAD_DIM), 0)
    cc = _iota2((N_HEADS * CHUNK, N_HEADS * HEAD_DIM), 1)
    block = (rr >> CHUNK_SHIFT) == (cc >> HEAD_SHIFT)
    one = lambda mask: jnp.where(mask, 1.0, 0.0)
    return dict(causal=r >= c, strict=r > c, eye=r == c, same_sub=(r >> SUB_SHIFT) == (c >> SUB_SHIFT), block=block,
                block16=one(block).astype(BF16), sub16=one((rr >> SUB_SHIFT) == (cc >> SUB_SHIFT)).astype(BF16),
                eye_sub=one(_iota2((SUB, BR_WIDTH), 0) == (_iota2((SUB, BR_WIDTH), 1) & (SUB - 1))))


def _chunk_tri(tt):
    r = _iota2((tt, tt), 0)
    c = _iota2((tt, tt), 1)
    return jnp.where((r >= c) & ((r >> CHUNK_SHIFT) == (c >> CHUNK_SHIFT)), 1.0, 0.0).astype(BF16)


def _chunk_ones(tt):
    r = _iota2((tt, tt), 0) >> CHUNK_SHIFT
    c = _iota2((tt, tt), 1) >> CHUNK_SHIFT
    return jnp.where(r == c, 1.0, 0.0).astype(BF16)


def _block_ones():
    return _chunk_ones(BR_WIDTH)


def _blockdiag(x, block16):
    return jnp.concatenate([x.astype(BF16)] * N_HEADS, axis=0) * block16


def _head_norm_gate(o, silu_gate, w, ones_blk):
    ms = _dot_xl(o * o, ones_blk, 2) * (1.0 / HEAD_DIM)
    return o * lax.rsqrt(ms + EPS) * w * silu_gate


def _in_proj_kernel(x_ref, nw_ref, w_ref, pa_ref, pb_ref, pc_ref, pd_ref):
    xn = _rms(x_ref[...], nw_ref[...]).astype(BF16)
    for i, ref in enumerate((pa_ref, pb_ref, pc_ref, pd_ref)):
        ref[...] = _dot(xn, w_ref[:, SEG_OFF[i]:SEG_OFF[i + 1]])


def _time_major_spec(tm, seq):
    nt = seq // tm
    return pl.BlockSpec((tm, BR_WIDTH), lambda i: (i % nt, i // nt))


def _in_proj_prompt_kernel(x_ref, halo_ref, nw_ref, w_ref, cw_ref, alog_ref, dtb_ref, gkw_ref, gkb_ref, lb_ref,
                           ga_ref, gb_ref, pc_ref, gd_ref, conv_ref, *, nt):
    w = BR_WIDTH
    seg = lambda i: w_ref[:, SEG_OFF[i]:SEG_OFF[i + 1]]
    xn = _rms(x_ref[...], nw_ref[...]).astype(BF16)
    pd = _dot(xn, seg(3))
    pa = _dot(xn, seg(0))
    hist = _dot(_rms(halo_ref[...], nw_ref[...]), w_ref[:, 0:3 * w])
    for j, val in enumerate(_hgrn_prep(pd, lb_ref)):
        gd_ref[:, j * w:(j + 1) * w] = val
    pb = _dot(xn, seg(1))
    raw = pa[:, 0:3 * w]
    hist = jnp.where((pl.program_id(0) % nt) == 0, 0.0, hist)
    rows = raw.shape[0]
    conv_ref[0] = raw[rows - (CONV_W - 1):rows, :]
    q, k, v = _gdn_conv_qkv(raw, hist, cw_ref)
    g, beta = _gdn_prep(pa[:, 4 * w:4 * w + LANE], alog_ref, dtb_ref)
    for j, val in enumerate((q, k, v, _silu(pa[:, 3 * w:4 * w]), g, beta)):
        ga_ref[:, j * w:(j + 1) * w] = val
    pc_ref[...] = _dot(xn, seg(2))
    for j, val in enumerate(_gla_prep(pb, gkw_ref, gkb_ref)):
        gb_ref[:, j * w:(j + 1) * w] = val


def _in_proj_prompt(x, nw, w, params, tm, seq):
    t = x.shape[0]
    nt = seq // tm
    bsz = t // seq
    row = lambda n: pl.BlockSpec((tm, n), lambda i: (i, 0))
    halo = pl.BlockSpec((SUBLANE, D_MODEL), lambda i: (jnp.maximum(i * (tm // SUBLANE) - 1, 0), 0))
    return pl.pallas_call(
        functools.partial(_in_proj_prompt_kernel, nt=nt),
        grid=(t // tm,),
        in_specs=[row(D_MODEL), halo, _full((1, D_MODEL)), _full((D_MODEL, N_MIX))] + [_full(a.shape) for a in params],
        out_specs=[row(PREP_A), row(PREP_B), _time_major_spec(tm, seq), row(PREP_B),
                   pl.BlockSpec((1, CONV_W - 1, 3 * BR_WIDTH), lambda i: (i // nt, 0, 0))],
        out_shape=[jax.ShapeDtypeStruct((t, PREP_A), F32), jax.ShapeDtypeStruct((t, PREP_B), F32),
                   jax.ShapeDtypeStruct((seq, bsz * BR_WIDTH), F32), jax.ShapeDtypeStruct((t, PREP_B), F32),
                   jax.ShapeDtypeStruct((bsz, CONV_W - 1, 3 * BR_WIDTH), F32)],
        compiler_params=_cparams("arbitrary"),
        name="in_proj_prompt",
    )(x, x, nw, w, *params)


def _in_proj(x, nw, w, tm, seq):
    t = x.shape[0]
    widths = (SEG_A, SEG_B, SEG_C, SEG_D)
    out_specs = [pl.BlockSpec((tm, n), lambda i: (i, 0)) for n in widths]
    out_shape = [jax.ShapeDtypeStruct((t, n), F32) for n in widths]
    out_specs[2] = _time_major_spec(tm, seq)
    out_shape[2] = jax.ShapeDtypeStruct((seq, (t // seq) * BR_WIDTH), F32)
    return pl.pallas_call(
        _in_proj_kernel,
        grid=(t // tm,),
        in_specs=[pl.BlockSpec((tm, D_MODEL), lambda i: (i, 0)), _full((1, D_MODEL)), _full((D_MODEL, N_MIX))],
        out_specs=out_specs,
        out_shape=out_shape,
        compiler_params=_cparams("parallel"),
        name="in_proj",
    )(x, nw, w)


def _merge_kernel(h_ref, oa_ref, ob_ref, oc_ref, od_ref, *rest, head_norm):
    if head_norm:
        gates, gains, rest = rest[0:3], rest[3:6], rest[6:]
    nw_ref, wm_ref, wb_ref, wo_ref, out_ref = rest
    h = h_ref[...]
    xn = _rms(h, nw_ref[...]).astype(BF16)
    o_vals = [oa_ref[...], ob_ref[...], oc_ref[...], od_ref[...]]
    if head_norm:
        ones_blk = _block_ones()
        for k, g_ref, w_ref in zip((0, 1, 3), gates, gains):
            o_vals[k] = _head_norm_gate(o_vals[k], g_ref[...], w_ref[...], ones_blk)
    proj = lambda k: (_dot(xn, wm_ref[:, k * D_MODEL:(k + 1) * D_MODEL]), _dot(o_vals[k], wb_ref[k]))
    acc = jnp.zeros(h.shape, F32)
    nxt = proj(0)
    for k in range(N_BRANCH):
        gate, br = nxt
        if k + 1 < N_BRANCH:
            nxt = proj(k + 1)
        acc = acc + _sigmoid(gate) * br
    out_ref[...] = h + _dot(acc, wo_ref[...])


def _merge(h, oa, ob, oc, od, nw, wm, wb, wo, tm, seq, prepared=None, gains=None):
    t = h.shape[0]
    row = lambda n: pl.BlockSpec((tm, n), lambda i: (i, 0))
    head_norm = prepared is not None
    extra, extra_specs = (), []
    if head_norm:
        extra = tuple(prepared) + tuple(gains)
        extra_specs = [pl.BlockSpec((tm, BR_WIDTH), lambda i: (i, 3))] * 3 + [_full((1, BR_WIDTH))] * 3
    return pl.pallas_call(
        functools.partial(_merge_kernel, head_norm=head_norm),
        grid=(t // tm,),
        in_specs=[row(D_MODEL), row(BR_WIDTH), row(BR_WIDTH), _time_major_spec(tm, seq), row(BR_WIDTH)]
                 + extra_specs + [_full((1, D_MODEL)), _full((D_MODEL, N_BRANCH * D_MODEL)),
                                  _full((N_BRANCH, BR_WIDTH, D_MODEL)), _full((D_MODEL, D_MODEL))],
        out_specs=row(D_MODEL),
        out_shape=jax.ShapeDtypeStruct((t, D_MODEL), F32),
        compiler_params=_cparams("parallel"),
        name="merge",
    )(h, oa, ob, oc, od, *extra, nw, wm, wb, wo)


def _ffn_kernel(h_ref, nw_ref, wg_ref, wu_ref, wd_ref, fw_ref, out_ref, *, final):
    h = h_ref[...]
    hn = _rms(h, nw_ref[...]).astype(BF16)
    n_tiles = D_FF // FF_TILE
    cols = [slice(c * FF_TILE, (c + 1) * FF_TILE) for c in range(n_tiles)]
    up = lambda sl: (_dot(hn, wg_ref[:, sl]), _dot(hn, wu_ref[:, sl]))
    acc = jnp.zeros(h.shape, F32)
    nxt = up(cols[0])
    for c in range(n_tiles):
        g, u = nxt
        if c + 1 < n_tiles:
            nxt = up(cols[c + 1])
        acc = acc + _dot(_silu(g) * u, wd_ref[cols[c], :])
    hnew = h + acc
    out_ref[...] = _rms(hnew, fw_ref[...]) if final else hnew


def _ffn(h, nw, wg, wu, wd, fw, tm, final):
    t = h.shape[0]
    row = pl.BlockSpec((tm, D_MODEL), lambda i: (i, 0))
    return pl.pallas_call(
        functools.partial(_ffn_kernel, final=final),
        grid=(t // tm,),
        in_specs=[row, _full((1, D_MODEL)), _full((D_MODEL, D_FF)), _full((D_MODEL, D_FF)),
                  _full((D_FF, D_MODEL)), _full((1, D_MODEL))],
        out_specs=row,
        out_shape=jax.ShapeDtypeStruct((t, D_MODEL), F32),
        compiler_params=_cparams("parallel"),
        name="ffn",
    )(h, nw, wg, wu, wd, fw)


def _run_skewed(streams, skew):
    results = [None] * len(streams)
    live, step = [], 0
    pending = list(enumerate(streams))
    while pending or live:
        while pending and pending[0][0] * skew <= step:
            live.append(pending.pop(0))
        still = []
        for idx, gen in live:
            try:
                next(gen)
                still.append((idx, gen))
            except StopIteration as done:
                results[idx] = done.value
        live = still
        step += 1
    return results


def _gla_stream(load, st, m, tt):
    each = lambda f, *xs: [f(*a) for a in zip(*xs)]
    blk = m["block16"]
    b_all = _dot_xr(_chunk_tri(tt), load(4, slice(0, tt)), 3)
    yield
    rows = [slice(c * CHUNK, (c + 1) * CHUNK) for c in range(tt // CHUNK)]
    q, k, v = ([load(j, r) for r in rows] for j in range(3))
    b = [b_all[r, :] for r in rows]
    b_last = each(lambda x: x[CHUNK - 1:CHUNK, :], b)
    scores = []
    for i0 in range(0, CHUNK, SUB):
        q_i = each(lambda qq, x: qq[i0:i0 + SUB, :] * jnp.exp(x[i0:i0 + SUB, :] - x[i0:i0 + 1, :]), q, b)
        k_i = each(lambda kk, x: kk * jnp.exp(jnp.minimum(x[i0:i0 + 1, :] - x, EXP_CAP)), k, b)
        yield
        scores.append(each(lambda a, c: _dot_nt(a, _blockdiag(c, blk)), q_i, k_i))
    kv = each(lambda vv, kk, x, xl: jnp.where(m["block"], _dot_tn(vv, kk * jnp.exp(xl - x)), 0.0), v, k, b, b_last)
    yield
    attn = each(lambda *r: jnp.where(m["causal"], jnp.concatenate(r, axis=0), 0.0), *scores)
    qe = each(lambda qq, x: qq * jnp.exp(x), q, b)
    yield
    o_loc = each(lambda a, vv: _dot(a, _blockdiag(vv, blk)), attn, v)
    yield
    outs = []
    for c in range(len(q)):
        outs.append(_dot_nt(qe[c], st) + o_loc[c])
        st = st * jnp.exp(b_last[c]) + kv[c]
        yield
    return outs, st


def _write_state(out_ref, i, st):
    r = _iota2((BR_WIDTH, BR_WIDTH), 0)
    c = _iota2((BR_WIDTH, BR_WIDTH), 1)
    eye = jnp.where(r == c, 1.0, 0.0).astype(BF16)
    nt_dot = lambda p: lax.dot_general(eye, p, (((1,), (1,)), ((), ())), preferred_element_type=F32)
    st_t = sum(nt_dot(p) for p in _split(st, 3))
    for h in range(N_HEADS):
        out_ref[i, h] = st_t[h * HEAD_DIM:(h + 1) * HEAD_DIM, h * HEAD_DIM:(h + 1) * HEAD_DIM]


def _gla_prep(pb, gkw_ref, gkb_ref):
    w = BR_WIDTH
    lf = _log_sigmoid(_dot(pb[:, 4 * w:4 * w + LANE], gkw_ref[...]) + gkb_ref[...]) * (1.0 / GLA_GATE_NORM)
    return pb[:, 0:w] * HEAD_DIM ** -0.5, pb[:, w:2 * w], pb[:, 2 * w:3 * w], _silu(pb[:, 3 * w:4 * w]), lf


def _hgrn_prep(pd, lb_ref):
    w = BR_WIDTH
    lb = lb_ref[...]
    xf = pd[:, w:2 * w]
    return (_silu(pd[:, 0:w]) * HEAD_DIM ** -0.5, (1.0 - lb) * _sigmoid(-xf), pd[:, 2 * w:3 * w],
            _silu(pd[:, 3 * w:4 * w]), _hgrn_log_forget(xf, lb))


def _linear_prompt_kernel(p_ref, o_ref, sout_ref, st_s, *, tt):
    @pl.when(pl.program_id(1) == 0)
    def _():
        st_s[...] = jnp.zeros(st_s.shape, F32)

    w = BR_WIDTH
    m = _head_masks()
    n_seq = p_ref.shape[0]
    load = lambda i: (lambda j, r: p_ref[i, r, j * w:(j + 1) * w])
    results = _run_skewed([_gla_stream(load(i), st_s[i], m, tt) for i in range(n_seq)], STREAM_SKEW)
    for i, (outs, st) in enumerate(results):
        st_s[i] = st
        o_ref[i] = jnp.concatenate(outs, axis=0)

    @pl.when(pl.program_id(1) == pl.num_programs(1) - 1)
    def _():
        for i in range(n_seq):
            _write_state(sout_ref, i, st_s[i])


def _hgrn_log_forget(xf, lb):
    ls_pos = _log_sigmoid(xf)
    pos = lb > 0
    lb_safe = jnp.where(pos, lb, 1.0)
    mixed = ls_pos + _softplus(jnp.log(lb_safe) - xf)
    return jnp.where(pos, mixed, ls_pos)


def _mixer_call(kernel, p, bsz, seq, tt, name):
    nt = seq // tt
    width = p.shape[1]
    n_seq = SEQ_PER_STEP if bsz % SEQ_PER_STEP == 0 else 1
    o, st = pl.pallas_call(
        functools.partial(kernel, tt=tt),
        grid=(bsz // n_seq, nt),
        in_specs=[pl.BlockSpec((n_seq, tt, width), lambda b, t: (b, t, 0))],
        out_specs=[pl.BlockSpec((n_seq, tt, BR_WIDTH), lambda b, t: (b, t, 0)),
                   pl.BlockSpec((n_seq, N_HEADS, HEAD_DIM, HEAD_DIM), lambda b, t: (b, 0, 0, 0))],
        out_shape=[jax.ShapeDtypeStruct((bsz, seq, BR_WIDTH), F32),
                   jax.ShapeDtypeStruct((bsz, N_HEADS, HEAD_DIM, HEAD_DIM), F32)],
        scratch_shapes=[pltpu.VMEM((n_seq, BR_WIDTH, BR_WIDTH), F32)],
        compiler_params=_cparams("parallel", "arbitrary"),
        name=name,
    )(p.reshape(bsz, seq, width))
    return o.reshape(bsz * seq, BR_WIDTH), st


def _mm_split(a, b, tile_mask):
    a_hi, a_lo = _split(a, 2)
    b_hi, b_lo = (jnp.concatenate([p] * (BR_WIDTH // b.shape[0]), axis=0) * tile_mask for p in _split(b, 2))
    dot = functools.partial(jnp.dot, preferred_element_type=F32)
    return dot(a_hi, b_hi) + (dot(a_lo, b_hi) + dot(a_hi, b_lo))


def _unit_lower_inverse(n, m):
    each = lambda f, *xs: [f(*a) for a in zip(*xs)]
    eye_c, sub, blk = m["eye_sub"], m["sub16"], m["block16"]
    nd = each(lambda a: jnp.where(m["same_sub"], a, 0.0), n)
    low = each(lambda a, d: a - d, n, nd)
    c = each(lambda d: d[0:SUB] + d[SUB:2 * SUB] + d[2 * SUB:3 * SUB] + d[3 * SUB:4 * SUB], nd)
    p = each(lambda a: _mm_split(a, a, sub), c)
    x = each(lambda a: eye_c - a, c)
    yield
    for _ in range(2):
        xp = each(lambda a, b: _mm_split(jnp.concatenate([a, b], axis=0), b, sub), x, p)
        x = each(lambda a, b: a + b[0:SUB], x, xp)
        p = each(lambda b: b[SUB:2 * SUB], xp)
        yield
    x = each(lambda a, b: a + _mm_split(a, b, sub), x, p)
    dinv = each(lambda a: jnp.where(m["same_sub"], jnp.concatenate([a] * (CHUNK // SUB), axis=0), 0.0), x)
    yield
    mm = each(lambda a, b: _mm_split(a, b, blk), dinv, low)
    yield
    m2 = each(lambda a: _mm_split(a, a, blk), mm)
    yield
    e = each(lambda d, a: d + _mm_split(a, d, blk), dinv, m2)
    yield
    return each(lambda a, b: b - _mm_split(a, b, blk), mm, e)


def _gdn_stream(load, st, m, tt):
    each = lambda f, *xs: [f(*a) for a in zip(*xs)]
    w = BR_WIDTH
    blk = m["block16"]
    gam_all = _dot_xr(_chunk_tri(tt), load(4, slice(0, tt)), 3)
    yield
    lane_j = _iota2((tt, w), 1) & (CHUNK - 1)
    row_j = _iota2((tt, w), 0) & (CHUNK - 1)
    gam_t = _dot_xr(_chunk_ones(tt), jnp.where(lane_j == row_j, gam_all, 0.0), 3)
    yield
    rows = [slice(c * CHUNK, (c + 1) * CHUNK) for c in range(tt // CHUNK)]
    q, k, v, bexp = ([load(j, r) for r in rows] for j in (0, 1, 2, 5))
    gam_i = [gam_all[r, :] for r in rows]
    gam_j = [gam_t[r, :] for r in rows]
    decay = each(lambda gi, gj: jnp.where(m["causal"], jnp.exp(jnp.minimum(gi - gj, 0.0)), 0.0), gam_i, gam_j)
    scores = each(lambda kk, qq: _dot_nt(jnp.concatenate([kk, qq], axis=0), _blockdiag(kk, blk)), k, q)
    yield
    attn = each(lambda s, d: s[CHUNK:2 * CHUNK] * d, scores, decay)
    n = each(lambda b, s, d: jnp.where(m["strict"], b * s[0:CHUNK] * d, 0.0), bexp, scores, decay)
    tinv = yield from _unit_lower_inverse(n, m)
    yield
    u_v = each(lambda t, b, vv: _dot(t, _blockdiag(b * vv, blk)), tinv, bexp, v)
    w_k = each(lambda t, b, gi, kk: _dot(t, _blockdiag(b * jnp.exp(gi) * kk, blk)), tinv, bexp, gam_i, k)
    yield
    outs = []
    for c in range(len(rows)):
        gam_last = gam_i[c][CHUNK - 1:CHUNK, :]
        ws = _dot_nt(jnp.concatenate([w_k[c], q[c]], axis=0), st)
        u = u_v[c] - ws[0:CHUNK]
        yield
        outs.append(jnp.exp(gam_i[c]) * ws[CHUNK:2 * CHUNK] + _dot(attn[c], _blockdiag(u, blk)))
        kd = k[c] * jnp.exp(gam_last - gam_i[c])
        st = st * jnp.exp(gam_last) + jnp.where(m["block"], _dot_tn(u, kd), 0.0)
        yield
    return outs, st


def _l2norm_heads(x, ones_blk):
    return x * lax.rsqrt(_dot_xl(x * x, ones_blk, 2) + EPS)


def _gdn_prep(a_alpha_beta, alog_ref, dtb_ref):
    r = _iota2((LANE, BR_WIDTH), 0)
    c = _iota2((LANE, BR_WIDTH), 1) >> HEAD_SHIFT
    sel_a = jnp.where(r == c, 1.0, 0.0).astype(BF16)
    sel_b = jnp.where(r == c + N_HEADS, 1.0, 0.0).astype(BF16)
    alpha = _dot_xl(a_alpha_beta, sel_a, 3)
    beta = _sigmoid(_dot_xl(a_alpha_beta, sel_b, 3))
    g = -jnp.exp(alog_ref[...]) * _softplus(alpha + dtb_ref[...])
    return g, beta


def _gdn_conv_qkv(raw, hist, cw_ref):
    w = BR_WIDTH
    rows = raw.shape[0]
    xp = jnp.concatenate([hist, raw], axis=0)
    y = raw * cw_ref[CONV_W - 1:CONV_W, :]
    for j in range(CONV_W - 1):
        start = SUBLANE - (CONV_W - 1) + j
        y = y + xp[start:start + rows, :] * cw_ref[j:j + 1, :]
    qkv = _silu(y)
    ones_blk = _block_ones()
    return (_l2norm_heads(qkv[:, 0:w], ones_blk) * HEAD_DIM ** -0.5, _l2norm_heads(qkv[:, w:2 * w], ones_blk),
            qkv[:, 2 * w:3 * w])


def _gdn_prompt_kernel(p_ref, o_ref, sout_ref, st_s, *, tt):
    t = pl.program_id(1)

    @pl.when(t == 0)
    def _():
        st_s[...] = jnp.zeros(st_s.shape, F32)

    w = BR_WIDTH
    m = _head_masks()
    n_seq = p_ref.shape[0]
    load = lambda i: (lambda j, r: p_ref[i, r, j * w:(j + 1) * w])
    results = _run_skewed([_gdn_stream(load(i), st_s[i], m, tt) for i in range(n_seq)], STREAM_SKEW)
    for i, (outs, st) in enumerate(results):
        st_s[i] = st
        o_ref[i] = jnp.concatenate(outs, axis=0)

    @pl.when(t == pl.num_programs(1) - 1)
    def _():
        for i in range(n_seq):
            _write_state(sout_ref, i, st_s[i])


def _s5_kernel(u_ref, x0r_ref, x0i_ref, ar_ref, ai_ref, br_ref, bi_ref, cr_ref, ci_ref, d_ref, glu_ref,
               o_ref, xr_ref, xi_ref, sr_s, si_s, *, steps, rows):
    @pl.when(pl.program_id(0) == 0)
    def _():
        xr_ref[...] = x0r_ref[...]
        xi_ref[...] = x0i_ref[...]

    u = u_ref[...]
    ub = u.astype(BF16)
    sr_s[...] = _dot(ub, br_ref[...])
    si_s[...] = _dot(ub, bi_ref[...])
    a_re = jnp.broadcast_to(ar_ref[...], (rows, S5_STATE))
    a_im = jnp.broadcast_to(ai_ref[...], (rows, S5_STATE))

    def body(t, carry):
        xr, xi = carry
        r = pl.ds(pl.multiple_of(t * rows, rows), rows)
        nr = (sr_s[r, :] + a_re * xr) - a_im * xi
        ni = (si_s[r, :] + a_re * xi) + a_im * xr
        sr_s[r, :] = nr
        si_s[r, :] = ni
        return nr, ni

    xr, xi = lax.fori_loop(0, steps, body, (xr_ref[...], xi_ref[...]), unroll=min(steps, S5_UNROLL))
    xr_ref[...] = xr
    xi_ref[...] = xi
    y = _dot(sr_s[...].astype(BF16), cr_ref[...]) - _dot(si_s[...].astype(BF16), ci_ref[...]) + d_ref[...] * u
    hg = _dot(_gelu(y).astype(BF16), glu_ref[...])
    o_ref[...] = hg[:, 0:BR_WIDTH] * _sigmoid(hg[:, BR_WIDTH:2 * BR_WIDTH])


def _s5(u, x0r, x0i, params, steps_total, rows, steps):
    nt = steps_total // steps
    tile = steps * rows
    return pl.pallas_call(
        functools.partial(_s5_kernel, steps=steps, rows=rows),
        grid=(nt,),
        in_specs=[pl.BlockSpec((tile, BR_WIDTH), lambda t: (t, 0)), _full(x0r.shape), _full(x0i.shape)]
                 + [_full(a.shape) for a in params],
        out_specs=[pl.BlockSpec((tile, BR_WIDTH), lambda t: (t, 0)), _full(x0r.shape), _full(x0i.shape)],
        out_shape=[jax.ShapeDtypeStruct((steps_total * rows, BR_WIDTH), F32),
                   jax.ShapeDtypeStruct(x0r.shape, F32), jax.ShapeDtypeStruct(x0i.shape, F32)],
        scratch_shapes=[pltpu.VMEM((tile, S5_STATE), F32)] * 2,
        compiler_params=_cparams("arbitrary"),
        name="s5",
    )(u, x0r, x0i, *params)


def _s5_params(lam_re, lam_im, b_re, b_im, c_re, c_im, d, log_dt, glu_w):
    lr = jnp.minimum(lam_re, -1e-4)
    li = lam_im
    dt = jnp.exp(log_dt)[:, None]
    mag = jnp.exp(lr * dt)
    ab_re, ab_im = mag * jnp.cos(li * dt), mag * jnp.sin(li * dt)
    den = lr * lr + li * li
    z_re = ((ab_re - 1.0) * lr + ab_im * li) / den
    z_im = (ab_im * lr - (ab_re - 1.0) * li) / den
    bb_re = z_re[..., None] * b_re - z_im[..., None] * b_im
    bb_im = z_re[..., None] * b_im + z_im[..., None] * b_re
    eye = jnp.eye(S5_GROUPS, dtype=F32)

    def pack_in(bb):
        return jnp.einsum('gph,gk->ghkp', bb, eye).reshape(BR_WIDTH, S5_STATE).astype(BF16)

    def pack_out(c):
        return jnp.einsum('ghp,gk->gpkh', c, eye).reshape(S5_STATE, BR_WIDTH).astype(BF16)

    return (ab_re.reshape(1, S5_STATE), ab_im.reshape(1, S5_STATE), pack_in(bb_re), pack_in(bb_im),
            pack_out(c_re), pack_out(c_im), d.reshape(1, BR_WIDTH), glu_w.astype(BF16))


def _expand_mats():
    r = _iota2((HEAD_DIM, HEAD_DIM * HEAD_DIM), 0)
    c = _iota2((HEAD_DIM, HEAD_DIM * HEAD_DIM), 1)
    rep_k = jnp.where(r == (c >> HEAD_SHIFT), 1.0, 0.0).astype(BF16)
    rep_v = jnp.where(r == (c & (HEAD_DIM - 1)), 1.0, 0.0).astype(BF16)
    return rep_k, rep_v


def _gla_decode_kernel(q_ref, k_ref, v_ref, lf_ref, s_ref, o_ref, sn_ref):
    rep_k, rep_v = _expand_mats()
    q, k, v, lf = q_ref[0, 0], k_ref[0, 0], v_ref[0, 0], lf_ref[0, 0]
    s = s_ref[...]
    dec = jnp.exp(lf)
    qe_x = _dot_xl(q * dec, rep_k, 1)
    o = _dot_nt_xl(qe_x * s, rep_v, 2) + jnp.sum(q * k, axis=-1, keepdims=True) * v
    sn_ref[...] = _dot_xl(dec, rep_k, 3) * s + _dot_xl(k, rep_k, 1) * _dot_xl(v, rep_v, 1)
    o_ref[0] = o


def _gdn_decode_kernel(q_ref, k_ref, v_ref, g_ref, b_ref, s_ref, o_ref, sn_ref):
    rep_k, rep_v = _expand_mats()
    q, k, v = q_ref[0, 0], k_ref[0, 0], v_ref[0, 0]
    eg = jnp.exp(g_ref[0, 0])
    beta = b_ref[0, 0]
    s = s_ref[...]
    k_x = _dot_xl(k, rep_k, 1)
    ks = _dot_nt_xl(k_x * s, rep_v, 2)
    qs = _dot_nt_xl(_dot_xl(q, rep_k, 1) * s, rep_v, 2)
    u = beta * v - (beta * eg) * ks
    o_ref[0] = eg * qs + jnp.sum(q * k, axis=-1, keepdims=True) * u
    sn_ref[...] = eg * s + k_x * _dot_xl(u, rep_v, 1)


def _decode_call(kernel, packs, state, name):
    rows = state.shape[0]
    hw = HEAD_DIM * HEAD_DIM
    vec_specs, vecs = [], []
    for a in packs:
        for j in range(a.shape[0]):
            vec_specs.append(pl.BlockSpec((1, 1, rows, a.shape[3]), lambda h, j=j: (j, h, 0, 0)))
            vecs.append(a)
    return pl.pallas_call(
        kernel,
        grid=(N_HEADS,),
        in_specs=vec_specs + [pl.BlockSpec((rows, hw), lambda h: (0, h))],
        out_specs=[pl.BlockSpec((1, rows, HEAD_DIM), lambda h: (h, 0, 0)), pl.BlockSpec((rows, hw), lambda h: (0, h))],
        out_shape=[jax.ShapeDtypeStruct((N_HEADS, rows, HEAD_DIM), F32), jax.ShapeDtypeStruct(state.shape, F32)],
        compiler_params=_cparams("parallel"),
        name=name,
    )(*vecs, state)


def _put_heads(ref, j, x):
    for h in range(N_HEADS):
        ref[j, h] = x[:, h * HEAD_DIM:(h + 1) * HEAD_DIM]


def _decode_prep_kernel(pa_ref, pb_ref, pd_ref, conv_ref, cw_ref, alog_ref, dtb_ref, gkw_ref, gkb_ref, lb_ref,
                        a_ref, ag_ref, b_ref, d_ref, convn_ref):
    w = BR_WIDTH
    cw = 3 * w
    ones_blk = _block_ones()
    raw = pa_ref[:, 0:cw]
    y = raw * cw_ref[CONV_W - 1:CONV_W, :]
    for j in range(CONV_W - 1):
        y = y + conv_ref[:, j * cw:(j + 1) * cw] * cw_ref[j:j + 1, :]
    convn_ref[:, 0:(CONV_W - 2) * cw] = conv_ref[:, cw:(CONV_W - 1) * cw]
    convn_ref[:, (CONV_W - 2) * cw:(CONV_W - 1) * cw] = raw
    qkv = _silu(y)
    _put_heads(a_ref, 0, _l2norm_heads(qkv[:, 0:w], ones_blk) * HEAD_DIM ** -0.5)
    _put_heads(a_ref, 1, _l2norm_heads(qkv[:, w:2 * w], ones_blk))
    _put_heads(a_ref, 2, qkv[:, 2 * w:3 * w])
    g, beta = _gdn_prep(pa_ref[:, 4 * w:4 * w + LANE], alog_ref, dtb_ref)
    for h in range(N_HEADS):
        ag_ref[0, h] = g[:, h * HEAD_DIM:h * HEAD_DIM + 1]
        ag_ref[1, h] = beta[:, h * HEAD_DIM:h * HEAD_DIM + 1]
    for ref, (q, k, v, _, lf) in ((b_ref, _gla_prep(pb_ref[...], gkw_ref, gkb_ref)),
                                  (d_ref, _hgrn_prep(pd_ref[...], lb_ref))):
        for j, val in enumerate((q, k, v, lf)):
            _put_heads(ref, j, val)


def _decode_prep(pa, pb, pd, conv, params):
    rows = pa.shape[0]
    ins = (pa, pb, pd, conv) + tuple(params)
    per_head = lambda n, width: (n, N_HEADS, rows, width)
    shapes = [per_head(3, HEAD_DIM), per_head(2, 1), per_head(4, HEAD_DIM), per_head(4, HEAD_DIM), conv.shape]
    return pl.pallas_call(
        _decode_prep_kernel,
        grid=(1,),
        in_specs=[_full(a.shape) for a in ins],
        out_specs=[_full(s) for s in shapes],
        out_shape=[jax.ShapeDtypeStruct(s, F32) for s in shapes],
        compiler_params=_cparams("arbitrary"),
        name="decode_prep",
    )(*ins)


def _decode_post_kernel(oa_ref, ob_ref, od_ref, pa_ref, pb_ref, pd_ref, nwa_ref, nwb_ref, nwd_ref,
                        a_ref, b_ref, d_ref):
    w = BR_WIDTH
    ones_blk = _block_ones()
    heads = lambda ref: jnp.concatenate([ref[h] for h in range(N_HEADS)], axis=1)
    for out, o, p, nw in ((a_ref, oa_ref, pa_ref, nwa_ref), (b_ref, ob_ref, pb_ref, nwb_ref),
                          (d_ref, od_ref, pd_ref, nwd_ref)):
        out[...] = _head_norm_gate(heads(o), _silu(p[:, 3 * w:4 * w]), nw[...], ones_blk)


def _decode_post(oa, ob, od, pa, pb, pd, nwa, nwb, nwd):
    ins = (oa, ob, od, pa, pb, pd, nwa, nwb, nwd)
    shp = (oa.shape[1], BR_WIDTH)
    return pl.pallas_call(
        _decode_post_kernel,
        grid=(1,),
        in_specs=[_full(a.shape) for a in ins],
        out_specs=[_full(shp)] * 3,
        out_shape=[jax.ShapeDtypeStruct(shp, F32)] * 3,
        compiler_params=_cparams("arbitrary"),
        name="decode_post",
    )(*ins)


def _pack_w_in(w_in):
    sizes = (3 * BR_WIDTH, N_HEADS, N_HEADS, BR_WIDTH, BR_WIDTH, BR_WIDTH, BR_WIDTH, GLA_GATE_RANK, BR_WIDTH,
             BR_WIDTH, BR_WIDTH, BR_WIDTH, BR_WIDTH, BR_WIDTH, N_BRANCH * D_MODEL)
    w_in = w_in.astype(BF16)
    parts, off = [], 0
    for n in sizes:
        parts.append(w_in[..., off:off + n])
        off += n
    (a_qkv, a_alpha, a_beta, a_gate, b_q, b_k, b_v, b_gk, b_gate, c_u, d_q, d_f, d_i, d_gate, merge) = parts
    zpad = lambda n: jnp.zeros(w_in.shape[:-1] + (n,), BF16)
    mix = jnp.concatenate([a_qkv, a_gate, a_alpha, a_beta, zpad(LANE - 2 * N_HEADS),
                           b_q, b_k, b_v, b_gate, b_gk, zpad(LANE - GLA_GATE_RANK),
                           c_u, d_q, d_f, d_i, d_gate], axis=-1)
    return mix, merge


def _tile_sizes(seq):
    pick = lambda n: n if seq % n == 0 else seq
    return pick(512), pick(256), pick(128)


def _hgrn_lower_bounds(logits):
    p = jax.nn.softmax(logits, axis=0)
    return jnp.cumsum(p, axis=0) - p[0]


def _row(x):
    return x.reshape(1, -1)


def _rep_heads(x):
    return jnp.repeat(x, HEAD_DIM).reshape(1, BR_WIDTH)


def kernel(x_prompt, x_sample, state_gdn_conv, state_gdn, state_gla, state_s5_re, state_s5_im, state_hgrn, norm1_w, w_in, gdn_conv_w, gdn_a_log, gdn_dt_bias, gdn_norm_w, gla_gk_w, gla_gk_b, gla_norm_w, s5_lambda_re, s5_lambda_im, s5_b_re, s5_b_im, s5_c_re, s5_c_im, s5_d, s5_log_dt, s5_glu_w, hgrn_lb_logits, hgrn_norm_w, w_branch, w_out, norm2_w, ffn_w_gate, ffn_w_up, ffn_w_down, final_norm_w):
    bsz, seq, _ = x_prompt.shape
    dbs = x_sample.shape[0]
    tm_p, tt, s5_steps = _tile_sizes(seq)
    lb_all = _hgrn_lower_bounds(hgrn_lb_logits)
    w_mix_all, w_merge_all = _pack_w_in(w_in)
    gkw_all = jnp.concatenate([gla_gk_w, jnp.zeros((DEPTH, LANE - GLA_GATE_RANK, BR_WIDTH), F32)], axis=1)
    wb_all, wo_all = w_branch.astype(BF16), w_out.astype(BF16)
    wg_all, wu_all, wd_all = ffn_w_gate.astype(BF16), ffn_w_up.astype(BF16), ffn_w_down.astype(BF16)
    hp = x_prompt.reshape(bsz * seq, D_MODEL)
    hs = x_sample.reshape(dbs, D_MODEL)
    zeros_p = jnp.zeros((bsz, S5_STATE), F32)
    fw = _row(final_norm_w)
    outs_p = [[] for _ in range(6)]
    outs_s = [[] for _ in range(6)]
    for i in range(DEPTH):
        final = i == DEPTH - 1
        w_mix, w_merge, gkw = w_mix_all[i], w_merge_all[i], gkw_all[i]
        nw1, nw2 = _row(norm1_w[i]), _row(norm2_w[i])
        gdn_params = (gdn_conv_w[i], _rep_heads(gdn_a_log[i]), _rep_heads(gdn_dt_bias[i]), _row(gdn_norm_w[i]))
        gla_params = (gkw, _row(gla_gk_b[i]), _row(gla_norm_w[i]))
        hgrn_params = (_row(lb_all[i]), _row(hgrn_norm_w[i]))
        s5_params = _s5_params(s5_lambda_re[i], s5_lambda_im[i], s5_b_re[i], s5_b_im[i], s5_c_re[i], s5_c_im[i],
                               s5_d[i], s5_log_dt[i], s5_glu_w[i])
        prep_params = (gdn_conv_w[i], gdn_params[1], gdn_params[2], gkw, gla_params[1], hgrn_params[0])
        wb, wo, wg, wu, wd = wb_all[i], wo_all[i], wg_all[i], wu_all[i], wd_all[i]

        ga, gb, pc, gd, conv_p = _in_proj_prompt(hp, nw1, w_mix, prep_params, tm_p, seq)
        o_a, st_a = _mixer_call(_gdn_prompt_kernel, ga, bsz, seq, tt, "gdn_prompt")
        o_b, st_b = _mixer_call(_linear_prompt_kernel, gb, bsz, seq, tt, "gla_prompt")
        o_d, st_d = _mixer_call(_linear_prompt_kernel, gd, bsz, seq, tt, "hgrn_prompt")
        o_c, xr_p, xi_p = _s5(pc.reshape(seq * bsz, BR_WIDTH), zeros_p, zeros_p, s5_params, seq, bsz, s5_steps)
        hp = _merge(hp, o_a, o_b, o_c.reshape(seq, bsz * BR_WIDTH), o_d, nw1, w_merge, wb, wo, tm_p, seq,
                    prepared=(ga, gb, gd), gains=(gdn_params[3], gla_params[2], hgrn_params[1]))
        hp = _ffn(hp, nw2, wg, wu, wd, fw, tm_p, final)
        for lst, s in zip(outs_p, (conv_p, st_a, st_b, xr_p.reshape(bsz, S5_GROUPS, S5_P),
                                   xi_p.reshape(bsz, S5_GROUPS, S5_P), st_d)):
            lst.append(s)

        pa, pb, pc, pd = _in_proj(hs, nw1, w_mix, dbs, dbs)
        va, vg, vb, vd, conv_s = _decode_prep(
            pa, pb, pd, state_gdn_conv[i].reshape(dbs, (CONV_W - 1) * 3 * BR_WIDTH), prep_params)
        flat = lambda s: s.reshape(dbs, N_HEADS * HEAD_DIM * HEAD_DIM)
        oa_h, sa = _decode_call(_gdn_decode_kernel, [va, vg], flat(state_gdn[i]), "gdn_decode")
        ob_h, sb = _decode_call(_gla_decode_kernel, [vb], flat(state_gla[i]), "gla_decode")
        od_h, sd = _decode_call(_gla_decode_kernel, [vd], flat(state_hgrn[i]), "hgrn_decode")
        o_a, o_b, o_d = _decode_post(oa_h, ob_h, od_h, pa, pb, pd, gdn_params[3], gla_params[2], hgrn_params[1])
        o_c, xr_s, xi_s = _s5(pc, state_s5_re[i].reshape(dbs, S5_STATE), state_s5_im[i].reshape(dbs, S5_STATE),
                              s5_params, 1, dbs, 1)
        hs = _merge(hs, o_a, o_b, o_c, o_d, nw1, w_merge, wb, wo, dbs, dbs)
        hs = _ffn(hs, nw2, wg, wu, wd, fw, dbs, final)
        st5 = lambda s: s.reshape(dbs, N_HEADS, HEAD_DIM, HEAD_DIM)
        for lst, s in zip(outs_s, (conv_s.reshape(dbs, CONV_W - 1, 3 * BR_WIDTH), st5(sa), st5(sb),
                                   xr_s.reshape(dbs, S5_GROUPS, S5_P), xi_s.reshape(dbs, S5_GROUPS, S5_P), st5(sd))):
            lst.append(s)

    return (hp.reshape(bsz, seq, D_MODEL), hs.reshape(dbs, 1, D_MODEL),
            *[jnp.stack(l) for l in outs_p], *[jnp.stack(l) for l in outs_s])
```
